```python
import jax, jax.numpy as jnp
from jax import lax
import numpy as np

D_MODEL = 1024
BATCH = 32
SEQ = 256
DEPTH = 2
DEC_BATCH = 4
DEC_SEQ = 1024
PAST_LEN = 512

GRID_W = 64
HEAD_DIM = 64
N_HEADS = D_MODEL // HEAD_DIM
KV_HEADS = N_HEADS // 4
ATT_WIDTH = N_HEADS * HEAD_DIM
KV_WIDTH = KV_HEADS * HEAD_DIM
ROPE_FREQS = HEAD_DIM // 4
ROPE_THETA = 10000.0
Q_BLOCK = 128
GLA_HEADS = 4
GLA_DK = D_MODEL // 2 // GLA_HEADS
GLA_DV = D_MODEL // GLA_HEADS
GLA_QK_WIDTH = GLA_HEADS * GLA_DK
GLA_V_WIDTH = GLA_HEADS * GLA_DV
GATE_RANK = 16
GATE_NORM = 16.0
GLA_CHUNK = 64
N_KEYS = 128
N_EXPERTS = N_KEYS * N_KEYS
PEER_HEADS = 8
PEER_TOPK = 16
PEER_QDIM = 256
PEER_HALF = PEER_QDIM // 2
PEER_BLOCK = 128
ALPHA = (2.0 * DEPTH) ** 0.25
BETA = (8.0 * DEPTH) ** -0.25
LN_EPS = 1e-5
RMS_EPS = 1e-6
IN_SIZES = (ATT_WIDTH, KV_WIDTH, KV_WIDTH, GLA_QK_WIDTH, GLA_QK_WIDTH, GLA_V_WIDTH, GLA_V_WIDTH, 2 * GATE_RANK, 2 * D_MODEL)
IN_SPLITS = tuple(int(s) for s in np.cumsum(IN_SIZES)[:-1])
IN_WIDTH = int(sum(IN_SIZES))

kernel_name = 'hybrid_diffusion_gqa_gla_peer_step'


def rms_norm(x, g):
    xf = x.astype(jnp.float32)
    out = xf * lax.rsqrt(jnp.mean(xf * xf, axis=-1, keepdims=True) + RMS_EPS)
    return out.astype(x.dtype) * g


def layer_norm(x, g, b):
    xf = x.astype(jnp.float32)
    mu = jnp.mean(xf, axis=-1, keepdims=True)
    var = jnp.mean(jnp.square(xf - mu), axis=-1, keepdims=True)
    return ((xf - mu) * lax.rsqrt(var + LN_EPS)).astype(x.dtype) * g + b


def axial_rope(L, dtype):
    rows = L // GRID_W
    r = jnp.repeat(jnp.arange(rows, dtype=jnp.float32), GRID_W)
    col = jnp.tile(jnp.arange(GRID_W, dtype=jnp.float32), rows)
    inv = ROPE_THETA ** (-jnp.arange(ROPE_FREQS, dtype=jnp.float32) / ROPE_FREQS)
    ang = jnp.stack([r[:, None] * inv, col[:, None] * inv], axis=1)
    return jnp.cos(ang).astype(dtype), jnp.sin(ang).astype(dtype)


def apply_rope(x, cos, sin):
    B, L, H, _ = x.shape
    xr = x.reshape(B, L, H, 2, 2, ROPE_FREQS)
    x1, x2 = xr[..., 0, :], xr[..., 1, :]
    c, s = cos[None, :, None], sin[None, :, None]
    out = jnp.stack([x1 * c - x2 * s, x2 * c + x1 * s], axis=-2)
    return out.reshape(B, L, H, HEAD_DIM)


def gqa_attend(q, k, v):
    B, Lq, H, D = q.shape
    G = k.shape[2]
    R = H // G
    nb = Lq // Q_BLOCK
    qb = q.reshape(B, nb, Q_BLOCK, G, R, D).transpose(1, 0, 2, 3, 4, 5)

    def one_block(qblk):
        s = jnp.einsum('bqgrd,bkgd->bgrqk', qblk, k).astype(jnp.float32) * (D ** -0.5)
        p = jax.nn.softmax(s, axis=-1).astype(v.dtype)
        return jnp.einsum('bgrqk,bkgd->bqgrd', p, v)

    o = lax.map(one_block, qb)
    return o.transpose(1, 0, 2, 3, 4, 5).reshape(B, Lq, H * D)


def gla_scan(q, k, v, log_a, s0):
    B, L, H, _ = q.shape
    DV = v.shape[-1]
    n = L // GLA_CHUNK

    def chunks(t):
        return t.astype(jnp.float32).reshape(B, n, GLA_CHUNK, H, -1).transpose(1, 0, 3, 2, 4)

    qc, kc, vc, ac = chunks(q), chunks(k), chunks(v), chunks(log_a)
    bc = jnp.cumsum(ac, axis=3)
    mask = jnp.tril(jnp.ones((GLA_CHUNK, GLA_CHUNK), dtype=bool))

    def step(S, inp):
        qi, ki, vi, bi = inp
        bl = bi[:, :, -1:, :]
        q_e = qi * jnp.exp(bi)
        k_e = ki * jnp.exp(-bi)
        A = jnp.where(mask, jnp.einsum('bhid,bhjd->bhij', q_e, k_e), 0.0)
        o = jnp.einsum('bhij,bhjv->bhiv', A, vi) + jnp.einsum('bhid,bhdv->bhiv', q_e, S)
        S_new = jnp.exp(bl[:, :, 0, :])[..., None] * S + jnp.einsum('bhjd,bhjv->bhdv', ki * jnp.exp(bl - bi), vi)
        return S_new, o

    S, o = lax.scan(step, s0.astype(jnp.float32), (qc, kc, vc, bc))
    o = o.transpose(1, 0, 3, 2, 4).reshape(B, L, H, DV)
    return o.astype(v.dtype), S


def token_mixer(h, w_in, q_norm, k_norm, gate_w2, gate_b, gla_norm, w_attn_o, w_gla_o, w_out, ctx):
    B, L, _ = h.shape
    q, k, v, gq, gk, gv, gout, glr, gmerge = jnp.split(h @ w_in, IN_SPLITS, axis=-1)
    q = rms_norm(q.reshape(B, L, N_HEADS, HEAD_DIM), q_norm)
    k = rms_norm(k.reshape(B, L, KV_HEADS, HEAD_DIM), k_norm)
    v = v.reshape(B, L, KV_HEADS, HEAD_DIM)
    gq = gq.reshape(B, L, GLA_HEADS, GLA_DK) * (GLA_DK ** -0.5)
    gk = gk.reshape(B, L, GLA_HEADS, GLA_DK)
    gv = gv.reshape(B, L, GLA_HEADS, GLA_DV)
    z = jnp.einsum('bldr,drk->dblk', glr.reshape(B, L, 2, GATE_RANK), gate_w2) + gate_b[:, None, None, :]
    log_a = (jax.nn.log_sigmoid(z.astype(jnp.float32)) / GATE_NORM).reshape(2, B, L, GLA_HEADS, GLA_DK)
    if ctx is None:
        attn = gqa_attend(q, k, v)
        s0_f = jnp.zeros((B, GLA_HEADS, GLA_DK, GLA_DV), jnp.float32)
        s0_b = s0_f
    else:
        ctx_k, ctx_v, s0_f, s0_b = ctx
        cos, sin = axial_rope(L, h.dtype)
        q_r = apply_rope(q, cos, sin)
        k_r = apply_rope(k, cos, sin)
        attn = gqa_attend(q_r, jnp.concatenate([k_r, ctx_k], axis=1), jnp.concatenate([v, ctx_v], axis=1))
    o_f, s_f = gla_scan(gq, gk, gv, log_a[0], s0_f)
    o_b, s_b = gla_scan(jnp.flip(gq, 1), jnp.flip(gk, 1), jnp.flip(gv, 1), jnp.flip(log_a[1], 1), s0_b)
    o = o_f + jnp.flip(o_b, 1)
    o = (rms_norm(o, gla_norm) * jax.nn.silu(gout.reshape(B, L, GLA_HEADS, GLA_DV))).reshape(B, L, GLA_V_WIDTH)
    g_attn, g_gla = jnp.split(jax.nn.sigmoid(gmerge), 2, axis=-1)
    y = (g_attn * (attn @ w_attn_o) + g_gla * (o @ w_gla_o)) @ w_out
    if ctx is None:
        return y, (k, v, s_f, s_b)
    return y, None


def peer(h, wq, sub_keys, u, v):
    B, L, D = h.shape
    nb = (B * L) // PEER_BLOCK
    xb = h.reshape(nb, PEER_BLOCK, D)

    def one_block(xt):
        q = (xt @ wq).reshape(PEER_BLOCK, PEER_HEADS, 2, PEER_HALF)
        s = jnp.einsum('thpd,pnd->thpn', q, sub_keys).astype(jnp.float32)
        s1, i1 = lax.top_k(s[:, :, 0], PEER_TOPK)
        s2, i2 = lax.top_k(s[:, :, 1], PEER_TOPK)
        cand = (s1[..., :, None] + s2[..., None, :]).reshape(PEER_BLOCK, PEER_HEADS, PEER_TOPK * PEER_TOPK)
        sc, ci = lax.top_k(cand, PEER_TOPK)
        idx = jnp.take_along_axis(i1, ci // PEER_TOPK, axis=-1) * N_KEYS + jnp.take_along_axis(i2, ci % PEER_TOPK, axis=-1)
        g = jax.nn.softmax(sc, axis=-1)
        a = jax.nn.gelu(jnp.einsum('td,thkd->thk', xt, u[idx]).astype(jnp.float32))
        return jnp.einsum('thk,thkd->td', (g * a).astype(xt.dtype), v[idx])

    return lax.map(one_block, xb).reshape(B, L, D)


def trunk_layer(x, cond, lp, ctx):
    (ada_w, ada_b, w_in, q_norm, k_norm, gate_w2, gate_b, gla_norm, w_attn_o, w_gla_o, w_out,
     ln1_g, ln1_b, ln2_g, ln2_b, peer_wq, peer_sub_keys, peer_u, peer_v) = lp
    mod = (jax.nn.silu(cond) @ ada_w + ada_b).reshape(-1, 1, 6 * D_MODEL)
    shift1, scale1, gate1, shift2, scale2, gate2 = jnp.split(mod, 6, axis=-1)
    h = x * (1.0 + scale1) + shift1
    mix, ctx_out = token_mixer(h, w_in, q_norm, k_norm, gate_w2, gate_b, gla_norm, w_attn_o, w_gla_o, w_out, ctx)
    x = layer_norm(ALPHA * x + gate1 * mix, ln1_g, ln1_b)
    h = x * (1.0 + scale2) + shift2
    x = layer_norm(ALPHA * x + gate2 * peer(h, peer_wq, peer_sub_keys, peer_u, peer_v), ln2_g, ln2_b)
    return x, ctx_out


def setup_inputs(seed: int = 0) -> dict:
    key = jax.random.key(seed)
    ks = jax.random.split(key, 32)
    f32 = jnp.float32
    nrm = lambda k, shape, scale: jax.random.normal(k, shape, f32) * scale
    return {
        'x_prompt': nrm(ks[0], (BATCH, SEQ, D_MODEL), 1.0),
        'x_sample': nrm(ks[1], (DEC_BATCH, DEC_SEQ, D_MODEL), 1.0),
        'cache_k': nrm(ks[2], (DEC_BATCH, DEPTH, PAST_LEN, KV_HEADS, HEAD_DIM), 1.0),
        'cache_v': nrm(ks[3], (DEC_BATCH, DEPTH, PAST_LEN, KV_HEADS, HEAD_DIM), 1.0),
        'state_gla': nrm(ks[4], (DEC_BATCH, DEPTH, 2, GLA_HEADS, GLA_DK, GLA_DV), 0.5),
        'c': nrm(ks[5], (DEC_BATCH, D_MODEL), 1.0),
        'c_ctx': nrm(ks[6], (D_MODEL,), 1.0),
        'ada_w': nrm(ks[7], (DEPTH, D_MODEL, 6 * D_MODEL), D_MODEL ** -0.5),
        'ada_b': nrm(ks[8], (DEPTH, 6 * D_MODEL), 0.02),
        'w_in': nrm(ks[9], (DEPTH, D_MODEL, IN_WIDTH), D_MODEL ** -0.5),
        'q_norm': 1.0 + nrm(ks[10], (DEPTH, HEAD_DIM), 0.02),
        'k_norm': 1.0 + nrm(ks[11], (DEPTH, HEAD_DIM), 0.02),
        'gate_w2': nrm(ks[12], (DEPTH, 2, GATE_RANK, GLA_QK_WIDTH), GATE_RANK ** -0.5),
        'gate_b': jax.random.uniform(ks[13], (DEPTH, 2, GLA_QK_WIDTH), f32, 1.0, 4.0),
        'gla_norm': 1.0 + nrm(ks[14], (DEPTH, GLA_DV), 0.02),
        'w_attn_o': nrm(ks[15], (DEPTH, ATT_WIDTH, D_MODEL), ATT_WIDTH ** -0.5),
        'w_gla_o': nrm(ks[16], (DEPTH, GLA_V_WIDTH, D_MODEL), GLA_V_WIDTH ** -0.5),
        'w_out': nrm(ks[17], (DEPTH, D_MODEL, D_MODEL), BETA * D_MODEL ** -0.5),
        'ln1_g': 1.0 + nrm(ks[18], (DEPTH, D_MODEL), 0.02),
        'ln1_b': nrm(ks[19], (DEPTH, D_MODEL), 0.02),
        'ln2_g': 1.0 + nrm(ks[20], (DEPTH, D_MODEL), 0.02),
        'ln2_b': nrm(ks[21], (DEPTH, D_MODEL), 0.02),
        'peer_wq': nrm(ks[22], (DEPTH, D_MODEL, PEER_HEADS * PEER_QDIM), D_MODEL ** -0.5),
        'peer_sub_keys': nrm(ks[23], (DEPTH, 2, N_KEYS, PEER_HALF), PEER_HALF ** -0.5),
        'peer_u': nrm(ks[24], (DEPTH, N_EXPERTS, D_MODEL), D_MODEL ** -0.5),
        'peer_v': nrm(ks[25], (DEPTH, N_EXPERTS, D_MODEL), BETA * PEER_HEADS ** -0.5),
    }


def reference(x_prompt, x_sample, cache_k, cache_v, state_gla, c, c_ctx, ada_w, ada_b, w_in, q_norm, k_norm,
              gate_w2, gate_b, gla_norm, w_attn_o, w_gla_o, w_out, ln1_g, ln1_b, ln2_g, ln2_b,
              peer_wq, peer_sub_keys, peer_u, peer_v):
    params = (ada_w, ada_b, w_in, q_norm, k_norm, gate_w2, gate_b, gla_norm, w_attn_o, w_gla_o, w_out,
              ln1_g, ln1_b, ln2_g, ln2_b, peer_wq, peer_sub_keys, peer_u, peer_v)
    xp = x_prompt
    ks, vs, ss = [], [], []
    for l in range(DEPTH):
        lp = tuple(p[l] for p in params)
        xp, (k_l, v_l, sf_l, sb_l) = trunk_layer(xp, c_ctx, lp, None)
        ks.append(k_l)
        vs.append(v_l)
        ss.append(jnp.stack([sf_l, sb_l], axis=1))
    new_cache_k = jnp.stack(ks, axis=1)
    new_cache_v = jnp.stack(vs, axis=1)
    new_state_gla = jnp.stack(ss, axis=1)
    xs = x_sample
    for l in range(DEPTH):
        lp = tuple(p[l] for p in params)
        ctx = (cache_k[:, l], cache_v[:, l], state_gla[:, l, 0], state_gla[:, l, 1])
        xs, _ = trunk_layer(xs, c, lp, ctx)
    return (xp, xs, new_cache_k, new_cache_v, new_state_gla)
```

```python
import functools

import numpy as np
import jax
import jax.numpy as jnp
from jax import lax
from jax.experimental import pallas as pl
from jax.experimental.pallas import tpu as pltpu

F32 = jnp.float32
BF16 = jnp.bfloat16

D_MODEL = 1024
HEAD_DIM = 64
N_HEADS = D_MODEL // HEAD_DIM
KV_HEADS = N_HEADS // 4
Q_PER_KV = N_HEADS // KV_HEADS
KV_WIDTH = KV_HEADS * HEAD_DIM
GRID_W = 64
ROPE_FREQS = HEAD_DIM // 4
ROPE_THETA = 10000.0
GLA_HEADS = 4
GLA_DK = D_MODEL // 2 // GLA_HEADS
GLA_DV = D_MODEL // GLA_HEADS
GLA_QK_WIDTH = GLA_HEADS * GLA_DK
GLA_V_WIDTH = GLA_HEADS * GLA_DV
GATE_RANK = 16
GATE_NORM = 16.0
GLA_CHUNK = 64
N_KEYS = 128
N_EXPERTS = N_KEYS * N_KEYS
PEER_HEADS = 8
PEER_TOPK = 16
PEER_QDIM = 256
PEER_HALF = PEER_QDIM // 2
LN_EPS = 1e-5
RMS_EPS = 1e-6

V7X_VMEM_BYTES = 64 * 1024 * 1024
VMEM_LIMIT = V7X_VMEM_BYTES - 8 * 1024 * 1024
LANES = 128

TOKEN_TILE = 256
PEER_TOKEN_TILE = 512
PEER_EXPERT_TILE = 1024


def _dot(a, b):
    return jnp.dot(a, b, preferred_element_type=F32)


def _dot_nt(a, b):
    return lax.dot_general(a, b, (((1,), (1,)), ((), ())), preferred_element_type=F32)


def _dot_tn(a, b):
    return lax.dot_general(a, b, (((0,), (0,)), ((), ())), preferred_element_type=F32)


def _const_spec(shape):
    zeros = (0,) * len(shape)
    return pl.BlockSpec(shape, lambda *_: zeros)


def _params(*sem):
    return pltpu.CompilerParams(dimension_semantics=sem, vmem_limit_bytes=VMEM_LIMIT)


def _layer_norm(x, g, b):
    mu = jnp.mean(x, axis=-1, keepdims=True)
    xc = x - mu
    var = jnp.mean(xc * xc, axis=-1, keepdims=True)
    return xc * lax.rsqrt(var + LN_EPS) * g + b


def _ada_kernel(c_ref, w_ref, b_ref, o_ref):
    c = c_ref[...]
    s = c * jax.nn.sigmoid(c)
    o_ref[...] = _dot(s.astype(BF16), w_ref[...].astype(BF16)) + b_ref[...]


def _ada_mod(cond, ada_w, ada_b):
    depth = ada_w.shape[0]
    rows = cond.shape[0]
    return pl.pallas_call(
        _ada_kernel,
        out_shape=jax.ShapeDtypeStruct((depth, rows, 6 * D_MODEL), F32),
        grid=(depth, 6),
        in_specs=[
            pl.BlockSpec((rows, D_MODEL), lambda l, j: (0, 0)),
            pl.BlockSpec((None, D_MODEL, D_MODEL), lambda l, j: (l, 0, j)),
            pl.BlockSpec((None, 1, D_MODEL), lambda l, j: (l, 0, j)),
        ],
        out_specs=pl.BlockSpec((None, rows, D_MODEL), lambda l, j: (l, 0, j)),
        compiler_params=_params("parallel", "parallel"),
        name="ada_mod",
    )(cond, ada_w, ada_b.reshape(depth, 1, 6 * D_MODEL))


def _rope(t, cos, sin_signed):
    width = t.shape[-1]
    up = pltpu.roll(t, width - ROPE_FREQS, 1)
    dn = pltpu.roll(t, ROPE_FREQS, 1)
    lane = lax.broadcasted_iota(jnp.int32, t.shape, 1)
    partner = jnp.where((lane & ROPE_FREQS) == 0, up, dn)
    return t * cos + partner * sin_signed


def _inproj_kernel(rope, *refs):
    (x_ref, mod_ref, wq_ref, wkv_ref, wg_ref, wglr_ref, wgm_ref, mq_ref, qg_ref, kg_ref,
     w2_ref, gb_ref) = refs[:12]
    refs = refs[12:]
    if rope:
        cos_ref, sin_ref = refs[:2]
        refs = refs[2:]
    q_out, k_out, v_out, gq_out, gk_out, gv_out, go_out, la_out, gm_out = refs

    m = mod_ref[0]
    h = (x_ref[...] * (1.0 + m[1:2]) + m[0:1]).astype(BF16)

    q = _dot(h, wq_ref[...])
    qn = q * lax.rsqrt(_dot((q * q).astype(BF16), mq_ref[...]) + RMS_EPS) * qg_ref[...]
    kv = _dot(h, wkv_ref[...])
    k = kv[:, :KV_WIDTH]
    kn = k * lax.rsqrt(_dot((k * k).astype(BF16), mq_ref[:KV_WIDTH, :KV_WIDTH]) + RMS_EPS) * kg_ref[...]
    if rope:
        cos = cos_ref[...]
        sin = sin_ref[...]
        qn = _rope(qn, cos, sin)
        kn = _rope(kn, cos[:, :KV_WIDTH], sin[:, :KV_WIDTH])
    q_out[...] = qn * (HEAD_DIM ** -0.5)
    k_out[...] = kn
    v_out[...] = kv[:, KV_WIDTH:]

    g = _dot(h, wg_ref[...])
    gq_out[...] = g[:, :GLA_QK_WIDTH] * (GLA_DK ** -0.5)
    gk_out[...] = g[:, GLA_QK_WIDTH:2 * GLA_QK_WIDTH]
    gv_out[...] = g[:, 2 * GLA_QK_WIDTH:2 * GLA_QK_WIDTH + GLA_V_WIDTH]
    go_out[...] = g[:, 2 * GLA_QK_WIDTH + GLA_V_WIDTH:]

    glr = _dot(h, wglr_ref[...])
    z = _dot(glr.astype(BF16), w2_ref[...]) + gb_ref[...]
    la_out[...] = (jnp.minimum(z, 0.0) - jnp.log1p(jnp.exp(-jnp.abs(z)))) * (1.0 / GATE_NORM)
    gm_out[...] = jax.nn.sigmoid(_dot(h, wgm_ref[...]))


def _inproj(x, mod, lw, seq_len, rope_tabs):
    tokens = x.shape[0]
    tm = TOKEN_TILE
    rope = rope_tabs is not None
    row = lambda i: (i, 0)
    ins = [x, mod, lw["wq"], lw["wkv"], lw["wg"], lw["wglr"], lw["wgm"], lw["mq"], lw["qg"], lw["kg"],
           lw["w2"], lw["gb"]]
    in_specs = [pl.BlockSpec((tm, D_MODEL), row),
                pl.BlockSpec((1, 6, D_MODEL), lambda i: ((i * tm) // seq_len % mod.shape[0], 0, 0))]
    in_specs += [_const_spec(a.shape) for a in ins[2:]]
    if rope:
        per_seq = seq_len // tm
        ins += list(rope_tabs)
        in_specs += [pl.BlockSpec((tm, D_MODEL), lambda i: (i % per_seq, 0))] * 2
    widths = (D_MODEL, KV_WIDTH, KV_WIDTH, GLA_QK_WIDTH, GLA_QK_WIDTH, GLA_V_WIDTH, GLA_V_WIDTH,
              2 * GLA_QK_WIDTH, 2 * D_MODEL)
    return pl.pallas_call(
        functools.partial(_inproj_kernel, rope),
        out_shape=[jax.ShapeDtypeStruct((tokens, w), F32) for w in widths],
        grid=(tokens // tm,),
        in_specs=in_specs,
        out_specs=[pl.BlockSpec((tm, w), row) for w in widths],
        compiler_params=_params("parallel"),
        name="inproj_rope" if rope else "inproj",
    )(*ins)


def _attn_kernel(has_ctx, *refs):
    if has_ctx:
        q_ref, k_ref, v_ref, ck_ref, cv_ref, o_ref = refs
    else:
        q_ref, k_ref, v_ref, o_ref = refs
    tq = q_ref.shape[0]
    for g in range(KV_HEADS):
        gs = slice(g * HEAD_DIM, (g + 1) * HEAD_DIM)
        kg = k_ref[:, gs].astype(BF16)
        vg = v_ref[:, gs].astype(BF16)
        if has_ctx:
            kg = jnp.concatenate([kg, ck_ref[:, gs].astype(BF16)], axis=0)
            vg = jnp.concatenate([vg, cv_ref[:, gs].astype(BF16)], axis=0)
        heads = [q_ref[:, (Q_PER_KV * g + r) * HEAD_DIM:(Q_PER_KV * g + r + 1) * HEAD_DIM]
                 for r in range(Q_PER_KV)]
        qs = jnp.concatenate(heads, axis=0).astype(BF16)
        s = _dot_nt(qs, kg)
        p = jnp.exp(s - jnp.max(s, axis=-1, keepdims=True))
        o = _dot(p.astype(BF16), vg) / jnp.sum(p, axis=-1, keepdims=True)
        for r in range(Q_PER_KV):
            h0 = (Q_PER_KV * g + r) * HEAD_DIM
            o_ref[:, h0:h0 + HEAD_DIM] = o[r * tq:(r + 1) * tq]


def _attention(q, k, v, batch, seq_len, tq, ctx_kv, layer):
    tokens = q.shape[0]
    nq = seq_len // tq
    ins = [q, k, v]
    in_specs = [pl.BlockSpec((tq, D_MODEL), lambda b, i: (b * nq + i, 0)),
                pl.BlockSpec((seq_len, KV_WIDTH), lambda b, i: (b, 0)),
                pl.BlockSpec((seq_len, KV_WIDTH), lambda b, i: (b, 0))]
    if ctx_kv is not None:
        past = ctx_kv[0].shape[2]
        ins += list(ctx_kv)
        in_specs += [pl.BlockSpec((None, None, past, KV_WIDTH), lambda b, i: (b, layer, 0, 0))] * 2
    return pl.pallas_call(
        functools.partial(_attn_kernel, ctx_kv is not None),
        out_shape=jax.ShapeDtypeStruct((tokens, D_MODEL), F32),
        grid=(batch, nq),
        in_specs=in_specs,
        out_specs=pl.BlockSpec((tq, D_MODEL), lambda b, i: (b * nq + i, 0)),
        compiler_params=_params("parallel", "parallel"),
        name="attention_ctx" if ctx_kv is not None else "attention",
    )(*ins)


def _split3(x):
    hi = x.astype(BF16)
    r = x - hi.astype(F32)
    mid = r.astype(BF16)
    lo = (r - mid.astype(F32)).astype(BF16)
    return hi, mid, lo


def _gla_kernel(has_s0, *refs):
    gq_ref, gk_ref, gv_ref, la_ref, tri_ref = refs[:5]
    refs = refs[5:]
    if has_s0:
        s0_ref = refs[0]
        refs = refs[1:]
    o_ref, st_out, bf_ref, bb_ref, st_ref = refs
    seq_len = gq_ref.shape[0]
    n_chunks = seq_len // GLA_CHUNK
    tril = tri_ref[0]
    triu = tri_ref[1]

    def cumsum_chunk(c, carry):
        rows = pl.ds(pl.multiple_of(c * GLA_CHUNK, GLA_CHUNK), GLA_CHUNK)
        la = la_ref[rows, :]
        pf = _split3(la[:, :GLA_QK_WIDTH])
        pb = _split3(la[:, GLA_QK_WIDTH:])
        bf_ref[rows, :] = _dot(tril, pf[0]) + _dot(tril, pf[1]) + _dot(tril, pf[2])
        bb_ref[rows, :] = _dot(triu, pb[0]) + _dot(triu, pb[1]) + _dot(triu, pb[2])
        return carry

    lax.fori_loop(0, n_chunks, cumsum_chunk, 0)

    ri = lax.broadcasted_iota(jnp.int32, (GLA_CHUNK, GLA_CHUNK), 0)
    ci = lax.broadcasted_iota(jnp.int32, (GLA_CHUNK, GLA_CHUNK), 1)
    for d in range(2):
        b_ref = bf_ref if d == 0 else bb_ref
        keep = (ci <= ri) if d == 0 else (ci >= ri)
        for h in range(GLA_HEADS):
            ks = slice(h * GLA_DK, (h + 1) * GLA_DK)
            vs = slice(h * GLA_DV, (h + 1) * GLA_DV)
            if has_s0:
                st_ref[...] = s0_ref[d, h].T
            else:
                st_ref[...] = jnp.zeros_like(st_ref)

            def step(i, carry, d=d, b_ref=b_ref, keep=keep, ks=ks, vs=vs):
                c = i if d == 0 else n_chunks - 1 - i
                rows = pl.ds(pl.multiple_of(c * GLA_CHUNK, GLA_CHUNK), GLA_CHUNK)
                b = b_ref[rows, ks]
                bl = b[GLA_CHUNK - 1:GLA_CHUNK] if d == 0 else b[0:1]
                kk = gk_ref[rows, ks]
                v = gv_ref[rows, vs].astype(BF16)
                qe = (gq_ref[rows, ks] * jnp.exp(b)).astype(BF16)
                ke = (kk * jnp.exp(-b)).astype(BF16)
                kl = (kk * jnp.exp(bl - b)).astype(BF16)
                a = jnp.where(keep, _dot_nt(qe, ke), 0.0).astype(BF16)
                st = st_ref[...]
                o = _dot(a, v) + _dot_nt(qe, st.astype(BF16))
                if d == 0:
                    o_ref[rows, vs] = o
                else:
                    o_ref[rows, vs] += o
                st_ref[...] = st * jnp.exp(bl) + _dot_tn(v, kl)
                return carry

            lax.fori_loop(0, n_chunks, step, 0)
            st_out[d, h] = st_ref[...].T


def _gla(gq, gk, gv, la, batch, seq_len, tri, s0, layer):
    tokens = gq.shape[0]
    seq = lambda w: pl.BlockSpec((seq_len, w), lambda b: (b, 0))
    ins = [gq, gk, gv, la, tri]
    in_specs = [seq(GLA_QK_WIDTH), seq(GLA_QK_WIDTH), seq(GLA_V_WIDTH), seq(2 * GLA_QK_WIDTH),
                _const_spec(tri.shape)]
    if s0 is not None:
        ins.append(s0)
        in_specs.append(pl.BlockSpec((None, None, 2, GLA_HEADS, GLA_DK, GLA_DV),
                                     lambda b: (b, layer, 0, 0, 0, 0)))
    return pl.pallas_call(
        functools.partial(_gla_kernel, s0 is not None),
        out_shape=[jax.ShapeDtypeStruct((tokens, GLA_V_WIDTH), F32),
                   jax.ShapeDtypeStruct((batch, 2, GLA_HEADS, GLA_DK, GLA_DV), F32)],
        grid=(batch,),
        in_specs=in_specs,
        out_specs=[seq(GLA_V_WIDTH),
                   pl.BlockSpec((None, 2, GLA_HEADS, GLA_DK, GLA_DV), lambda b: (b, 0, 0, 0, 0))],
        scratch_shapes=[pltpu.VMEM((seq_len, GLA_QK_WIDTH), F32),
                        pltpu.VMEM((seq_len, GLA_QK_WIDTH), F32),
                        pltpu.VMEM((GLA_DV, GLA_DK), F32)],
        compiler_params=_params("parallel"),
        name="gla_s0" if s0 is not None else "gla",
    )(*ins)


def _postmix_kernel(alpha, x_ref, mod_ref, at_ref, og_ref, go_ref, gm_ref, wa_ref, wl_ref, wo_ref,
                    gn_ref, lg_ref, lb_ref, x1_ref, ht_ref):
    m = mod_ref[0]
    og = og_ref[...]
    parts = []
    for h in range(GLA_HEADS):
        oh = og[:, h * GLA_DV:(h + 1) * GLA_DV]
        parts.append(oh * lax.rsqrt(jnp.mean(oh * oh, axis=-1, keepdims=True) + RMS_EPS))
    go = go_ref[...]
    o = jnp.concatenate(parts, axis=-1) * gn_ref[...] * (go * jax.nn.sigmoid(go))
    gm = gm_ref[...]
    y = (gm[:, :D_MODEL] * _dot(at_ref[...].astype(BF16), wa_ref[...])
         + gm[:, D_MODEL:] * _dot(o.astype(BF16), wl_ref[...]))
    mix = _dot(y.astype(BF16), wo_ref[...])
    x1 = _layer_norm(alpha * x_ref[...] + m[2:3] * mix, lg_ref[...], lb_ref[...])
    x1_ref[...] = x1
    ht_ref[...] = (x1 * (1.0 + m[4:5]) + m[3:4]).T.astype(BF16)


def _postmix(x, mod, attn, og, go, gm, lw, seq_len, alpha):
    tokens = x.shape[0]
    tm = TOKEN_TILE
    row = lambda w: pl.BlockSpec((tm, w), lambda i: (i, 0))
    consts = [lw["wa"], lw["wl"], lw["wo"], lw["gn"], lw["ln1g"], lw["ln1b"]]
    return pl.pallas_call(
        functools.partial(_postmix_kernel, alpha),
        out_shape=[jax.ShapeDtypeStruct((tokens, D_MODEL), F32),
                   jax.ShapeDtypeStruct((D_MODEL, tokens), BF16)],
        grid=(tokens // tm,),
        in_specs=[row(D_MODEL),
                  pl.BlockSpec((1, 6, D_MODEL), lambda i: ((i * tm) // seq_len % mod.shape[0], 0, 0)),
                  row(D_MODEL), row(D_MODEL), row(D_MODEL), row(2 * D_MODEL)]
                 + [_const_spec(a.shape) for a in consts],
        out_specs=[row(D_MODEL), pl.BlockSpec((D_MODEL, tm), lambda i: (0, i))],
        compiler_params=_params("parallel"),
        name="postmix",
    )(x, mod, attn, og, go, gm, *consts)


def _peer_candidate_tables():
    groups = [[(0, r) for r in range(16)], [(r, 0) for r in range(16)]]
    for t in (1, 2, 3):
        groups.append([(t, r) for r in range(8)])
        if t < 3:
            groups.append([(r, t) for r in range(8)])
    seen = set()
    ci, neg = [], []
    for grp in groups:
        for (r1, r2) in grp:
            ok = (r1 + 1) * (r2 + 1) <= PEER_TOPK and (r1, r2) not in seen
            if ok:
                seen.add((r1, r2))
            ci.append(float(r1 * PEER_TOPK + r2) if ok else 1e9)
            neg.append(0.0 if ok else -np.inf)
    tab = np.stack([np.asarray(ci, np.float32), np.asarray(neg, np.float32)])
    return np.ascontiguousarray(np.broadcast_to(tab[:, :, None], tab.shape + (LANES,)))


def _extract_top(s):
    key = lax.broadcasted_iota(jnp.int32, s.shape, 0).astype(F32)
    slot = lax.broadcasted_iota(jnp.int32, (PEER_TOPK, s.shape[1]), 0)
    rank = jnp.full(s.shape, float(PEER_TOPK), F32)
    vals = jnp.zeros((PEER_TOPK, s.shape[1]), F32)
    for r in range(PEER_TOPK):
        m = jnp.max(s, axis=0, keepdims=True)
        first = jnp.min(jnp.where(s == m, key, float(N_KEYS)), axis=0, keepdims=True)
        hit = key == first
        rank = jnp.where(hit, float(r), rank)
        s = jnp.where(hit, -jnp.inf, s)
        vals = jnp.where(slot == r, m, vals)
    return vals, rank


def _candidate_counts(v1, v2, ci, neg):
    lo = slice(0, 8)
    cand = jnp.concatenate([
        v1[0:1] + v2, v1 + v2[0:1],
        v1[1:2] + v2[lo], v1[lo] + v2[1:2],
        v1[2:3] + v2[lo], v1[lo] + v2[2:3],
        v1[3:4] + v2[lo]], axis=0) + neg
    taken = jnp.zeros(cand.shape, F32)
    for _ in range(PEER_TOPK):
        m = jnp.max(cand, axis=0, keepdims=True)
        first = jnp.min(jnp.where(cand == m, ci, 2e9), axis=0, keepdims=True)
        hit = ci == first
        taken = jnp.where(hit, 1.0, taken)
        cand = jnp.where(hit, -jnp.inf, cand)
    row_sum = lambda a, b: jnp.sum(taken[a:b], axis=0, keepdims=True)
    slot = lax.broadcasted_iota(jnp.int32, v1.shape, 0)
    counts = taken[16:32] + jnp.concatenate(
        [taken[40:48] + taken[56:64], jnp.zeros((8, v1.shape[1]), F32)], axis=0)
    counts += jnp.where(slot == 0, row_sum(0, 16), 0.0)
    counts += jnp.where(slot == 1, row_sum(32, 40), 0.0)
    counts += jnp.where(slot == 2, row_sum(48, 56), 0.0)
    counts += jnp.where(slot == 3, row_sum(64, 72), 0.0)
    return counts


def _route_kernel(ht_ref, wq_ref, keys_ref, tab_ref, cnt_out, p1_out, rk_out, p2_out, q_ref):
    q_ref[...] = _dot(wq_ref[...], ht_ref[...])
    ci = tab_ref[0]
    neg = tab_ref[1]

    def head(h, carry):
        r0 = pl.multiple_of(h * PEER_QDIM, PEER_QDIM)
        s1 = _dot(keys_ref[0], q_ref[pl.ds(r0, PEER_HALF), :].astype(BF16))
        s2 = _dot(keys_ref[1], q_ref[pl.ds(r0 + PEER_HALF, PEER_HALF), :].astype(BF16))
        v1, rank1 = _extract_top(s1)
        v2, rank2 = _extract_top(s2)
        counts = _candidate_counts(v1, v2, ci, neg)
        cnt = jnp.zeros(s1.shape, F32)
        for r in range(PEER_TOPK):
            cnt = jnp.where(rank1 == float(r), counts[r:r + 1], cnt)
        e1 = jnp.exp(v1 - v1[0:1])
        e2 = jnp.exp(v2 - v2[0:1])
        inner = jnp.zeros(v1.shape, F32)
        for r in range(PEER_TOPK):
            inner += jnp.where(counts > float(r), e2[r:r + 1], 0.0)
        z = jnp.sum(e1 * inner, axis=0, keepdims=True)
        cnt_out[h] = cnt
        p1_out[h] = jnp.exp(s1 - v1[0:1])
        rk_out[h] = rank2
        p2_out[h] = jnp.exp(s2 - v2[0:1]) / z
        return carry

    lax.fori_loop(0, PEER_HEADS, head, 0)


def _peer_route(ht, lw, tab):
    tokens = ht.shape[1]
    dense = jax.ShapeDtypeStruct((PEER_HEADS, N_KEYS, tokens), F32)
    out_spec = pl.BlockSpec((PEER_HEADS, N_KEYS, LANES), lambda i: (0, 0, i))
    return pl.pallas_call(
        _route_kernel,
        out_shape=[dense] * 4,
        grid=(tokens // LANES,),
        in_specs=[pl.BlockSpec((D_MODEL, LANES), lambda i: (0, i)),
                  _const_spec(lw["pwq"].shape), _const_spec(lw["pkeys"].shape), _const_spec(tab.shape)],
        out_specs=[out_spec] * 4,
        scratch_shapes=[pltpu.VMEM((PEER_HEADS * PEER_QDIM, LANES), F32)],
        compiler_params=_params("parallel"),
        name="peer_route",
    )(ht, lw["pwq"], lw["pkeys"], tab)


def _gelu_tanh(x):
    return 0.5 * x * (1.0 + jnp.tanh(0.7978845608028654 * (x + 0.044715 * (x * x * x))))


def _peer_kernel(alpha, ht_ref, u_ref, vt_ref, cnt_ref, p1_ref, rk_ref, p2_ref, x1_ref, mod_ref,
                 lg_ref, lb_ref, o_ref, acc_ref, a_ref, g_ref):
    j = pl.program_id(1)
    tb = ht_ref.shape[1]
    rows_per_tile = u_ref.shape[0] // N_KEYS

    @pl.when(j == 0)
    def _():
        acc_ref[...] = jnp.zeros_like(acc_ref)

    a_ref[...] = _dot(u_ref[...], ht_ref[...])

    def first_key(a, carry):
        tile_rows = pl.ds(pl.multiple_of(j * rows_per_tile, rows_per_tile), rows_per_tile)
        to_top = (rows_per_tile - a) % rows_per_tile
        row_a = lambda blk: pltpu.roll(blk, to_top, 0)[0:1]
        for lg in range(tb // LANES):
            ls = slice(lg * LANES, (lg + 1) * LANES)
            cb = [row_a(cnt_ref[h, tile_rows, ls]) for h in range(PEER_HEADS)]
            pb = [row_a(p1_ref[h, tile_rows, ls]) for h in range(PEER_HEADS)]
            for g in range(N_KEYS // 16):
                ks = slice(g * 16, (g + 1) * 16)
                w = jnp.zeros((16, LANES), F32)
                for h in range(PEER_HEADS):
                    w += jnp.where(rk_ref[h, ks, ls] < cb[h], p2_ref[h, ks, ls], 0.0) * pb[h]
                rows = pl.ds(pl.multiple_of(a * N_KEYS + g * 16, 16), 16)
                g_ref[rows, ls] = (w * _gelu_tanh(a_ref[rows, ls])).astype(BF16)
        return carry

    lax.fori_loop(0, rows_per_tile, first_key, 0)
    acc_ref[...] += _dot(vt_ref[...], g_ref[...])

    @pl.when(j == pl.num_programs(1) - 1)
    def _():
        m = mod_ref[0]
        o_ref[...] = _layer_norm(alpha * x1_ref[...] + m[5:6] * acc_ref[...].T, lg_ref[...], lb_ref[...])


def _peer_dense(ht, route, x1, mod, lw, seq_len, alpha):
    tokens = x1.shape[0]
    tb = min(PEER_TOKEN_TILE, tokens)
    et = PEER_EXPERT_TILE
    assert et // N_KEYS == 8, "one sublane tile of first-key rows per expert tile"
    dense = pl.BlockSpec((PEER_HEADS, N_KEYS, tb), lambda i, j: (0, 0, i))
    return pl.pallas_call(
        functools.partial(_peer_kernel, alpha),
        out_shape=jax.ShapeDtypeStruct((tokens, D_MODEL), F32),
        grid=(tokens // tb, N_EXPERTS // et),
        in_specs=[pl.BlockSpec((D_MODEL, tb), lambda i, j: (0, i)),
                  pl.BlockSpec((et, D_MODEL), lambda i, j: (j, 0)),
                  pl.BlockSpec((D_MODEL, et), lambda i, j: (0, j)),
                  dense, dense, dense, dense,
                  pl.BlockSpec((tb, D_MODEL), lambda i, j: (i, 0)),
                  pl.BlockSpec((1, 6, D_MODEL), lambda i, j: ((i * tb) // seq_len % mod.shape[0], 0, 0)),
                  _const_spec(lw["ln2g"].shape), _const_spec(lw["ln2b"].shape)],
        out_specs=pl.BlockSpec((tb, D_MODEL), lambda i, j: (i, 0)),
        scratch_shapes=[pltpu.VMEM((D_MODEL, tb), F32),
                        pltpu.VMEM((et, tb), F32),
                        pltpu.VMEM((et, tb), BF16)],
        compiler_params=_params("parallel", "arbitrary"),
        name="peer_dense",
    )(ht, lw["pu"], lw["pvt"], *route, x1, mod, lw["ln2g"], lw["ln2b"])


def _rope_tables(seq_len):
    rows = seq_len // GRID_W
    r = jnp.repeat(jnp.arange(rows, dtype=F32), GRID_W)
    col = jnp.tile(jnp.arange(GRID_W, dtype=F32), rows)
    inv = ROPE_THETA ** (-jnp.arange(ROPE_FREQS, dtype=F32) / ROPE_FREQS)
    ang = jnp.stack([r[:, None] * inv, col[:, None] * inv], axis=1)
    cos, sin = jnp.cos(ang), jnp.sin(ang)
    cos_h = jnp.concatenate([cos, cos], axis=-1).reshape(seq_len, HEAD_DIM)
    sin_h = jnp.concatenate([-sin, sin], axis=-1).reshape(seq_len, HEAD_DIM)
    return jnp.tile(cos_h, (1, N_HEADS)), jnp.tile(sin_h, (1, N_HEADS))


def _layer_weights(l, w_in, q_norm, k_norm, gate_w2, gate_b, gla_norm, w_attn_o, w_gla_o, w_out,
                   ln1_g, ln1_b, ln2_g, ln2_b, peer_wq, peer_sub_keys, peer_u, peer_v):
    w = w_in[l].astype(BF16)
    o_q, o_k, o_g, o_lr, o_gm = 0, D_MODEL, D_MODEL + 2 * KV_WIDTH, 0, 0
    o_lr = o_g + 2 * GLA_QK_WIDTH + 2 * GLA_V_WIDTH
    o_gm = o_lr + 2 * GATE_RANK
    head_id = np.arange(D_MODEL) // HEAD_DIM
    mq = jnp.asarray((head_id[:, None] == head_id[None, :]).astype(np.float32) / HEAD_DIM, BF16)
    w2 = jnp.zeros((LANES, 2 * GLA_QK_WIDTH), F32)
    w2 = w2.at[:GATE_RANK, :GLA_QK_WIDTH].set(gate_w2[l, 0])
    w2 = w2.at[GATE_RANK:2 * GATE_RANK, GLA_QK_WIDTH:].set(gate_w2[l, 1])
    row = lambda a: a.reshape(1, -1)
    return dict(
        wq=w[:, o_q:o_k], wkv=w[:, o_k:o_g], wg=w[:, o_g:o_lr],
        wglr=jnp.pad(w[:, o_lr:o_gm], ((0, 0), (0, LANES - 2 * GATE_RANK))),
        wgm=w[:, o_gm:], mq=mq,
        qg=row(jnp.tile(q_norm[l], N_HEADS)), kg=row(jnp.tile(k_norm[l], KV_HEADS)),
        w2=w2.astype(BF16), gb=row(gate_b[l]),
        wa=w_attn_o[l].astype(BF16), wl=w_gla_o[l].astype(BF16), wo=w_out[l].astype(BF16),
        gn=row(jnp.tile(gla_norm[l], GLA_HEADS)),
        ln1g=row(ln1_g[l]), ln1b=row(ln1_b[l]), ln2g=row(ln2_g[l]), ln2b=row(ln2_b[l]),
        pwq=peer_wq[l].T.astype(BF16), pkeys=peer_sub_keys[l].astype(BF16),
        pu=peer_u[l].astype(BF16), pvt=peer_v[l].T.astype(BF16),
    )


def _trunk_layer(x, mod, lw, batch, seq_len, alpha, consts, ctx, layer):
    rope_tabs = None if ctx is None else consts["rope"]
    q, k, v, gq, gk, gv, go, la, gm = _inproj(x, mod, lw, seq_len, rope_tabs)
    if ctx is None:
        attn = _attention(q, k, v, batch, seq_len, min(seq_len, 256), None, layer)
        og, states = _gla(gq, gk, gv, la, batch, seq_len, consts["tri"], None, layer)
    else:
        attn = _attention(q, k, v, batch, seq_len, 128, ctx[:2], layer)
        og, states = _gla(gq, gk, gv, la, batch, seq_len, consts["tri"], ctx[2], layer)
    x1, ht = _postmix(x, mod, attn, og, go, gm, lw, seq_len, alpha)
    route = _peer_route(ht, lw, consts["cand"])
    x2 = _peer_dense(ht, route, x1, mod, lw, seq_len, alpha)
    return x2, (k, v, states)


def kernel(x_prompt, x_sample, cache_k, cache_v, state_gla, c, c_ctx, ada_w, ada_b, w_in, q_norm, k_norm,
           gate_w2, gate_b, gla_norm, w_attn_o, w_gla_o, w_out, ln1_g, ln1_b, ln2_g, ln2_b,
           peer_wq, peer_sub_keys, peer_u, peer_v):
    depth = ada_w.shape[0]
    alpha = (2.0 * depth) ** 0.25
    batch, seq, _ = x_prompt.shape
    dec_batch, dec_seq, _ = x_sample.shape
    past = cache_k.shape[2]

    n_cond = 1 + dec_batch
    cond = jnp.concatenate([c_ctx[None], c, jnp.zeros((-n_cond % 8, D_MODEL), F32)], axis=0)
    mod = _ada_mod(cond, ada_w, ada_b).reshape(depth, cond.shape[0], 6, D_MODEL)

    idx = np.arange(GLA_CHUNK)
    tri = jnp.asarray(np.stack([idx[None, :] <= idx[:, None], idx[None, :] >= idx[:, None]]), BF16)
    consts = dict(tri=tri, cand=jnp.asarray(_peer_candidate_tables()), rope=_rope_tables(dec_seq))
    weights = [_layer_weights(l, w_in, q_norm, k_norm, gate_w2, gate_b, gla_norm, w_attn_o, w_gla_o, w_out,
                              ln1_g, ln1_b, ln2_g, ln2_b, peer_wq, peer_sub_keys, peer_u, peer_v)
               for l in range(depth)]

    xp = x_prompt.reshape(batch * seq, D_MODEL)
    ks, vs, ss = [], [], []
    for l in range(depth):
        xp, (k_l, v_l, s_l) = _trunk_layer(xp, mod[l, 0:1], weights[l], batch, seq, alpha, consts, None, l)
        ks.append(k_l.reshape(batch, seq, KV_HEADS, HEAD_DIM))
        vs.append(v_l.reshape(batch, seq, KV_HEADS, HEAD_DIM))
        ss.append(s_l)
    new_cache_k = jnp.stack(ks, axis=1)
    new_cache_v = jnp.stack(vs, axis=1)
    new_state = jnp.stack(ss, axis=1)

    ctx = (cache_k.reshape(dec_batch, depth, past, KV_WIDTH), cache_v.reshape(dec_batch, depth, past, KV_WIDTH),
           state_gla)
    xs = x_sample.reshape(dec_batch * dec_seq, D_MODEL)
    for l in range(depth):
        xs, _ = _trunk_layer(xs, mod[l, 1:1 + dec_batch], weights[l], dec_batch, dec_seq, alpha, consts, ctx, l)

    return (xp.reshape(batch, seq, D_MODEL), xs.reshape(dec_batch, dec_seq, D_MODEL),
            new_cache_k, new_cache_v, new_state)
```

```python
import functools

import numpy as np
import jax
import jax.numpy as jnp
from jax import lax
from jax.experimental import pallas as pl
from jax.experimental.pallas import tpu as pltpu

F32 = jnp.float32
BF16 = jnp.bfloat16

D_MODEL = 1024
HEAD_DIM = 64
N_HEADS = D_MODEL // HEAD_DIM
KV_HEADS = N_HEADS // 4
Q_PER_KV = N_HEADS // KV_HEADS
KV_WIDTH = KV_HEADS * HEAD_DIM
GRID_W = 64
ROPE_FREQS = HEAD_DIM // 4
ROPE_THETA = 10000.0
GLA_HEADS = 4
GLA_DK = D_MODEL // 2 // GLA_HEADS
GLA_DV = D_MODEL // GLA_HEADS
GLA_QK_WIDTH = GLA_HEADS * GLA_DK
GLA_V_WIDTH = GLA_HEADS * GLA_DV
GATE_RANK = 16
GATE_NORM = 16.0
GLA_CHUNK = 64
N_KEYS = 128
N_EXPERTS = N_KEYS * N_KEYS
PEER_HEADS = 8
PEER_TOPK = 16
PEER_QDIM = 256
PEER_HALF = PEER_QDIM // 2
LN_EPS = 1e-5
RMS_EPS = 1e-6

V7X_VMEM_BYTES = 64 * 1024 * 1024
VMEM_LIMIT = V7X_VMEM_BYTES - 8 * 1024 * 1024
LANES = 128

TOKEN_TILE = 256
PEER_TOKEN_TILE = 512
PEER_EXPERT_TILE = 1024
PEER_CHUNK_KEYS = 2


def _dot(a, b):
    return jnp.dot(a, b, preferred_element_type=F32)


def _dot_nt(a, b):
    return lax.dot_general(a, b, (((1,), (1,)), ((), ())), preferred_element_type=F32)


def _dot_tn(a, b):
    return lax.dot_general(a, b, (((0,), (0,)), ((), ())), preferred_element_type=F32)


def _const_spec(shape):
    zeros = (0,) * len(shape)
    return pl.BlockSpec(shape, lambda *_: zeros)


def _params(*sem):
    return pltpu.CompilerParams(dimension_semantics=sem, vmem_limit_bytes=VMEM_LIMIT)


def _layer_norm(x, g, b):
    mu = jnp.mean(x, axis=-1, keepdims=True)
    xc = x - mu
    var = jnp.mean(xc * xc, axis=-1, keepdims=True)
    return xc * lax.rsqrt(var + LN_EPS) * g + b


def _ada_kernel(c_ref, w_ref, b_ref, o_ref):
    c = c_ref[...]
    s = c * jax.nn.sigmoid(c)
    o_ref[...] = _dot(s.astype(BF16), w_ref[...].astype(BF16)) + b_ref[...]


def _ada_mod(cond, ada_w, ada_b):
    depth = ada_w.shape[0]
    rows = cond.shape[0]
    return pl.pallas_call(
        _ada_kernel,
        out_shape=jax.ShapeDtypeStruct((depth, rows, 6 * D_MODEL), F32),
        grid=(depth, 6),
        in_specs=[
            pl.BlockSpec((rows, D_MODEL), lambda l, j: (0, 0)),
            pl.BlockSpec((None, D_MODEL, D_MODEL), lambda l, j: (l, 0, j)),
            pl.BlockSpec((None, 1, D_MODEL), lambda l, j: (l, 0, j)),
        ],
        out_specs=pl.BlockSpec((None, rows, D_MODEL), lambda l, j: (l, 0, j)),
        compiler_params=_params("parallel", "parallel"),
        name="ada_mod",
    )(cond, ada_w, ada_b.reshape(depth, 1, 6 * D_MODEL))


def _rope(t, cos, sin_signed):
    width = t.shape[-1]
    up = pltpu.roll(t, width - ROPE_FREQS, 1)
    dn = pltpu.roll(t, ROPE_FREQS, 1)
    lane = lax.broadcasted_iota(jnp.int32, t.shape, 1)
    partner = jnp.where((lane & ROPE_FREQS) == 0, up, dn)
    return t * cos + partner * sin_signed


def _inproj_kernel(rope, *refs):
    (x_ref, mod_ref, wq_ref, wkv_ref, wg_ref, wglr_ref, wgm_ref, mq_ref, qg_ref, kg_ref,
     w2_ref, gb_ref) = refs[:12]
    refs = refs[12:]
    if rope:
        cos_ref, sin_ref = refs[:2]
        refs = refs[2:]
    q_out, k_out, v_out, gq_out, gk_out, gv_out, go_out, la_out, gm_out = refs

    m = mod_ref[0]
    h = (x_ref[...] * (1.0 + m[1:2]) + m[0:1]).astype(BF16)

    q = _dot(h, wq_ref[...])
    qn = q * lax.rsqrt(_dot((q * q).astype(BF16), mq_ref[...]) + RMS_EPS) * qg_ref[...]
    kv = _dot(h, wkv_ref[...])
    k = kv[:, :KV_WIDTH]
    kn = k * lax.rsqrt(_dot((k * k).astype(BF16), mq_ref[:KV_WIDTH, :KV_WIDTH]) + RMS_EPS) * kg_ref[...]
    if rope:
        cos = cos_ref[...]
        sin = sin_ref[...]
        qn = _rope(qn, cos, sin)
        kn = _rope(kn, cos[:, :KV_WIDTH], sin[:, :KV_WIDTH])
    q_out[...] = qn * (HEAD_DIM ** -0.5)
    k_out[...] = kn
    v_out[...] = kv[:, KV_WIDTH:]

    g = _dot(h, wg_ref[...])
    gq_out[...] = g[:, :GLA_QK_WIDTH] * (GLA_DK ** -0.5)
    gk_out[...] = g[:, GLA_QK_WIDTH:2 * GLA_QK_WIDTH]
    gv_out[...] = g[:, 2 * GLA_QK_WIDTH:2 * GLA_QK_WIDTH + GLA_V_WIDTH]
    go_out[...] = g[:, 2 * GLA_QK_WIDTH + GLA_V_WIDTH:]

    glr = _dot(h, wglr_ref[...])
    z = _dot(glr.astype(BF16), w2_ref[...]) + gb_ref[...]
    la_out[...] = (jnp.minimum(z, 0.0) - jnp.log1p(jnp.exp(-jnp.abs(z)))) * (1.0 / GATE_NORM)
    gm_out[...] = jax.nn.sigmoid(_dot(h, wgm_ref[...]))


def _inproj(x, mod, lw, seq_len, rope_tabs):
    tokens = x.shape[0]
    tm = TOKEN_TILE
    rope = rope_tabs is not None
    row = lambda i: (i, 0)
    ins = [x, mod, lw["wq"], lw["wkv"], lw["wg"], lw["wglr"], lw["wgm"], lw["mq"], lw["qg"], lw["kg"],
           lw["w2"], lw["gb"]]
    in_specs = [pl.BlockSpec((tm, D_MODEL), row),
                pl.BlockSpec((1, 6, D_MODEL), lambda i: ((i * tm) // seq_len % mod.shape[0], 0, 0))]
    in_specs += [_const_spec(a.shape) for a in ins[2:]]
    if rope:
        per_seq = seq_len // tm
        ins += list(rope_tabs)
        in_specs += [pl.BlockSpec((tm, D_MODEL), lambda i: (i % per_seq, 0))] * 2
    widths = (D_MODEL, KV_WIDTH, KV_WIDTH, GLA_QK_WIDTH, GLA_QK_WIDTH, GLA_V_WIDTH, GLA_V_WIDTH,
              2 * GLA_QK_WIDTH, 2 * D_MODEL)
    return pl.pallas_call(
        functools.partial(_inproj_kernel, rope),
        out_shape=[jax.ShapeDtypeStruct((tokens, w), F32) for w in widths],
        grid=(tokens // tm,),
        in_specs=in_specs,
        out_specs=[pl.BlockSpec((tm, w), row) for w in widths],
        compiler_params=_params("parallel"),
        name="inproj_rope" if rope else "inproj",
    )(*ins)


def _attn_kernel(has_ctx, *refs):
    if has_ctx:
        q_ref, k_ref, v_ref, ck_ref, cv_ref, o_ref = refs
    else:
        q_ref, k_ref, v_ref, o_ref = refs
    tq = q_ref.shape[0]
    for g in range(KV_HEADS):
        gs = slice(g * HEAD_DIM, (g + 1) * HEAD_DIM)
        kg = k_ref[:, gs].astype(BF16)
        vg = v_ref[:, gs].astype(BF16)
        if has_ctx:
            kg = jnp.concatenate([kg, ck_ref[:, gs].astype(BF16)], axis=0)
            vg = jnp.concatenate([vg, cv_ref[:, gs].astype(BF16)], axis=0)
        heads = [q_ref[:, (Q_PER_KV * g + r) * HEAD_DIM:(Q_PER_KV * g + r + 1) * HEAD_DIM]
                 for r in range(Q_PER_KV)]
        qs = jnp.concatenate(heads, axis=0).astype(BF16)
        s = _dot_nt(qs, kg)
        p = jnp.exp(s - jnp.max(s, axis=-1, keepdims=True))
        o = _dot(p.astype(BF16), vg) / jnp.sum(p, axis=-1, keepdims=True)
        for r in range(Q_PER_KV):
            h0 = (Q_PER_KV * g + r) * HEAD_DIM
            o_ref[:, h0:h0 + HEAD_DIM] = o[r * tq:(r + 1) * tq]


def _attention(q, k, v, batch, seq_len, tq, ctx_kv, layer):
    tokens = q.shape[0]
    nq = seq_len // tq
    ins = [q, k, v]
    in_specs = [pl.BlockSpec((tq, D_MODEL), lambda b, i: (b * nq + i, 0)),
                pl.BlockSpec((seq_len, KV_WIDTH), lambda b, i: (b, 0)),
                pl.BlockSpec((seq_len, KV_WIDTH), lambda b, i: (b, 0))]
    if ctx_kv is not None:
        past = ctx_kv[0].shape[2]
        ins += list(ctx_kv)
        in_specs += [pl.BlockSpec((None, None, past, KV_WIDTH), lambda b, i: (b, layer, 0, 0))] * 2
    return pl.pallas_call(
        functools.partial(_attn_kernel, ctx_kv is not None),
        out_shape=jax.ShapeDtypeStruct((tokens, D_MODEL), F32),
        grid=(batch, nq),
        in_specs=in_specs,
        out_specs=pl.BlockSpec((tq, D_MODEL), lambda b, i: (b * nq + i, 0)),
        compiler_params=_params("parallel", "parallel"),
        name="attention_ctx" if ctx_kv is not None else "attention",
    )(*ins)


def _split3(x):
    hi = x.astype(BF16)
    r = x - hi.astype(F32)
    mid = r.astype(BF16)
    lo = (r - mid.astype(F32)).astype(BF16)
    return hi, mid, lo


def _gla_kernel(has_s0, *refs):
    gq_ref, gk_ref, gv_ref, la_ref, tri_ref = refs[:5]
    refs = refs[5:]
    if has_s0:
        s0_ref = refs[0]
        refs = refs[1:]
    o_ref, st_out, bf_ref, bb_ref, st_ref = refs
    seq_len = gq_ref.shape[0]
    n_chunks = seq_len // GLA_CHUNK
    tril = tri_ref[0]
    triu = tri_ref[1]

    def cumsum_chunk(c, carry):
        rows = pl.ds(pl.multiple_of(c * GLA_CHUNK, GLA_CHUNK), GLA_CHUNK)
        la = la_ref[rows, :]
        pf = _split3(la[:, :GLA_QK_WIDTH])
        pb = _split3(la[:, GLA_QK_WIDTH:])
        bf_ref[rows, :] = _dot(tril, pf[0]) + _dot(tril, pf[1]) + _dot(tril, pf[2])
        bb_ref[rows, :] = _dot(triu, pb[0]) + _dot(triu, pb[1]) + _dot(triu, pb[2])
        return carry

    lax.fori_loop(0, n_chunks, cumsum_chunk, 0)

    ri = lax.broadcasted_iota(jnp.int32, (GLA_CHUNK, GLA_CHUNK), 0)
    ci = lax.broadcasted_iota(jnp.int32, (GLA_CHUNK, GLA_CHUNK), 1)
    for d in range(2):
        b_ref = bf_ref if d == 0 else bb_ref
        keep = (ci <= ri) if d == 0 else (ci >= ri)
        for h in range(GLA_HEADS):
            ks = slice(h * GLA_DK, (h + 1) * GLA_DK)
            vs = slice(h * GLA_DV, (h + 1) * GLA_DV)
            if has_s0:
                st_ref[...] = s0_ref[d, h].T
            else:
                st_ref[...] = jnp.zeros_like(st_ref)

            def step(i, carry, d=d, b_ref=b_ref, keep=keep, ks=ks, vs=vs):
                c = i if d == 0 else n_chunks - 1 - i
                rows = pl.ds(pl.multiple_of(c * GLA_CHUNK, GLA_CHUNK), GLA_CHUNK)
                b = b_ref[rows, ks]
                bl = b[GLA_CHUNK - 1:GLA_CHUNK] if d == 0 else b[0:1]
                kk = gk_ref[rows, ks]
                v = gv_ref[rows, vs].astype(BF16)
                qe = (gq_ref[rows, ks] * jnp.exp(b)).astype(BF16)
                ke = (kk * jnp.exp(-b)).astype(BF16)
                kl = (kk * jnp.exp(bl - b)).astype(BF16)
                a = jnp.where(keep, _dot_nt(qe, ke), 0.0).astype(BF16)
                st = st_ref[...]
                o = _dot(a, v) + _dot_nt(qe, st.astype(BF16))
                if d == 0:
                    o_ref[rows, vs] = o
                else:
                    o_ref[rows, vs] += o
                st_ref[...] = st * jnp.exp(bl) + _dot_tn(v, kl)
                return carry

            lax.fori_loop(0, n_chunks, step, 0)
            st_out[d, h] = st_ref[...].T


def _gla(gq, gk, gv, la, batch, seq_len, tri, s0, layer):
    tokens = gq.shape[0]
    seq = lambda w: pl.BlockSpec((seq_len, w), lambda b: (b, 0))
    ins = [gq, gk, gv, la, tri]
    in_specs = [seq(GLA_QK_WIDTH), seq(GLA_QK_WIDTH), seq(GLA_V_WIDTH), seq(2 * GLA_QK_WIDTH),
                _const_spec(tri.shape)]
    if s0 is not None:
        ins.append(s0)
        in_specs.append(pl.BlockSpec((None, None, 2, GLA_HEADS, GLA_DK, GLA_DV),
                                     lambda b: (b, layer, 0, 0, 0, 0)))
    return pl.pallas_call(
        functools.partial(_gla_kernel, s0 is not None),
        out_shape=[jax.ShapeDtypeStruct((tokens, GLA_V_WIDTH), F32),
                   jax.ShapeDtypeStruct((batch, 2, GLA_HEADS, GLA_DK, GLA_DV), F32)],
        grid=(batch,),
        in_specs=in_specs,
        out_specs=[seq(GLA_V_WIDTH),
                   pl.BlockSpec((None, 2, GLA_HEADS, GLA_DK, GLA_DV), lambda b: (b, 0, 0, 0, 0))],
        scratch_shapes=[pltpu.VMEM((seq_len, GLA_QK_WIDTH), F32),
                        pltpu.VMEM((seq_len, GLA_QK_WIDTH), F32),
                        pltpu.VMEM((GLA_DV, GLA_DK), F32)],
        compiler_params=_params("parallel"),
        name="gla_s0" if s0 is not None else "gla",
    )(*ins)


def _postmix_kernel(alpha, x_ref, mod_ref, at_ref, og_ref, go_ref, gm_ref, wa_ref, wl_ref, wo_ref,
                    gn_ref, lg_ref, lb_ref, x1_ref, ht_ref):
    m = mod_ref[0]
    og = og_ref[...]
    parts = []
    for h in range(GLA_HEADS):
        oh = og[:, h * GLA_DV:(h + 1) * GLA_DV]
        parts.append(oh * lax.rsqrt(jnp.mean(oh * oh, axis=-1, keepdims=True) + RMS_EPS))
    go = go_ref[...]
    o = jnp.concatenate(parts, axis=-1) * gn_ref[...] * (go * jax.nn.sigmoid(go))
    gm = gm_ref[...]
    y = (gm[:, :D_MODEL] * _dot(at_ref[...].astype(BF16), wa_ref[...])
         + gm[:, D_MODEL:] * _dot(o.astype(BF16), wl_ref[...]))
    mix = _dot(y.astype(BF16), wo_ref[...])
    x1 = _layer_norm(alpha * x_ref[...] + m[2:3] * mix, lg_ref[...], lb_ref[...])
    x1_ref[...] = x1
    ht_ref[...] = (x1 * (1.0 + m[4:5]) + m[3:4]).T.astype(BF16)


def _postmix(x, mod, attn, og, go, gm, lw, seq_len, alpha):
    tokens = x.shape[0]
    tm = TOKEN_TILE
    row = lambda w: pl.BlockSpec((tm, w), lambda i: (i, 0))
    consts = [lw["wa"], lw["wl"], lw["wo"], lw["gn"], lw["ln1g"], lw["ln1b"]]
    return pl.pallas_call(
        functools.partial(_postmix_kernel, alpha),
        out_shape=[jax.ShapeDtypeStruct((tokens, D_MODEL), F32),
                   jax.ShapeDtypeStruct((D_MODEL, tokens), BF16)],
        grid=(tokens // tm,),
        in_specs=[row(D_MODEL),
                  pl.BlockSpec((1, 6, D_MODEL), lambda i: ((i * tm) // seq_len % mod.shape[0], 0, 0)),
                  row(D_MODEL), row(D_MODEL), row(D_MODEL), row(2 * D_MODEL)]
                 + [_const_spec(a.shape) for a in consts],
        out_specs=[row(D_MODEL), pl.BlockSpec((D_MODEL, tm), lambda i: (0, i))],
        compiler_params=_params("parallel"),
        name="postmix",
    )(x, mod, attn, og, go, gm, *consts)


def _peer_candidate_tables():
    groups = [[(0, r) for r in range(16)], [(r, 0) for r in range(16)]]
    for t in (1, 2, 3):
        groups.append([(t, r) for r in range(8)])
        if t < 3:
            groups.append([(r, t) for r in range(8)])
    seen = set()
    ci, neg = [], []
    for grp in groups:
        for (r1, r2) in grp:
            ok = (r1 + 1) * (r2 + 1) <= PEER_TOPK and (r1, r2) not in seen
            if ok:
                seen.add((r1, r2))
            ci.append(float(r1 * PEER_TOPK + r2) if ok else 1e9)
            neg.append(0.0 if ok else -np.inf)
    tab = np.stack([np.asarray(ci, np.float32), np.asarray(neg, np.float32)])
    return np.ascontiguousarray(np.broadcast_to(tab[:, :, None], tab.shape + (LANES,)))


def _extract_top(s):
    key = lax.broadcasted_iota(jnp.int32, s.shape, 0).astype(F32)
    slot = lax.broadcasted_iota(jnp.int32, (PEER_TOPK, s.shape[1]), 0)
    rank = jnp.full(s.shape, float(PEER_TOPK), F32)
    vals = jnp.zeros((PEER_TOPK, s.shape[1]), F32)
    for r in range(PEER_TOPK):
        m = jnp.max(s, axis=0, keepdims=True)
        first = jnp.min(jnp.where(s == m, key, float(N_KEYS)), axis=0, keepdims=True)
        hit = key == first
        rank = jnp.where(hit, float(r), rank)
        s = jnp.where(hit, -jnp.inf, s)
        vals = jnp.where(slot == r, m, vals)
    return vals, rank


def _candidate_counts(v1, v2, ci, neg):
    lo = slice(0, 8)
    cand = jnp.concatenate([
        v1[0:1] + v2, v1 + v2[0:1],
        v1[1:2] + v2[lo], v1[lo] + v2[1:2],
        v1[2:3] + v2[lo], v1[lo] + v2[2:3],
        v1[3:4] + v2[lo]], axis=0) + neg
    taken = jnp.zeros(cand.shape, F32)
    for _ in range(PEER_TOPK):
        m = jnp.max(cand, axis=0, keepdims=True)
        first = jnp.min(jnp.where(cand == m, ci, 2e9), axis=0, keepdims=True)
        hit = ci == first
        taken = jnp.where(hit, 1.0, taken)
        cand = jnp.where(hit, -jnp.inf, cand)
    row_sum = lambda a, b: jnp.sum(taken[a:b], axis=0, keepdims=True)
    slot = lax.broadcasted_iota(jnp.int32, v1.shape, 0)
    counts = taken[16:32] + jnp.concatenate(
        [taken[40:48] + taken[56:64], jnp.zeros((8, v1.shape[1]), F32)], axis=0)
    counts += jnp.where(slot == 0, row_sum(0, 16), 0.0)
    counts += jnp.where(slot == 1, row_sum(32, 40), 0.0)
    counts += jnp.where(slot == 2, row_sum(48, 56), 0.0)
    counts += jnp.where(slot == 3, row_sum(64, 72), 0.0)
    return counts


def _route_kernel(ht_ref, wq_ref, keys_ref, tab_ref, cnt_out, p1_out, rk_out, p2_out, q_ref):
    q_ref[...] = _dot(wq_ref[...], ht_ref[...])
    ci = tab_ref[0]
    neg = tab_ref[1]

    def head(h, carry):
        r0 = pl.multiple_of(h * PEER_QDIM, PEER_QDIM)
        s1 = _dot(keys_ref[0], q_ref[pl.ds(r0, PEER_HALF), :].astype(BF16))
        s2 = _dot(keys_ref[1], q_ref[pl.ds(r0 + PEER_HALF, PEER_HALF), :].astype(BF16))
        v1, rank1 = _extract_top(s1)
        v2, rank2 = _extract_top(s2)
        counts = _candidate_counts(v1, v2, ci, neg)
        cnt = jnp.zeros(s1.shape, F32)
        for r in range(PEER_TOPK):
            cnt = jnp.where(rank1 == float(r), counts[r:r + 1], cnt)
        e1 = jnp.exp(v1 - v1[0:1])
        e2 = jnp.exp(v2 - v2[0:1])
        inner = jnp.zeros(v1.shape, F32)
        for r in range(PEER_TOPK):
            inner += jnp.where(counts > float(r), e2[r:r + 1], 0.0)
        z = jnp.sum(e1 * inner, axis=0, keepdims=True)
        rows = pl.ds(pl.multiple_of(h * N_KEYS, N_KEYS), N_KEYS)
        cnt_out[rows, :] = cnt
        p1_out[rows, :] = jnp.exp(s1 - v1[0:1])
        half = pl.ds(pl.multiple_of(h * (N_KEYS // 2), N_KEYS // 2), N_KEYS // 2)
        rk_out[half, :] = pltpu.bitcast(rank2.astype(BF16), jnp.uint32)
        p2_out[half, :] = pltpu.bitcast((jnp.exp(s2 - v2[0:1]) / z).astype(BF16), jnp.uint32)
        return carry

    lax.fori_loop(0, PEER_HEADS, head, 0)


def _peer_route(ht, lw, tab):
    tokens = ht.shape[1]
    n_rows = PEER_HEADS * N_KEYS
    dense = lambda dt, rows: jax.ShapeDtypeStruct((rows, tokens), dt)
    out_spec = lambda rows: pl.BlockSpec((rows, LANES), lambda i: (0, i))
    return pl.pallas_call(
        _route_kernel,
        out_shape=[dense(F32, n_rows), dense(F32, n_rows),
                   dense(jnp.uint32, n_rows // 2), dense(jnp.uint32, n_rows // 2)],
        grid=(tokens // LANES,),
        in_specs=[pl.BlockSpec((D_MODEL, LANES), lambda i: (0, i)),
                  _const_spec(lw["pwq"].shape), _const_spec(lw["pkeys"].shape), _const_spec(tab.shape)],
        out_specs=[out_spec(n_rows), out_spec(n_rows), out_spec(n_rows // 2), out_spec(n_rows // 2)],
        scratch_shapes=[pltpu.VMEM((PEER_HEADS * PEER_QDIM, LANES), F32)],
        compiler_params=_params("parallel"),
        name="peer_route",
    )(ht, lw["pwq"], lw["pkeys"], tab)


def _gelu_tanh(x):
    k1 = -2.0 * 0.7978845608028654 * 1.4426950408889634
    e = jnp.exp2(x * (k1 + (k1 * 0.044715) * (x * x)))
    return x / (1.0 + e)


def _peer_kernel(alpha, ht_ref, u_ref, vt_ref, cnt_ref, p1_ref, rk_ref, p2_ref, x1_ref, mod_ref,
                 lg_ref, lb_ref, o_ref, acc_ref, a_ref, g_ref):
    j = pl.program_id(1)
    tb = ht_ref.shape[1]
    first_keys = u_ref.shape[0] // N_KEYS

    @pl.when(j == 0)
    def _():
        acc_ref[...] = jnp.zeros_like(acc_ref)

    head_rows = lambda h: pl.ds(pl.multiple_of(h * N_KEYS + j * first_keys, first_keys), first_keys)
    ht = ht_ref[...]
    n_chunks = first_keys // PEER_CHUNK_KEYS
    chunk_rows = lambda c: slice(c * PEER_CHUNK_KEYS * N_KEYS, (c + 1) * PEER_CHUNK_KEYS * N_KEYS)

    def activations(c):
        a_ref[chunk_rows(c), :] = _dot(u_ref[chunk_rows(c), :], ht)

    activations(0)
    for c in range(n_chunks):
        rows = chunk_rows(c)
        if c + 1 < n_chunks:
            activations(c + 1)
        for a in range(c * PEER_CHUNK_KEYS, (c + 1) * PEER_CHUNK_KEYS):
            for lg in range(tb // LANES):
                ls = slice(lg * LANES, (lg + 1) * LANES)
                row_a = lambda ref, h: jnp.broadcast_to(ref[head_rows(h), ls][a:a + 1], (16, LANES)).astype(BF16)
                cb = [row_a(cnt_ref, h) for h in range(PEER_HEADS)]
                pb = [row_a(p1_ref, h) for h in range(PEER_HEADS)]
                for g in range(N_KEYS // 16):
                    w = jnp.zeros((16, LANES), BF16)
                    for h in range(PEER_HEADS):
                        ks = slice(h * (N_KEYS // 2) + g * 8, h * (N_KEYS // 2) + (g + 1) * 8)
                        rk = pltpu.bitcast(rk_ref[ks, ls], BF16)
                        p2 = pltpu.bitcast(p2_ref[ks, ls], BF16)
                        w += jnp.where(rk < cb[h], p2, 0.0) * pb[h]
                    er = slice(a * N_KEYS + g * 16, a * N_KEYS + (g + 1) * 16)
                    g_ref[er, ls] = w * _gelu_tanh(a_ref[er, ls]).astype(BF16)
        acc_ref[...] += _dot(vt_ref[:, rows], g_ref[rows, :])

    @pl.when(j == pl.num_programs(1) - 1)
    def _():
        m = mod_ref[0]
        o_ref[...] = _layer_norm(alpha * x1_ref[...] + m[5:6] * acc_ref[...].T, lg_ref[...], lb_ref[...])


def _peer_dense(ht, route, x1, mod, lw, seq_len, alpha):
    tokens = x1.shape[0]
    tb = min(PEER_TOKEN_TILE, tokens)
    et = PEER_EXPERT_TILE
    assert et // N_KEYS == 8, "one sublane tile of first-key rows per expert tile"
    dense = pl.BlockSpec((PEER_HEADS * N_KEYS, tb), lambda i, j: (0, i))
    packed = pl.BlockSpec((PEER_HEADS * N_KEYS // 2, tb), lambda i, j: (0, i))
    return pl.pallas_call(
        functools.partial(_peer_kernel, alpha),
        out_shape=jax.ShapeDtypeStruct((tokens, D_MODEL), F32),
        grid=(tokens // tb, N_EXPERTS // et),
        in_specs=[pl.BlockSpec((D_MODEL, tb), lambda i, j: (0, i)),
                  pl.BlockSpec((et, D_MODEL), lambda i, j: (j, 0)),
                  pl.BlockSpec((D_MODEL, et), lambda i, j: (0, j)),
                  dense, dense, packed, packed,
                  pl.BlockSpec((tb, D_MODEL), lambda i, j: (i, 0)),
                  pl.BlockSpec((1, 6, D_MODEL), lambda i, j: ((i * tb) // seq_len % mod.shape[0], 0, 0)),
                  _const_spec(lw["ln2g"].shape), _const_spec(lw["ln2b"].shape)],
        out_specs=pl.BlockSpec((tb, D_MODEL), lambda i, j: (i, 0)),
        scratch_shapes=[pltpu.VMEM((D_MODEL, tb), F32),
                        pltpu.VMEM((et, tb), F32),
                        pltpu.VMEM((et, tb), BF16)],
        compiler_params=_params("parallel", "arbitrary"),
        name="peer_dense",
    )(ht, lw["pu"], lw["pvt"], *route, x1, mod, lw["ln2g"], lw["ln2b"])


def _rope_tables(seq_len):
    rows = seq_len // GRID_W
    r = jnp.repeat(jnp.arange(rows, dtype=F32), GRID_W)
    col = jnp.tile(jnp.arange(GRID_W, dtype=F32), rows)
    inv = ROPE_THETA ** (-jnp.arange(ROPE_FREQS, dtype=F32) / ROPE_FREQS)
    ang = jnp.stack([r[:, None] * inv, col[:, None] * inv], axis=1)
    cos, sin = jnp.cos(ang), jnp.sin(ang)
    cos_h = jnp.concatenate([cos, cos], axis=-1).reshape(seq_len, HEAD_DIM)
    sin_h = jnp.concatenate([-sin, sin], axis=-1).reshape(seq_len, HEAD_DIM)
    return jnp.tile(cos_h, (1, N_HEADS)), jnp.tile(sin_h, (1, N_HEADS))


def _layer_weights(l, w_in, q_norm, k_norm, gate_w2, gate_b, gla_norm, w_attn_o, w_gla_o, w_out,
                   ln1_g, ln1_b, ln2_g, ln2_b, peer_wq, peer_sub_keys, peer_u, peer_v):
    w = w_in[l].astype(BF16)
    o_q, o_k, o_g, o_lr, o_gm = 0, D_MODEL, D_MODEL + 2 * KV_WIDTH, 0, 0
    o_lr = o_g + 2 * GLA_QK_WIDTH + 2 * GLA_V_WIDTH
    o_gm = o_lr + 2 * GATE_RANK
    head_id = np.arange(D_MODEL) // HEAD_DIM
    mq = jnp.asarray((head_id[:, None] == head_id[None, :]).astype(np.float32) / HEAD_DIM, BF16)
    w2 = jnp.zeros((LANES, 2 * GLA_QK_WIDTH), F32)
    w2 = w2.at[:GATE_RANK, :GLA_QK_WIDTH].set(gate_w2[l, 0])
    w2 = w2.at[GATE_RANK:2 * GATE_RANK, GLA_QK_WIDTH:].set(gate_w2[l, 1])
    row = lambda a: a.reshape(1, -1)
    return dict(
        wq=w[:, o_q:o_k], wkv=w[:, o_k:o_g], wg=w[:, o_g:o_lr],
        wglr=jnp.pad(w[:, o_lr:o_gm], ((0, 0), (0, LANES - 2 * GATE_RANK))),
        wgm=w[:, o_gm:], mq=mq,
        qg=row(jnp.tile(q_norm[l], N_HEADS)), kg=row(jnp.tile(k_norm[l], KV_HEADS)),
        w2=w2.astype(BF16), gb=row(gate_b[l]),
        wa=w_attn_o[l].astype(BF16), wl=w_gla_o[l].astype(BF16), wo=w_out[l].astype(BF16),
        gn=row(jnp.tile(gla_norm[l], GLA_HEADS)),
        ln1g=row(ln1_g[l]), ln1b=row(ln1_b[l]), ln2g=row(ln2_g[l]), ln2b=row(ln2_b[l]),
        pwq=peer_wq[l].T.astype(BF16), pkeys=peer_sub_keys[l].astype(BF16),
        pu=peer_u[l].astype(BF16), pvt=peer_v[l].T.astype(BF16),
    )


def _trunk_layer(x, mod, lw, batch, seq_len, alpha, consts, ctx, layer):
    rope_tabs = None if ctx is None else consts["rope"]
    q, k, v, gq, gk, gv, go, la, gm = _inproj(x, mod, lw, seq_len, rope_tabs)
    if ctx is None:
        attn = _attention(q, k, v, batch, seq_len, min(seq_len, 256), None, layer)
        og, states = _gla(gq, gk, gv, la, batch, seq_len, consts["tri"], None, layer)
    else:
        attn = _attention(q, k, v, batch, seq_len, 128, ctx[:2], layer)
        og, states = _gla(gq, gk, gv, la, batch, seq_len, consts["tri"], ctx[2], layer)
    x1, ht = _postmix(x, mod, attn, og, go, gm, lw, seq_len, alpha)
    route = _peer_route(ht, lw, consts["cand"])
    x2 = _peer_dense(ht, route, x1, mod, lw, seq_len, alpha)
    return x2, (k, v, states)


def kernel(x_prompt, x_sample, cache_k, cache_v, state_gla, c, c_ctx, ada_w, ada_b, w_in, q_norm, k_norm,
           gate_w2, gate_b, gla_norm, w_attn_o, w_gla_o, w_out, ln1_g, ln1_b, ln2_g, ln2_b,
           peer_wq, peer_sub_keys, peer_u, peer_v):
    depth = ada_w.shape[0]
    alpha = (2.0 * depth) ** 0.25
    batch, seq, _ = x_prompt.shape
    dec_batch, dec_seq, _ = x_sample.shape
    past = cache_k.shape[2]

    n_cond = 1 + dec_batch
    cond = jnp.concatenate([c_ctx[None], c, jnp.zeros((-n_cond % 8, D_MODEL), F32)], axis=0)
    mod = _ada_mod(cond, ada_w, ada_b).reshape(depth, cond.shape[0], 6, D_MODEL)

    idx = np.arange(GLA_CHUNK)
    tri = jnp.asarray(np.stack([idx[None, :] <= idx[:, None], idx[None, :] >= idx[:, None]]), BF16)
    consts = dict(tri=tri, cand=jnp.asarray(_peer_candidate_tables()), rope=_rope_tables(dec_seq))
    weights = [_layer_weights(l, w_in, q_norm, k_norm, gate_w2, gate_b, gla_norm, w_attn_o, w_gla_o, w_out,
                              ln1_g, ln1_b, ln2_g, ln2_b, peer_wq, peer_sub_keys, peer_u, peer_v)
               for l in range(depth)]

    xp = x_prompt.reshape(batch * seq, D_MODEL)
    ks, vs, ss = [], [], []
    for l in range(depth):
        xp, (k_l, v_l, s_l) = _trunk_layer(xp, mod[l, 0:1], weights[l], batch, seq, alpha, consts, None, l)
        ks.append(k_l.reshape(batch, seq, KV_HEADS, HEAD_DIM))
        vs.append(v_l.reshape(batch, seq, KV_HEADS, HEAD_DIM))
        ss.append(s_l)
    new_cache_k = jnp.stack(ks, axis=1)
    new_cache_v = jnp.stack(vs, axis=1)
    new_state = jnp.stack(ss, axis=1)

    ctx = (cache_k.reshape(dec_batch, depth, past, KV_WIDTH), cache_v.reshape(dec_batch, depth, past, KV_WIDTH),
           state_gla)
    xs = x_sample.reshape(dec_batch * dec_seq, D_MODEL)
    for l in range(depth):
        xs, _ = _trunk_layer(xs, mod[l, 1:1 + dec_batch], weights[l], dec_batch, dec_seq, alpha, consts, ctx, l)

    return (xp.reshape(batch, seq, D_MODEL), xs.reshape(dec_batch, dec_seq, D_MODEL),
            new_cache_k, new_cache_v, new_state)
```

```python
import functools

import numpy as np
import jax
import jax.numpy as jnp
from jax import lax
from jax.experimental import pallas as pl
from jax.experimental.pallas import tpu as pltpu

F32 = jnp.float32
BF16 = jnp.bfloat16

D_MODEL = 1024
HEAD_DIM = 64
N_HEADS = D_MODEL // HEAD_DIM
KV_HEADS = N_HEADS // 4
Q_PER_KV = N_HEADS // KV_HEADS
KV_WIDTH = KV_HEADS * HEAD_DIM
GRID_W = 64
ROPE_FREQS = HEAD_DIM // 4
ROPE_THETA = 10000.0
GLA_HEADS = 4
GLA_DK = D_MODEL // 2 // GLA_HEADS
GLA_DV = D_MODEL // GLA_HEADS
GLA_QK_WIDTH = GLA_HEADS * GLA_DK
GLA_V_WIDTH = GLA_HEADS * GLA_DV
GATE_RANK = 16
GATE_NORM = 16.0
GLA_CHUNK = 64
N_KEYS = 128
N_EXPERTS = N_KEYS * N_KEYS
PEER_HEADS = 8
PEER_TOPK = 16
PEER_QDIM = 256
PEER_HALF = PEER_QDIM // 2
LN_EPS = 1e-5
RMS_EPS = 1e-6

V7X_VMEM_BYTES = 64 * 1024 * 1024
VMEM_LIMIT = V7X_VMEM_BYTES - 8 * 1024 * 1024
LANES = 128

TOKEN_TILE = 256
PEER_TOKEN_TILE = 512
PEER_EXPERT_TILE = 1024
PEER_CHUNK_KEYS = 2


def _dot(a, b):
    return jnp.dot(a, b, preferred_element_type=F32)


def _dot_nt(a, b):
    return lax.dot_general(a, b, (((1,), (1,)), ((), ())), preferred_element_type=F32)


def _dot_tn(a, b):
    return lax.dot_general(a, b, (((0,), (0,)), ((), ())), preferred_element_type=F32)


def _const_spec(shape):
    zeros = (0,) * len(shape)
    return pl.BlockSpec(shape, lambda *_: zeros)


def _params(*sem):
    return pltpu.CompilerParams(dimension_semantics=sem, vmem_limit_bytes=VMEM_LIMIT)


def _layer_norm(x, g, b):
    mu = jnp.mean(x, axis=-1, keepdims=True)
    xc = x - mu
    var = jnp.mean(xc * xc, axis=-1, keepdims=True)
    return xc * lax.rsqrt(var + LN_EPS) * g + b


def _ada_kernel(c_ref, w_ref, b_ref, o_ref):
    c = c_ref[...]
    s = c * jax.nn.sigmoid(c)
    o_ref[...] = _dot(s.astype(BF16), w_ref[...].astype(BF16)) + b_ref[...]


def _ada_mod(cond, ada_w, ada_b):
    depth = ada_w.shape[0]
    rows = cond.shape[0]
    return pl.pallas_call(
        _ada_kernel,
        out_shape=jax.ShapeDtypeStruct((depth, rows, 6 * D_MODEL), F32),
        grid=(depth, 6),
        in_specs=[
            pl.BlockSpec((rows, D_MODEL), lambda l, j: (0, 0)),
            pl.BlockSpec((None, D_MODEL, D_MODEL), lambda l, j: (l, 0, j)),
            pl.BlockSpec((None, 1, D_MODEL), lambda l, j: (l, 0, j)),
        ],
        out_specs=pl.BlockSpec((None, rows, D_MODEL), lambda l, j: (l, 0, j)),
        compiler_params=_params("parallel", "parallel"),
        name="ada_mod",
    )(cond, ada_w, ada_b.reshape(depth, 1, 6 * D_MODEL))


def _rope(t, cos, sin_signed):
    width = t.shape[-1]
    up = pltpu.roll(t, width - ROPE_FREQS, 1)
    dn = pltpu.roll(t, ROPE_FREQS, 1)
    lane = lax.broadcasted_iota(jnp.int32, t.shape, 1)
    partner = jnp.where((lane & ROPE_FREQS) == 0, up, dn)
    return t * cos + partner * sin_signed


def _inproj_kernel(rope, *refs):
    (x_ref, mod_ref, wq_ref, wkv_ref, wg_ref, wglr_ref, wgm_ref, mq_ref, qg_ref, kg_ref,
     w2_ref, gb_ref) = refs[:12]
    refs = refs[12:]
    if rope:
        cos_ref, sin_ref = refs[:2]
        refs = refs[2:]
    q_out, k_out, v_out, gq_out, gk_out, gv_out, go_out, la_out, gm_out = refs

    m = mod_ref[0]
    h = (x_ref[...] * (1.0 + m[1:2]) + m[0:1]).astype(BF16)

    q = _dot(h, wq_ref[...])
    qn = q * lax.rsqrt(_dot((q * q).astype(BF16), mq_ref[...]) + RMS_EPS) * qg_ref[...]
    kv = _dot(h, wkv_ref[...])
    k = kv[:, :KV_WIDTH]
    kn = k * lax.rsqrt(_dot((k * k).astype(BF16), mq_ref[:KV_WIDTH, :KV_WIDTH]) + RMS_EPS) * kg_ref[...]
    if rope:
        cos = cos_ref[...]
        sin = sin_ref[...]
        qn = _rope(qn, cos, sin)
        kn = _rope(kn, cos[:, :KV_WIDTH], sin[:, :KV_WIDTH])
    q_out[...] = qn * (HEAD_DIM ** -0.5)
    k_out[...] = kn
    v_out[...] = kv[:, KV_WIDTH:]

    g = _dot(h, wg_ref[...])
    gq_out[...] = g[:, :GLA_QK_WIDTH] * (GLA_DK ** -0.5)
    gk_out[...] = g[:, GLA_QK_WIDTH:2 * GLA_QK_WIDTH]
    gv_out[...] = g[:, 2 * GLA_QK_WIDTH:2 * GLA_QK_WIDTH + GLA_V_WIDTH]
    go_out[...] = g[:, 2 * GLA_QK_WIDTH + GLA_V_WIDTH:]

    glr = _dot(h, wglr_ref[...])
    z = _dot(glr.astype(BF16), w2_ref[...]) + gb_ref[...]
    la_out[...] = (jnp.minimum(z, 0.0) - jnp.log1p(jnp.exp(-jnp.abs(z)))) * (1.0 / GATE_NORM)
    gm_out[...] = jax.nn.sigmoid(_dot(h, wgm_ref[...]))


def _inproj(x, mod, lw, seq_len, rope_tabs):
    tokens = x.shape[0]
    tm = TOKEN_TILE
    rope = rope_tabs is not None
    row = lambda i: (i, 0)
    ins = [x, mod, lw["wq"], lw["wkv"], lw["wg"], lw["wglr"], lw["wgm"], lw["mq"], lw["qg"], lw["kg"],
           lw["w2"], lw["gb"]]
    in_specs = [pl.BlockSpec((tm, D_MODEL), row),
                pl.BlockSpec((1, 6, D_MODEL), lambda i: ((i * tm) // seq_len % mod.shape[0], 0, 0))]
    in_specs += [_const_spec(a.shape) for a in ins[2:]]
    if rope:
        per_seq = seq_len // tm
        ins += list(rope_tabs)
        in_specs += [pl.BlockSpec((tm, D_MODEL), lambda i: (i % per_seq, 0))] * 2
    widths = (D_MODEL, KV_WIDTH, KV_WIDTH, GLA_QK_WIDTH, GLA_QK_WIDTH, GLA_V_WIDTH, GLA_V_WIDTH,
              2 * GLA_QK_WIDTH, 2 * D_MODEL)
    return pl.pallas_call(
        functools.partial(_inproj_kernel, rope),
        out_shape=[jax.ShapeDtypeStruct((tokens, w), F32) for w in widths],
        grid=(tokens // tm,),
        in_specs=in_specs,
        out_specs=[pl.BlockSpec((tm, w), row) for w in widths],
        compiler_params=_params("parallel"),
        name="inproj_rope" if rope else "inproj",
    )(*ins)


def _attn_kernel(has_ctx, *refs):
    if has_ctx:
        q_ref, k_ref, v_ref, ck_ref, cv_ref, o_ref = refs
    else:
        q_ref, k_ref, v_ref, o_ref = refs
    tq = q_ref.shape[0]
    for g in range(KV_HEADS):
        gs = slice(g * HEAD_DIM, (g + 1) * HEAD_DIM)
        kg = k_ref[:, gs].astype(BF16)
        vg = v_ref[:, gs].astype(BF16)
        if has_ctx:
            kg = jnp.concatenate([kg, ck_ref[:, gs].astype(BF16)], axis=0)
            vg = jnp.concatenate([vg, cv_ref[:, gs].astype(BF16)], axis=0)
        heads = [q_ref[:, (Q_PER_KV * g + r) * HEAD_DIM:(Q_PER_KV * g + r + 1) * HEAD_DIM]
                 for r in range(Q_PER_KV)]
        qs = jnp.concatenate(heads, axis=0).astype(BF16)
        s = _dot_nt(qs, kg)
        p = jnp.exp(s - jnp.max(s, axis=-1, keepdims=True))
        o = _dot(p.astype(BF16), vg) / jnp.sum(p, axis=-1, keepdims=True)
        for r in range(Q_PER_KV):
            h0 = (Q_PER_KV * g + r) * HEAD_DIM
            o_ref[:, h0:h0 + HEAD_DIM] = o[r * tq:(r + 1) * tq]


def _attention(q, k, v, batch, seq_len, tq, ctx_kv, layer):
    tokens = q.shape[0]
    nq = seq_len // tq
    ins = [q, k, v]
    in_specs = [pl.BlockSpec((tq, D_MODEL), lambda b, i: (b * nq + i, 0)),
                pl.BlockSpec((seq_len, KV_WIDTH), lambda b, i: (b, 0)),
                pl.BlockSpec((seq_len, KV_WIDTH), lambda b, i: (b, 0))]
    if ctx_kv is not None:
        past = ctx_kv[0].shape[2]
        ins += list(ctx_kv)
        in_specs += [pl.BlockSpec((None, None, past, KV_WIDTH), lambda b, i: (b, layer, 0, 0))] * 2
    return pl.pallas_call(
        functools.partial(_attn_kernel, ctx_kv is not None),
        out_shape=jax.ShapeDtypeStruct((tokens, D_MODEL), F32),
        grid=(batch, nq),
        in_specs=in_specs,
        out_specs=pl.BlockSpec((tq, D_MODEL), lambda b, i: (b * nq + i, 0)),
        compiler_params=_params("parallel", "parallel"),
        name="attention_ctx" if ctx_kv is not None else "attention",
    )(*ins)


def _split3(x):
    hi = x.astype(BF16)
    r = x - hi.astype(F32)
    mid = r.astype(BF16)
    lo = (r - mid.astype(F32)).astype(BF16)
    return hi, mid, lo


def _gla_kernel(has_s0, *refs):
    gq_ref, gk_ref, gv_ref, la_ref, tri_ref = refs[:5]
    refs = refs[5:]
    if has_s0:
        s0_ref = refs[0]
        refs = refs[1:]
    o_ref, st_out, bf_ref, bb_ref, st_ref = refs
    seq_len = gq_ref.shape[0]
    n_chunks = seq_len // GLA_CHUNK
    tril = tri_ref[0]
    triu = tri_ref[1]

    def cumsum_chunk(c, carry):
        rows = pl.ds(pl.multiple_of(c * GLA_CHUNK, GLA_CHUNK), GLA_CHUNK)
        la = la_ref[rows, :]
        pf = _split3(la[:, :GLA_QK_WIDTH])
        pb = _split3(la[:, GLA_QK_WIDTH:])
        bf_ref[rows, :] = _dot(tril, pf[0]) + _dot(tril, pf[1]) + _dot(tril, pf[2])
        bb_ref[rows, :] = _dot(triu, pb[0]) + _dot(triu, pb[1]) + _dot(triu, pb[2])
        return carry

    lax.fori_loop(0, n_chunks, cumsum_chunk, 0)

    ri = lax.broadcasted_iota(jnp.int32, (GLA_CHUNK, GLA_CHUNK), 0)
    ci = lax.broadcasted_iota(jnp.int32, (GLA_CHUNK, GLA_CHUNK), 1)
    for d in range(2):
        for h in range(GLA_HEADS):
            if has_s0:
                st_ref[d * GLA_HEADS + h] = s0_ref[d, h].T
            else:
                st_ref[d * GLA_HEADS + h] = jnp.zeros((GLA_DV, GLA_DK), F32)
    o_ref[...] = jnp.zeros_like(o_ref)

    def step(i, carry):
        for d in range(2):
            b_ref = bf_ref if d == 0 else bb_ref
            keep = (ci <= ri) if d == 0 else (ci >= ri)
            c = i if d == 0 else n_chunks - 1 - i
            rows = pl.ds(pl.multiple_of(c * GLA_CHUNK, GLA_CHUNK), GLA_CHUNK)
            for h in range(GLA_HEADS):
                ks = slice(h * GLA_DK, (h + 1) * GLA_DK)
                vs = slice(h * GLA_DV, (h + 1) * GLA_DV)
                b = b_ref[rows, ks]
                bl = b[GLA_CHUNK - 1:GLA_CHUNK] if d == 0 else b[0:1]
                kk = gk_ref[rows, ks]
                v = gv_ref[rows, vs].astype(BF16)
                qe = (gq_ref[rows, ks] * jnp.exp(b)).astype(BF16)
                ke = (kk * jnp.exp(-b)).astype(BF16)
                kl = (kk * jnp.exp(bl - b)).astype(BF16)
                a = jnp.where(keep, _dot_nt(qe, ke), 0.0).astype(BF16)
                st = st_ref[d * GLA_HEADS + h]
                o_ref[rows, vs] += _dot(a, v) + _dot_nt(qe, st.astype(BF16))
                st_ref[d * GLA_HEADS + h] = st * jnp.exp(bl) + _dot_tn(v, kl)
        return carry

    lax.fori_loop(0, n_chunks, step, 0)
    for d in range(2):
        for h in range(GLA_HEADS):
            st_out[d, h] = st_ref[d * GLA_HEADS + h].T


def _gla(gq, gk, gv, la, batch, seq_len, tri, s0, layer):
    tokens = gq.shape[0]
    seq = lambda w: pl.BlockSpec((seq_len, w), lambda b: (b, 0))
    ins = [gq, gk, gv, la, tri]
    in_specs = [seq(GLA_QK_WIDTH), seq(GLA_QK_WIDTH), seq(GLA_V_WIDTH), seq(2 * GLA_QK_WIDTH),
                _const_spec(tri.shape)]
    if s0 is not None:
        ins.append(s0)
        in_specs.append(pl.BlockSpec((None, None, 2, GLA_HEADS, GLA_DK, GLA_DV),
                                     lambda b: (b, layer, 0, 0, 0, 0)))
    return pl.pallas_call(
        functools.partial(_gla_kernel, s0 is not None),
        out_shape=[jax.ShapeDtypeStruct((tokens, GLA_V_WIDTH), F32),
                   jax.ShapeDtypeStruct((batch, 2, GLA_HEADS, GLA_DK, GLA_DV), F32)],
        grid=(batch,),
        in_specs=in_specs,
        out_specs=[seq(GLA_V_WIDTH),
                   pl.BlockSpec((None, 2, GLA_HEADS, GLA_DK, GLA_DV), lambda b: (b, 0, 0, 0, 0))],
        scratch_shapes=[pltpu.VMEM((seq_len, GLA_QK_WIDTH), F32),
                        pltpu.VMEM((seq_len, GLA_QK_WIDTH), F32),
                        pltpu.VMEM((2 * GLA_HEADS, GLA_DV, GLA_DK), F32)],
        compiler_params=_params("parallel"),
        name="gla_s0" if s0 is not None else "gla",
    )(*ins)


def _postmix_kernel(alpha, x_ref, mod_ref, at_ref, og_ref, go_ref, gm_ref, wa_ref, wl_ref, wo_ref,
                    gn_ref, lg_ref, lb_ref, x1_ref, ht_ref):
    m = mod_ref[0]
    og = og_ref[...]
    parts = []
    for h in range(GLA_HEADS):
        oh = og[:, h * GLA_DV:(h + 1) * GLA_DV]
        parts.append(oh * lax.rsqrt(jnp.mean(oh * oh, axis=-1, keepdims=True) + RMS_EPS))
    go = go_ref[...]
    o = jnp.concatenate(parts, axis=-1) * gn_ref[...] * (go * jax.nn.sigmoid(go))
    gm = gm_ref[...]
    y = (gm[:, :D_MODEL] * _dot(at_ref[...].astype(BF16), wa_ref[...])
         + gm[:, D_MODEL:] * _dot(o.astype(BF16), wl_ref[...]))
    mix = _dot(y.astype(BF16), wo_ref[...])
    x1 = _layer_norm(alpha * x_ref[...] + m[2:3] * mix, lg_ref[...], lb_ref[...])
    x1_ref[...] = x1
    ht_ref[...] = (x1 * (1.0 + m[4:5]) + m[3:4]).T.astype(BF16)


def _postmix(x, mod, attn, og, go, gm, lw, seq_len, alpha):
    tokens = x.shape[0]
    tm = TOKEN_TILE
    row = lambda w: pl.BlockSpec((tm, w), lambda i: (i, 0))
    consts = [lw["wa"], lw["wl"], lw["wo"], lw["gn"], lw["ln1g"], lw["ln1b"]]
    return pl.pallas_call(
        functools.partial(_postmix_kernel, alpha),
        out_shape=[jax.ShapeDtypeStruct((tokens, D_MODEL), F32),
                   jax.ShapeDtypeStruct((D_MODEL, tokens), BF16)],
        grid=(tokens // tm,),
        in_specs=[row(D_MODEL),
                  pl.BlockSpec((1, 6, D_MODEL), lambda i: ((i * tm) // seq_len % mod.shape[0], 0, 0)),
                  row(D_MODEL), row(D_MODEL), row(D_MODEL), row(2 * D_MODEL)]
                 + [_const_spec(a.shape) for a in consts],
        out_specs=[row(D_MODEL), pl.BlockSpec((D_MODEL, tm), lambda i: (0, i))],
        compiler_params=_params("parallel"),
        name="postmix",
    )(x, mod, attn, og, go, gm, *consts)


def _peer_candidate_tables():
    groups = [[(0, r) for r in range(16)], [(r, 0) for r in range(16)]]
    for t in (1, 2, 3):
        groups.append([(t, r) for r in range(8)])
        if t < 3:
            groups.append([(r, t) for r in range(8)])
    seen = set()
    ci, neg = [], []
    for grp in groups:
        for (r1, r2) in grp:
            ok = (r1 + 1) * (r2 + 1) <= PEER_TOPK and (r1, r2) not in seen
            if ok:
                seen.add((r1, r2))
            ci.append(float(r1 * PEER_TOPK + r2) if ok else 1e9)
            neg.append(0.0 if ok else -np.inf)
    tab = np.stack([np.asarray(ci, np.float32), np.asarray(neg, np.float32)])
    return np.ascontiguousarray(np.broadcast_to(tab[:, :, None], tab.shape + (LANES,)))


def _extract_top(s, exact_ties):
    key = lax.broadcasted_iota(jnp.int32, s.shape, 0).astype(F32)
    slot = lax.broadcasted_iota(jnp.int32, (PEER_TOPK, s.shape[1]), 0)
    rank = jnp.full(s.shape, float(PEER_TOPK), F32)
    vals = jnp.zeros((PEER_TOPK, s.shape[1]), F32)
    for r in range(PEER_TOPK):
        m = jnp.max(s, axis=0, keepdims=True)
        hit = s == m
        if exact_ties:
            hit = key == jnp.min(jnp.where(hit, key, float(N_KEYS)), axis=0, keepdims=True)
        rank = jnp.where(hit, float(r), rank)
        s = jnp.where(hit, -jnp.inf, s)
        vals = jnp.where(slot == r, m, vals)
    ranked = jnp.sum(jnp.where(rank < float(PEER_TOPK), 1.0, 0.0), axis=0, keepdims=True)
    return vals, rank, ranked


def _candidate_counts(v1, v2, ci, neg, exact_ties):
    lo = slice(0, 8)
    cand = jnp.concatenate([
        v1[0:1] + v2, v1 + v2[0:1],
        v1[1:2] + v2[lo], v1[lo] + v2[1:2],
        v1[2:3] + v2[lo], v1[lo] + v2[2:3],
        v1[3:4] + v2[lo]], axis=0) + neg
    taken = jnp.zeros(cand.shape, F32)
    for _ in range(PEER_TOPK):
        m = jnp.max(cand, axis=0, keepdims=True)
        hit = cand == m
        if exact_ties:
            hit = ci == jnp.min(jnp.where(hit, ci, 2e9), axis=0, keepdims=True)
        taken = jnp.where(hit, 1.0, taken)
        cand = jnp.where(hit, -jnp.inf, cand)
    row_sum = lambda a, b: jnp.sum(taken[a:b], axis=0, keepdims=True)
    slot = lax.broadcasted_iota(jnp.int32, v1.shape, 0)
    counts = taken[16:32] + jnp.concatenate(
        [taken[40:48] + taken[56:64], jnp.zeros((8, v1.shape[1]), F32)], axis=0)
    counts += jnp.where(slot == 0, row_sum(0, 16), 0.0)
    counts += jnp.where(slot == 1, row_sum(32, 40), 0.0)
    counts += jnp.where(slot == 2, row_sum(48, 56), 0.0)
    counts += jnp.where(slot == 3, row_sum(64, 72), 0.0)
    return counts, jnp.sum(counts, axis=0, keepdims=True)


def _route_kernel(ht_ref, wq_ref, keys_ref, tab_ref, cnt_out, p1_out, rk_out, p2_out,
                  q_ref, v1_ref, v2_ref, rank1_ref, rank2_ref, counts_ref):
    q_ref[...] = _dot(wq_ref[...], ht_ref[...])
    ci = tab_ref[0]
    neg = tab_ref[1]

    def head(h, carry):
        r0 = pl.multiple_of(h * PEER_QDIM, PEER_QDIM)
        s1 = _dot(keys_ref[0], q_ref[pl.ds(r0, PEER_HALF), :].astype(BF16))
        s2 = _dot(keys_ref[1], q_ref[pl.ds(r0 + PEER_HALF, PEER_HALF), :].astype(BF16))

        def select(exact_ties):
            v1, rank1, n1 = _extract_top(s1, exact_ties)
            v2, rank2, n2 = _extract_top(s2, exact_ties)
            counts, n3 = _candidate_counts(v1, v2, ci, neg, exact_ties)
            v1_ref[...], v2_ref[...], counts_ref[...] = v1, v2, counts
            rank1_ref[...], rank2_ref[...] = rank1, rank2
            full = float(PEER_TOPK)
            return jnp.where((n1 == full) & (n2 == full) & (n3 == full), 0.0, 1.0)

        tied = jnp.max(select(False))

        @pl.when(tied > 0.0)
        def _():
            select(True)

        v1, v2, counts = v1_ref[...], v2_ref[...], counts_ref[...]
        rank1 = rank1_ref[...]
        cnt = jnp.zeros(s1.shape, F32)
        for r in range(PEER_TOPK):
            cnt = jnp.where(rank1 == float(r), counts[r:r + 1], cnt)
        e1 = jnp.exp(v1 - v1[0:1])
        e2 = jnp.exp(v2 - v2[0:1])
        inner = jnp.zeros(v1.shape, F32)
        for r in range(PEER_TOPK):
            inner += jnp.where(counts > float(r), e2[r:r + 1], 0.0)
        z = jnp.sum(e1 * inner, axis=0, keepdims=True)
        rows = pl.ds(pl.multiple_of(h * N_KEYS, N_KEYS), N_KEYS)
        cnt_out[rows, :] = cnt
        p1_out[rows, :] = jnp.exp(s1 - v1[0:1])
        half = pl.ds(pl.multiple_of(h * (N_KEYS // 2), N_KEYS // 2), N_KEYS // 2)
        rk_out[half, :] = pltpu.bitcast(rank2_ref[...].astype(BF16), jnp.uint32)
        p2_out[half, :] = pltpu.bitcast((jnp.exp(s2 - v2[0:1]) / z).astype(BF16), jnp.uint32)
        return carry

    lax.fori_loop(0, PEER_HEADS, head, 0)


def _peer_route(ht, lw, tab):
    tokens = ht.shape[1]
    n_rows = PEER_HEADS * N_KEYS
    dense = lambda dt, rows: jax.ShapeDtypeStruct((rows, tokens), dt)
    out_spec = lambda rows: pl.BlockSpec((rows, LANES), lambda i: (0, i))
    return pl.pallas_call(
        _route_kernel,
        out_shape=[dense(F32, n_rows), dense(F32, n_rows),
                   dense(jnp.uint32, n_rows // 2), dense(jnp.uint32, n_rows // 2)],
        grid=(tokens // LANES,),
        in_specs=[pl.BlockSpec((D_MODEL, LANES), lambda i: (0, i)),
                  _const_spec(lw["pwq"].shape), _const_spec(lw["pkeys"].shape), _const_spec(tab.shape)],
        out_specs=[out_spec(n_rows), out_spec(n_rows), out_spec(n_rows // 2), out_spec(n_rows // 2)],
        scratch_shapes=[pltpu.VMEM((PEER_HEADS * PEER_QDIM, LANES), F32),
                        pltpu.VMEM((PEER_TOPK, LANES), F32), pltpu.VMEM((PEER_TOPK, LANES), F32),
                        pltpu.VMEM((N_KEYS, LANES), F32), pltpu.VMEM((N_KEYS, LANES), F32),
                        pltpu.VMEM((PEER_TOPK, LANES), F32)],
        compiler_params=_params("parallel"),
        name="peer_route",
    )(ht, lw["pwq"], lw["pkeys"], tab)


def _gelu_tanh(x):
    k1 = -2.0 * 0.7978845608028654 * 1.4426950408889634
    e = jnp.exp2(x * (k1 + (k1 * 0.044715) * (x * x)))
    return x / (1.0 + e)


def _peer_kernel(alpha, ht_ref, u_ref, vt_ref, cnt_ref, p1_ref, rk_ref, p2_ref, x1_ref, mod_ref,
                 lg_ref, lb_ref, o_ref, acc_ref, a_ref, g_ref):
    j = pl.program_id(1)
    tb = ht_ref.shape[1]
    first_keys = u_ref.shape[0] // N_KEYS

    @pl.when(j == 0)
    def _():
        acc_ref[...] = jnp.zeros_like(acc_ref)

    head_rows = lambda h: pl.ds(pl.multiple_of(h * N_KEYS + j * first_keys, first_keys), first_keys)
    ht = ht_ref[...]
    n_chunks = first_keys // PEER_CHUNK_KEYS
    chunk_rows = lambda c: slice(c * PEER_CHUNK_KEYS * N_KEYS, (c + 1) * PEER_CHUNK_KEYS * N_KEYS)

    def activations(c):
        a_ref[chunk_rows(c), :] = _dot(u_ref[chunk_rows(c), :], ht)

    activations(0)
    for c in range(n_chunks):
        rows = chunk_rows(c)
        if c + 1 < n_chunks:
            activations(c + 1)
        for a in range(c * PEER_CHUNK_KEYS, (c + 1) * PEER_CHUNK_KEYS):
            for lg in range(tb // LANES):
                ls = slice(lg * LANES, (lg + 1) * LANES)
                row_a = lambda ref, h: jnp.broadcast_to(ref[head_rows(h), ls][a:a + 1], (16, LANES)).astype(BF16)
                cb = [row_a(cnt_ref, h) for h in range(PEER_HEADS)]
                pb = [row_a(p1_ref, h) for h in range(PEER_HEADS)]
                for g in range(N_KEYS // 16):
                    w = jnp.zeros((16, LANES), BF16)
                    for h in range(PEER_HEADS):
                        ks = slice(h * (N_KEYS // 2) + g * 8, h * (N_KEYS // 2) + (g + 1) * 8)
                        rk = pltpu.bitcast(rk_ref[ks, ls], BF16)
                        p2 = pltpu.bitcast(p2_ref[ks, ls], BF16)
                        w += jnp.where(rk < cb[h], p2, 0.0) * pb[h]
                    er = slice(a * N_KEYS + g * 16, a * N_KEYS + (g + 1) * 16)
                    g_ref[er, ls] = w * _gelu_tanh(a_ref[er, ls]).astype(BF16)
        acc_ref[...] += _dot(vt_ref[:, rows], g_ref[rows, :])

    @pl.when(j == pl.num_programs(1) - 1)
    def _():
        m = mod_ref[0]
        o_ref[...] = _layer_norm(alpha * x1_ref[...] + m[5:6] * acc_ref[...].T, lg_ref[...], lb_ref[...])


def _peer_dense(ht, route, x1, mod, lw, seq_len, alpha):
    tokens = x1.shape[0]
    tb = min(PEER_TOKEN_TILE, tokens)
    et = PEER_EXPERT_TILE
    assert et // N_KEYS == 8, "one sublane tile of first-key rows per expert tile"
    dense = pl.BlockSpec((PEER_HEADS * N_KEYS, tb), lambda i, j: (0, i))
    packed = pl.BlockSpec((PEER_HEADS * N_KEYS // 2, tb), lambda i, j: (0, i))
    return pl.pallas_call(
        functools.partial(_peer_kernel, alpha),
        out_shape=jax.ShapeDtypeStruct((tokens, D_MODEL), F32),
        grid=(tokens // tb, N_EXPERTS // et),
        in_specs=[pl.BlockSpec((D_MODEL, tb), lambda i, j: (0, i)),
                  pl.BlockSpec((et, D_MODEL), lambda i, j: (j, 0)),
                  pl.BlockSpec((D_MODEL, et), lambda i, j: (0, j)),
                  dense, dense, packed, packed,
                  pl.BlockSpec((tb, D_MODEL), lambda i, j: (i, 0)),
                  pl.BlockSpec((1, 6, D_MODEL), lambda i, j: ((i * tb) // seq_len % mod.shape[0], 0, 0)),
                  _const_spec(lw["ln2g"].shape), _const_spec(lw["ln2b"].shape)],
        out_specs=pl.BlockSpec((tb, D_MODEL), lambda i, j: (i, 0)),
        scratch_shapes=[pltpu.VMEM((D_MODEL, tb), F32),
                        pltpu.VMEM((et, tb), F32),
                        pltpu.VMEM((et, tb), BF16)],
        compiler_params=_params("parallel", "arbitrary"),
        name="peer_dense",
    )(ht, lw["pu"], lw["pvt"], *route, x1, mod, lw["ln2g"], lw["ln2b"])


def _rope_tables(seq_len):
    rows = seq_len // GRID_W
    r = jnp.repeat(jnp.arange(rows, dtype=F32), GRID_W)
    col = jnp.tile(jnp.arange(GRID_W, dtype=F32), rows)
    inv = ROPE_THETA ** (-jnp.arange(ROPE_FREQS, dtype=F32) / ROPE_FREQS)
    ang = jnp.stack([r[:, None] * inv, col[:, None] * inv], axis=1)
    cos, sin = jnp.cos(ang), jnp.sin(ang)
    cos_h = jnp.concatenate([cos, cos], axis=-1).reshape(seq_len, HEAD_DIM)
    sin_h = jnp.concatenate([-sin, sin], axis=-1).reshape(seq_len, HEAD_DIM)
    return jnp.tile(cos_h, (1, N_HEADS)), jnp.tile(sin_h, (1, N_HEADS))


def _layer_weights(l, w_in, q_norm, k_norm, gate_w2, gate_b, gla_norm, w_attn_o, w_gla_o, w_out,
                   ln1_g, ln1_b, ln2_g, ln2_b, peer_wq, peer_sub_keys, peer_u, peer_v):
    w = w_in[l].astype(BF16)
    o_q, o_k, o_g, o_lr, o_gm = 0, D_MODEL, D_MODEL + 2 * KV_WIDTH, 0, 0
    o_lr = o_g + 2 * GLA_QK_WIDTH + 2 * GLA_V_WIDTH
    o_gm = o_lr + 2 * GATE_RANK
    head_id = np.arange(D_MODEL) // HEAD_DIM
    mq = jnp.asarray((head_id[:, None] == head_id[None, :]).astype(np.float32) / HEAD_DIM, BF16)
    w2 = jnp.zeros((LANES, 2 * GLA_QK_WIDTH), F32)
    w2 = w2.at[:GATE_RANK, :GLA_QK_WIDTH].set(gate_w2[l, 0])
    w2 = w2.at[GATE_RANK:2 * GATE_RANK, GLA_QK_WIDTH:].set(gate_w2[l, 1])
    row = lambda a: a.reshape(1, -1)
    return dict(
        wq=w[:, o_q:o_k], wkv=w[:, o_k:o_g], wg=w[:, o_g:o_lr],
        wglr=jnp.pad(w[:, o_lr:o_gm], ((0, 0), (0, LANES - 2 * GATE_RANK))),
        wgm=w[:, o_gm:], mq=mq,
        qg=row(jnp.tile(q_norm[l], N_HEADS)), kg=row(jnp.tile(k_norm[l], KV_HEADS)),
        w2=w2.astype(BF16), gb=row(gate_b[l]),
        wa=w_attn_o[l].astype(BF16), wl=w_gla_o[l].astype(BF16), wo=w_out[l].astype(BF16),
        gn=row(jnp.tile(gla_norm[l], GLA_HEADS)),
        ln1g=row(ln1_g[l]), ln1b=row(ln1_b[l]), ln2g=row(ln2_g[l]), ln2b=row(ln2_b[l]),
        pwq=peer_wq[l].T.astype(BF16), pkeys=peer_sub_keys[l].astype(BF16),
        pu=peer_u[l].astype(BF16), pvt=peer_v[l].T.astype(BF16),
    )


def _trunk_layer(x, mod, lw, batch, seq_len, alpha, consts, ctx, layer):
    rope_tabs = None if ctx is None else consts["rope"]
    q, k, v, gq, gk, gv, go, la, gm = _inproj(x, mod, lw, seq_len, rope_tabs)
    if ctx is None:
        attn = _attention(q, k, v, batch, seq_len, min(seq_len, 256), None, layer)
        og, states = _gla(gq, gk, gv, la, batch, seq_len, consts["tri"], None, layer)
    else:
        attn = _attention(q, k, v, batch, seq_len, 128, ctx[:2], layer)
        og, states = _gla(gq, gk, gv, la, batch, seq_len, consts["tri"], ctx[2], layer)
    x1, ht = _postmix(x, mod, attn, og, go, gm, lw, seq_len, alpha)
    route = _peer_route(ht, lw, consts["cand"])
    x2 = _peer_dense(ht, route, x1, mod, lw, seq_len, alpha)
    return x2, (k, v, states)


def kernel(x_prompt, x_sample, cache_k, cache_v, state_gla, c, c_ctx, ada_w, ada_b, w_in, q_norm, k_norm,
           gate_w2, gate_b, gla_norm, w_attn_o, w_gla_o, w_out, ln1_g, ln1_b, ln2_g, ln2_b,
           peer_wq, peer_sub_keys, peer_u, peer_v):
    depth = ada_w.shape[0]
    alpha = (2.0 * depth) ** 0.25
    batch, seq, _ = x_prompt.shape
    dec_batch, dec_seq, _ = x_sample.shape
    past = cache_k.shape[2]

    n_cond = 1 + dec_batch
    cond = jnp.concatenate([c_ctx[None], c, jnp.zeros((-n_cond % 8, D_MODEL), F32)], axis=0)
    mod = _ada_mod(cond, ada_w, ada_b).reshape(depth, cond.shape[0], 6, D_MODEL)

    idx = np.arange(GLA_CHUNK)
    tri = jnp.asarray(np.stack([idx[None, :] <= idx[:, None], idx[None, :] >= idx[:, None]]), BF16)
    consts = dict(tri=tri, cand=jnp.asarray(_peer_candidate_tables()), rope=_rope_tables(dec_seq))
    weights = [_layer_weights(l, w_in, q_norm, k_norm, gate_w2, gate_b, gla_norm, w_attn_o, w_gla_o, w_out,
                              ln1_g, ln1_b, ln2_g, ln2_b, peer_wq, peer_sub_keys, peer_u, peer_v)
               for l in range(depth)]

    xp = x_prompt.reshape(batch * seq, D_MODEL)
    ks, vs, ss = [], [], []
    for l in range(depth):
        xp, (k_l, v_l, s_l) = _trunk_layer(xp, mod[l, 0:1], weights[l], batch, seq, alpha, consts, None, l)
        ks.append(k_l.reshape(batch, seq, KV_HEADS, HEAD_DIM))
        vs.append(v_l.reshape(batch, seq, KV_HEADS, HEAD_DIM))
        ss.append(s_l)
    new_cache_k = jnp.stack(ks, axis=1)
    new_cache_v = jnp.stack(vs, axis=1)
    new_state = jnp.stack(ss, axis=1)

    ctx = (cache_k.reshape(dec_batch, depth, past, KV_WIDTH), cache_v.reshape(dec_batch, depth, past, KV_WIDTH),
           state_gla)
    xs = x_sample.reshape(dec_batch * dec_seq, D_MODEL)
    for l in range(depth):
        xs, _ = _trunk_layer(xs, mod[l, 1:1 + dec_batch], weights[l], dec_batch, dec_seq, alpha, consts, ctx, l)

    return (xp.reshape(batch, seq, D_MODEL), xs.reshape(dec_batch, dec_seq, D_MODEL),
            new_cache_k, new_cache_v, new_state)
```

```python
import functools

import numpy as np
import jax
import jax.numpy as jnp
from jax import lax
from jax.experimental import pallas as pl
from jax.experimental.pallas import tpu as pltpu

F32 = jnp.float32
BF16 = jnp.bfloat16

D_MODEL = 1024
HEAD_DIM = 64
N_HEADS = D_MODEL // HEAD_DIM
KV_HEADS = N_HEADS // 4
Q_PER_KV = N_HEADS // KV_HEADS
KV_WIDTH = KV_HEADS * HEAD_DIM
GRID_W = 64
ROPE_FREQS = HEAD_DIM // 4
ROPE_THETA = 10000.0
GLA_HEADS = 4
GLA_DK = D_MODEL // 2 // GLA_HEADS
GLA_DV = D_MODEL // GLA_HEADS
GLA_QK_WIDTH = GLA_HEADS * GLA_DK
GLA_V_WIDTH = GLA_HEADS * GLA_DV
GATE_RANK = 16
GATE_NORM = 16.0
GLA_CHUNK = 64
N_KEYS = 128
N_EXPERTS = N_KEYS * N_KEYS
PEER_HEADS = 8
PEER_TOPK = 16
PEER_QDIM = 256
PEER_HALF = PEER_QDIM // 2
LN_EPS = 1e-5
RMS_EPS = 1e-6

V7X_VMEM_BYTES = 64 * 1024 * 1024
VMEM_LIMIT = V7X_VMEM_BYTES - 8 * 1024 * 1024
LANES = 128

TOKEN_TILE = 256
PEER_TOKEN_TILE = 512
ROUTE_TOKEN_TILE = 512
PEER_EXPERT_TILE = 1024
PEER_CHUNK_KEYS = 4


def _dot(a, b):
    return jnp.dot(a, b, preferred_element_type=F32)


def _dot_nt(a, b):
    return lax.dot_general(a, b, (((1,), (1,)), ((), ())), preferred_element_type=F32)


def _dot_tn(a, b):
    return lax.dot_general(a, b, (((0,), (0,)), ((), ())), preferred_element_type=F32)


def _const_spec(shape):
    zeros = (0,) * len(shape)
    return pl.BlockSpec(shape, lambda *_: zeros)


def _params(*sem):
    return pltpu.CompilerParams(dimension_semantics=sem, vmem_limit_bytes=VMEM_LIMIT)


def _layer_norm(x, g, b):
    mu = jnp.mean(x, axis=-1, keepdims=True)
    xc = x - mu
    var = jnp.mean(xc * xc, axis=-1, keepdims=True)
    return xc * lax.rsqrt(var + LN_EPS) * g + b


def _ada_kernel(c_ref, w_ref, b_ref, o_ref):
    c = c_ref[...]
    s = c * jax.nn.sigmoid(c)
    o_ref[...] = _dot(s.astype(BF16), w_ref[...].astype(BF16)) + b_ref[...]


def _ada_mod(cond, ada_w, ada_b):
    depth = ada_w.shape[0]
    rows = cond.shape[0]
    return pl.pallas_call(
        _ada_kernel,
        out_shape=jax.ShapeDtypeStruct((depth, rows, 6 * D_MODEL), F32),
        grid=(depth, 6),
        in_specs=[
            pl.BlockSpec((rows, D_MODEL), lambda l, j: (0, 0)),
            pl.BlockSpec((None, D_MODEL, D_MODEL), lambda l, j: (l, 0, j)),
            pl.BlockSpec((None, 1, D_MODEL), lambda l, j: (l, 0, j)),
        ],
        out_specs=pl.BlockSpec((None, rows, D_MODEL), lambda l, j: (l, 0, j)),
        compiler_params=_params("parallel", "parallel"),
        name="ada_mod",
    )(cond, ada_w, ada_b.reshape(depth, 1, 6 * D_MODEL))


def _rope(t, cos, sin_signed):
    width = t.shape[-1]
    up = pltpu.roll(t, width - ROPE_FREQS, 1)
    dn = pltpu.roll(t, ROPE_FREQS, 1)
    lane = lax.broadcasted_iota(jnp.int32, t.shape, 1)
    partner = jnp.where((lane & ROPE_FREQS) == 0, up, dn)
    return t * cos + partner * sin_signed


def _inproj_kernel(rope, *refs):
    (x_ref, mod_ref, wq_ref, wkv_ref, wg_ref, wglr_ref, wgm_ref, mq_ref, qg_ref, kg_ref,
     w2_ref, gb_ref) = refs[:12]
    refs = refs[12:]
    if rope:
        cos_ref, sin_ref = refs[:2]
        refs = refs[2:]
    q_out, k_out, v_out, gq_out, gk_out, gv_out, go_out, la_out, gm_out = refs

    m = mod_ref[0]
    h = (x_ref[...] * (1.0 + m[1:2]) + m[0:1]).astype(BF16)

    q = _dot(h, wq_ref[...])
    qn = q * lax.rsqrt(_dot((q * q).astype(BF16), mq_ref[...]) + RMS_EPS) * qg_ref[...]
    kv = _dot(h, wkv_ref[...])
    k = kv[:, :KV_WIDTH]
    kn = k * lax.rsqrt(_dot((k * k).astype(BF16), mq_ref[:KV_WIDTH, :KV_WIDTH]) + RMS_EPS) * kg_ref[...]
    if rope:
        cos = cos_ref[...]
        sin = sin_ref[...]
        qn = _rope(qn, cos, sin)
        kn = _rope(kn, cos[:, :KV_WIDTH], sin[:, :KV_WIDTH])
    q_out[...] = qn * (HEAD_DIM ** -0.5)
    k_out[...] = kn
    v_out[...] = kv[:, KV_WIDTH:]

    g = _dot(h, wg_ref[...])
    gq_out[...] = g[:, :GLA_QK_WIDTH] * (GLA_DK ** -0.5)
    gk_out[...] = g[:, GLA_QK_WIDTH:2 * GLA_QK_WIDTH]
    gv_out[...] = g[:, 2 * GLA_QK_WIDTH:2 * GLA_QK_WIDTH + GLA_V_WIDTH]
    go_out[...] = g[:, 2 * GLA_QK_WIDTH + GLA_V_WIDTH:]

    glr = _dot(h, wglr_ref[...])
    z = _dot(glr.astype(BF16), w2_ref[...]) + gb_ref[...]
    la_out[...] = (jnp.minimum(z, 0.0) - jnp.log1p(jnp.exp(-jnp.abs(z)))) * (1.0 / GATE_NORM)
    gm_out[...] = jax.nn.sigmoid(_dot(h, wgm_ref[...]))


def _inproj(x, mod, lw, seq_len, rope_tabs):
    tokens = x.shape[0]
    tm = TOKEN_TILE
    rope = rope_tabs is not None
    row = lambda i: (i, 0)
    ins = [x, mod, lw["wq"], lw["wkv"], lw["wg"], lw["wglr"], lw["wgm"], lw["mq"], lw["qg"], lw["kg"],
           lw["w2"], lw["gb"]]
    in_specs = [pl.BlockSpec((tm, D_MODEL), row),
                pl.BlockSpec((1, 6, D_MODEL), lambda i: ((i * tm) // seq_len % mod.shape[0], 0, 0))]
    in_specs += [_const_spec(a.shape) for a in ins[2:]]
    if rope:
        per_seq = seq_len // tm
        ins += list(rope_tabs)
        in_specs += [pl.BlockSpec((tm, D_MODEL), lambda i: (i % per_seq, 0))] * 2
    widths = (D_MODEL, KV_WIDTH, KV_WIDTH, GLA_QK_WIDTH, GLA_QK_WIDTH, GLA_V_WIDTH, GLA_V_WIDTH,
              2 * GLA_QK_WIDTH, 2 * D_MODEL)
    return pl.pallas_call(
        functools.partial(_inproj_kernel, rope),
        out_shape=[jax.ShapeDtypeStruct((tokens, w), F32) for w in widths],
        grid=(tokens // tm,),
        in_specs=in_specs,
        out_specs=[pl.BlockSpec((tm, w), row) for w in widths],
        compiler_params=_params("parallel"),
        name="inproj_rope" if rope else "inproj",
    )(*ins)


def _attn_kernel(has_ctx, *refs):
    if has_ctx:
        q_ref, k_ref, v_ref, ck_ref, cv_ref, o_ref = refs
    else:
        q_ref, k_ref, v_ref, o_ref = refs
    tq = q_ref.shape[0]
    for g in range(KV_HEADS):
        gs = slice(g * HEAD_DIM, (g + 1) * HEAD_DIM)
        kg = k_ref[:, gs].astype(BF16)
        vg = v_ref[:, gs].astype(BF16)
        if has_ctx:
            kg = jnp.concatenate([kg, ck_ref[:, gs].astype(BF16)], axis=0)
            vg = jnp.concatenate([vg, cv_ref[:, gs].astype(BF16)], axis=0)
        heads = [q_ref[:, (Q_PER_KV * g + r) * HEAD_DIM:(Q_PER_KV * g + r + 1) * HEAD_DIM]
                 for r in range(Q_PER_KV)]
        qs = jnp.concatenate(heads, axis=0).astype(BF16)
        s = _dot_nt(qs, kg)
        p = jnp.exp(s - jnp.max(s, axis=-1, keepdims=True))
        o = _dot(p.astype(BF16), vg) / jnp.sum(p, axis=-1, keepdims=True)
        for r in range(Q_PER_KV):
            h0 = (Q_PER_KV * g + r) * HEAD_DIM
            o_ref[:, h0:h0 + HEAD_DIM] = o[r * tq:(r + 1) * tq]


def _attention(q, k, v, batch, seq_len, tq, ctx_kv, layer):
    tokens = q.shape[0]
    nq = seq_len // tq
    ins = [q, k, v]
    in_specs = [pl.BlockSpec((tq, D_MODEL), lambda b, i: (b * nq + i, 0)),
                pl.BlockSpec((seq_len, KV_WIDTH), lambda b, i: (b, 0)),
                pl.BlockSpec((seq_len, KV_WIDTH), lambda b, i: (b, 0))]
    if ctx_kv is not None:
        past = ctx_kv[0].shape[2]
        ins += list(ctx_kv)
        in_specs += [pl.BlockSpec((None, None, past, KV_WIDTH), lambda b, i: (b, layer, 0, 0))] * 2
    return pl.pallas_call(
        functools.partial(_attn_kernel, ctx_kv is not None),
        out_shape=jax.ShapeDtypeStruct((tokens, D_MODEL), F32),
        grid=(batch, nq),
        in_specs=in_specs,
        out_specs=pl.BlockSpec((tq, D_MODEL), lambda b, i: (b * nq + i, 0)),
        compiler_params=_params("parallel", "parallel"),
        name="attention_ctx" if ctx_kv is not None else "attention",
    )(*ins)


def _split3(x):
    hi = x.astype(BF16)
    r = x - hi.astype(F32)
    mid = r.astype(BF16)
    lo = (r - mid.astype(F32)).astype(BF16)
    return hi, mid, lo


def _gla_kernel(has_s0, *refs):
    gq_ref, gk_ref, gv_ref, la_ref, tri_ref = refs[:5]
    refs = refs[5:]
    if has_s0:
        s0_ref = refs[0]
        refs = refs[1:]
    o_ref, st_out, bf_ref, bb_ref, st_ref = refs
    seq_len = gq_ref.shape[0]
    n_chunks = seq_len // GLA_CHUNK
    tril = tri_ref[0]
    triu = tri_ref[1]

    def cumsum_chunk(c, carry):
        rows = pl.ds(pl.multiple_of(c * GLA_CHUNK, GLA_CHUNK), GLA_CHUNK)
        la = la_ref[rows, :]
        pf = _split3(la[:, :GLA_QK_WIDTH])
        pb = _split3(la[:, GLA_QK_WIDTH:])
        bf_ref[rows, :] = _dot(tril, pf[0]) + _dot(tril, pf[1]) + _dot(tril, pf[2])
        bb_ref[rows, :] = _dot(triu, pb[0]) + _dot(triu, pb[1]) + _dot(triu, pb[2])
        return carry

    lax.fori_loop(0, n_chunks, cumsum_chunk, 0)

    ri = lax.broadcasted_iota(jnp.int32, (GLA_CHUNK, GLA_CHUNK), 0)
    ci = lax.broadcasted_iota(jnp.int32, (GLA_CHUNK, GLA_CHUNK), 1)
    for d in range(2):
        for h in range(GLA_HEADS):
            if has_s0:
                st_ref[d * GLA_HEADS + h] = s0_ref[d, h].T
            else:
                st_ref[d * GLA_HEADS + h] = jnp.zeros((GLA_DV, GLA_DK), F32)
    o_ref[...] = jnp.zeros_like(o_ref)

    def step(i, carry):
        for d in range(2):
            b_ref = bf_ref if d == 0 else bb_ref
            keep = (ci <= ri) if d == 0 else (ci >= ri)
            c = i if d == 0 else n_chunks - 1 - i
            rows = pl.ds(pl.multiple_of(c * GLA_CHUNK, GLA_CHUNK), GLA_CHUNK)
            for h in range(GLA_HEADS):
                ks = slice(h * GLA_DK, (h + 1) * GLA_DK)
                vs = slice(h * GLA_DV, (h + 1) * GLA_DV)
                b = b_ref[rows, ks]
                bl = b[GLA_CHUNK - 1:GLA_CHUNK] if d == 0 else b[0:1]
                kk = gk_ref[rows, ks]
                v = gv_ref[rows, vs].astype(BF16)
                qe = (gq_ref[rows, ks] * jnp.exp(b)).astype(BF16)
                ke = (kk * jnp.exp(-b)).astype(BF16)
                kl = (kk * jnp.exp(bl - b)).astype(BF16)
                a = jnp.where(keep, _dot_nt(qe, ke), 0.0).astype(BF16)
                st = st_ref[d * GLA_HEADS + h]
                o_ref[rows, vs] += _dot(a, v) + _dot_nt(qe, st.astype(BF16))
                st_ref[d * GLA_HEADS + h] = st * jnp.exp(bl) + _dot_tn(v, kl)
        return carry

    lax.fori_loop(0, n_chunks, step, 0)
    for d in range(2):
        for h in range(GLA_HEADS):
            st_out[d, h] = st_ref[d * GLA_HEADS + h].T


def _gla(gq, gk, gv, la, batch, seq_len, tri, s0, layer):
    tokens = gq.shape[0]
    seq = lambda w: pl.BlockSpec((seq_len, w), lambda b: (b, 0))
    ins = [gq, gk, gv, la, tri]
    in_specs = [seq(GLA_QK_WIDTH), seq(GLA_QK_WIDTH), seq(GLA_V_WIDTH), seq(2 * GLA_QK_WIDTH),
                _const_spec(tri.shape)]
    if s0 is not None:
        ins.append(s0)
        in_specs.append(pl.BlockSpec((None, None, 2, GLA_HEADS, GLA_DK, GLA_DV),
                                     lambda b: (b, layer, 0, 0, 0, 0)))
    return pl.pallas_call(
        functools.partial(_gla_kernel, s0 is not None),
        out_shape=[jax.ShapeDtypeStruct((tokens, GLA_V_WIDTH), F32),
                   jax.ShapeDtypeStruct((batch, 2, GLA_HEADS, GLA_DK, GLA_DV), F32)],
        grid=(batch,),
        in_specs=in_specs,
        out_specs=[seq(GLA_V_WIDTH),
                   pl.BlockSpec((None, 2, GLA_HEADS, GLA_DK, GLA_DV), lambda b: (b, 0, 0, 0, 0))],
        scratch_shapes=[pltpu.VMEM((seq_len, GLA_QK_WIDTH), F32),
                        pltpu.VMEM((seq_len, GLA_QK_WIDTH), F32),
                        pltpu.VMEM((2 * GLA_HEADS, GLA_DV, GLA_DK), F32)],
        compiler_params=_params("parallel"),
        name="gla_s0" if s0 is not None else "gla",
    )(*ins)


def _postmix_kernel(alpha, x_ref, mod_ref, at_ref, og_ref, go_ref, gm_ref, wa_ref, wl_ref, wo_ref,
                    gn_ref, lg_ref, lb_ref, x1_ref, ht_ref):
    m = mod_ref[0]
    og = og_ref[...]
    parts = []
    for h in range(GLA_HEADS):
        oh = og[:, h * GLA_DV:(h + 1) * GLA_DV]
        parts.append(oh * lax.rsqrt(jnp.mean(oh * oh, axis=-1, keepdims=True) + RMS_EPS))
    go = go_ref[...]
    o = jnp.concatenate(parts, axis=-1) * gn_ref[...] * (go * jax.nn.sigmoid(go))
    gm = gm_ref[...]
    y = (gm[:, :D_MODEL] * _dot(at_ref[...].astype(BF16), wa_ref[...])
         + gm[:, D_MODEL:] * _dot(o.astype(BF16), wl_ref[...]))
    mix = _dot(y.astype(BF16), wo_ref[...])
    x1 = _layer_norm(alpha * x_ref[...] + m[2:3] * mix, lg_ref[...], lb_ref[...])
    x1_ref[...] = x1
    ht_ref[...] = (x1 * (1.0 + m[4:5]) + m[3:4]).T.astype(BF16)


def _postmix(x, mod, attn, og, go, gm, lw, seq_len, alpha):
    tokens = x.shape[0]
    tm = TOKEN_TILE
    row = lambda w: pl.BlockSpec((tm, w), lambda i: (i, 0))
    consts = [lw["wa"], lw["wl"], lw["wo"], lw["gn"], lw["ln1g"], lw["ln1b"]]
    return pl.pallas_call(
        functools.partial(_postmix_kernel, alpha),
        out_shape=[jax.ShapeDtypeStruct((tokens, D_MODEL), F32),
                   jax.ShapeDtypeStruct((D_MODEL, tokens), BF16)],
        grid=(tokens // tm,),
        in_specs=[row(D_MODEL),
                  pl.BlockSpec((1, 6, D_MODEL), lambda i: ((i * tm) // seq_len % mod.shape[0], 0, 0)),
                  row(D_MODEL), row(D_MODEL), row(D_MODEL), row(2 * D_MODEL)]
                 + [_const_spec(a.shape) for a in consts],
        out_specs=[row(D_MODEL), pl.BlockSpec((D_MODEL, tm), lambda i: (0, i))],
        compiler_params=_params("parallel"),
        name="postmix",
    )(x, mod, attn, og, go, gm, *consts)


def _peer_candidate_tables(lanes):
    groups = [[(0, r) for r in range(16)], [(r, 0) for r in range(16)]]
    for t in (1, 2, 3):
        groups.append([(t, r) for r in range(8)])
        if t < 3:
            groups.append([(r, t) for r in range(8)])
    seen = set()
    ci, neg = [], []
    for grp in groups:
        for (r1, r2) in grp:
            ok = (r1 + 1) * (r2 + 1) <= PEER_TOPK and (r1, r2) not in seen
            if ok:
                seen.add((r1, r2))
            ci.append(float(r1 * PEER_TOPK + r2) if ok else 1e9)
            neg.append(0.0 if ok else -np.inf)
    tab = np.stack([np.asarray(ci, np.float32), np.asarray(neg, np.float32)])
    return np.ascontiguousarray(np.broadcast_to(tab[:, :, None], tab.shape + (lanes,)))


def _extract_top(s, exact_ties):
    key = lax.broadcasted_iota(jnp.int32, s.shape, 0).astype(F32)
    slot = lax.broadcasted_iota(jnp.int32, (PEER_TOPK, s.shape[1]), 0)
    rank = jnp.full(s.shape, float(PEER_TOPK), F32)
    vals = jnp.zeros((PEER_TOPK, s.shape[1]), F32)
    for r in range(PEER_TOPK):
        m = jnp.max(s, axis=0, keepdims=True)
        hit = s == m
        if exact_ties:
            hit = key == jnp.min(jnp.where(hit, key, float(N_KEYS)), axis=0, keepdims=True)
        rank = jnp.where(hit, float(r), rank)
        s = jnp.where(hit, -jnp.inf, s)
        vals = jnp.where(slot == r, m, vals)
    ranked = jnp.sum(jnp.where(rank < float(PEER_TOPK), 1.0, 0.0), axis=0, keepdims=True)
    return vals, rank, ranked


def _candidate_counts(v1, v2, ci, neg, exact_ties):
    lo = slice(0, 8)
    cand = jnp.concatenate([
        v1[0:1] + v2, v1 + v2[0:1],
        v1[1:2] + v2[lo], v1[lo] + v2[1:2],
        v1[2:3] + v2[lo], v1[lo] + v2[2:3],
        v1[3:4] + v2[lo]], axis=0) + neg
    taken = jnp.zeros(cand.shape, F32)
    for _ in range(PEER_TOPK):
        m = jnp.max(cand, axis=0, keepdims=True)
        hit = cand == m
        if exact_ties:
            hit = ci == jnp.min(jnp.where(hit, ci, 2e9), axis=0, keepdims=True)
        taken = jnp.where(hit, 1.0, taken)
        cand = jnp.where(hit, -jnp.inf, cand)
    row_sum = lambda a, b: jnp.sum(taken[a:b], axis=0, keepdims=True)
    slot = lax.broadcasted_iota(jnp.int32, v1.shape, 0)
    counts = taken[16:32] + jnp.concatenate(
        [taken[40:48] + taken[56:64], jnp.zeros((8, v1.shape[1]), F32)], axis=0)
    counts += jnp.where(slot == 0, row_sum(0, 16), 0.0)
    counts += jnp.where(slot == 1, row_sum(32, 40), 0.0)
    counts += jnp.where(slot == 2, row_sum(48, 56), 0.0)
    counts += jnp.where(slot == 3, row_sum(64, 72), 0.0)
    return counts, jnp.sum(counts, axis=0, keepdims=True)


def _route_kernel(ht_ref, wq_ref, keys_ref, tab_ref, cnt_out, p1_out, rk_out, p2_out,
                  q_ref, v1_ref, v2_ref, rank1_ref, rank2_ref, counts_ref):
    q_ref[...] = _dot(wq_ref[...], ht_ref[...])
    ci = tab_ref[0]
    neg = tab_ref[1]

    def head(h, carry):
        r0 = pl.multiple_of(h * PEER_QDIM, PEER_QDIM)
        s1 = _dot(keys_ref[0], q_ref[pl.ds(r0, PEER_HALF), :].astype(BF16))
        s2 = _dot(keys_ref[1], q_ref[pl.ds(r0 + PEER_HALF, PEER_HALF), :].astype(BF16))

        def select(exact_ties):
            v1, rank1, n1 = _extract_top(s1, exact_ties)
            v2, rank2, n2 = _extract_top(s2, exact_ties)
            counts, n3 = _candidate_counts(v1, v2, ci, neg, exact_ties)
            v1_ref[...], v2_ref[...], counts_ref[...] = v1, v2, counts
            rank1_ref[...], rank2_ref[...] = rank1, rank2
            full = float(PEER_TOPK)
            return jnp.where((n1 == full) & (n2 == full) & (n3 == full), 0.0, 1.0)

        tied = jnp.max(select(False))

        @pl.when(tied > 0.0)
        def _():
            select(True)

        v1, v2, counts = v1_ref[...], v2_ref[...], counts_ref[...]
        rank1 = rank1_ref[...]
        cnt = jnp.zeros(s1.shape, F32)
        for r in range(PEER_TOPK):
            cnt = jnp.where(rank1 == float(r), counts[r:r + 1], cnt)
        e1 = jnp.exp(v1 - v1[0:1])
        e2 = jnp.exp(v2 - v2[0:1])
        inner = jnp.zeros(v1.shape, F32)
        for r in range(PEER_TOPK):
            inner += jnp.where(counts > float(r), e2[r:r + 1], 0.0)
        z = jnp.sum(e1 * inner, axis=0, keepdims=True)
        rows = pl.ds(pl.multiple_of(h * N_KEYS, N_KEYS), N_KEYS)
        cnt_out[rows, :] = cnt
        p1_out[rows, :] = jnp.exp(s1 - v1[0:1])
        half = pl.ds(pl.multiple_of(h * (N_KEYS // 2), N_KEYS // 2), N_KEYS // 2)
        rk_out[half, :] = pltpu.bitcast(rank2_ref[...].astype(BF16), jnp.uint32)
        p2_out[half, :] = pltpu.bitcast((jnp.exp(s2 - v2[0:1]) / z).astype(BF16), jnp.uint32)
        return carry

    lax.fori_loop(0, PEER_HEADS, head, 0)


def _peer_route(ht, lw):
    tokens = ht.shape[1]
    rt = min(ROUTE_TOKEN_TILE, tokens)
    tab = jnp.asarray(_peer_candidate_tables(rt))
    n_rows = PEER_HEADS * N_KEYS
    dense = lambda dt, rows: jax.ShapeDtypeStruct((rows, tokens), dt)
    out_spec = lambda rows: pl.BlockSpec((rows, rt), lambda i: (0, i))
    return pl.pallas_call(
        _route_kernel,
        out_shape=[dense(F32, n_rows), dense(F32, n_rows),
                   dense(jnp.uint32, n_rows // 2), dense(jnp.uint32, n_rows // 2)],
        grid=(tokens // rt,),
        in_specs=[pl.BlockSpec((D_MODEL, rt), lambda i: (0, i)),
                  _const_spec(lw["pwq"].shape), _const_spec(lw["pkeys"].shape), _const_spec(tab.shape)],
        out_specs=[out_spec(n_rows), out_spec(n_rows), out_spec(n_rows // 2), out_spec(n_rows // 2)],
        scratch_shapes=[pltpu.VMEM((PEER_HEADS * PEER_QDIM, rt), F32),
                        pltpu.VMEM((PEER_TOPK, rt), F32), pltpu.VMEM((PEER_TOPK, rt), F32),
                        pltpu.VMEM((N_KEYS, rt), F32), pltpu.VMEM((N_KEYS, rt), F32),
                        pltpu.VMEM((PEER_TOPK, rt), F32)],
        compiler_params=_params("parallel"),
        name="peer_route",
    )(ht, lw["pwq"], lw["pkeys"], tab)


def _gelu_tanh(x):
    k1 = -2.0 * 0.7978845608028654 * 1.4426950408889634
    e = jnp.exp2(x * (k1 + (k1 * 0.044715) * (x * x)))
    return x / (1.0 + e)


def _peer_kernel(alpha, ht_ref, u_ref, vt_ref, cnt_ref, p1_ref, rk_ref, p2_ref, x1_ref, mod_ref,
                 lg_ref, lb_ref, o_ref, acc_ref, a_ref, g_ref):
    j = pl.program_id(1)
    tb = ht_ref.shape[1]
    first_keys = u_ref.shape[0] // N_KEYS

    @pl.when(j == 0)
    def _():
        acc_ref[...] = jnp.zeros_like(acc_ref)

    head_rows = lambda h: pl.ds(pl.multiple_of(h * N_KEYS + j * first_keys, first_keys), first_keys)
    ht = ht_ref[...]
    n_chunks = first_keys // PEER_CHUNK_KEYS
    chunk_rows = lambda c: slice(c * PEER_CHUNK_KEYS * N_KEYS, (c + 1) * PEER_CHUNK_KEYS * N_KEYS)

    def activations(c):
        a_ref[chunk_rows(c), :] = _dot(u_ref[chunk_rows(c), :], ht)

    activations(0)
    for c in range(n_chunks):
        rows = chunk_rows(c)
        if c + 1 < n_chunks:
            activations(c + 1)
        for a in range(c * PEER_CHUNK_KEYS, (c + 1) * PEER_CHUNK_KEYS):
            for lg in range(tb // LANES):
                ls = slice(lg * LANES, (lg + 1) * LANES)
                row_a = lambda ref, h: jnp.broadcast_to(ref[head_rows(h), ls][a:a + 1], (16, LANES)).astype(BF16)
                groups = range(N_KEYS // 16)
                w = [jnp.zeros((16, LANES), BF16) for _ in groups]
                for h in range(PEER_HEADS):
                    cb = row_a(cnt_ref, h)
                    pb = row_a(p1_ref, h)
                    for g in groups:
                        ks = slice(h * (N_KEYS // 2) + g * 8, h * (N_KEYS // 2) + (g + 1) * 8)
                        rk = pltpu.bitcast(rk_ref[ks, ls], BF16)
                        p2 = pltpu.bitcast(p2_ref[ks, ls], BF16)
                        w[g] += jnp.where(rk < cb, p2, 0.0) * pb
                for g in groups:
                    er = slice(a * N_KEYS + g * 16, a * N_KEYS + (g + 1) * 16)
                    g_ref[er, ls] = w[g] * _gelu_tanh(a_ref[er, ls]).astype(BF16)
        acc_ref[...] += _dot_tn(vt_ref[rows, :], g_ref[rows, :])

    @pl.when(j == pl.num_programs(1) - 1)
    def _():
        m = mod_ref[0]
        o_ref[...] = _layer_norm(alpha * x1_ref[...] + m[5:6] * acc_ref[...].T, lg_ref[...], lb_ref[...])


def _peer_dense(ht, route, x1, mod, lw, seq_len, alpha):
    tokens = x1.shape[0]
    tb = min(PEER_TOKEN_TILE, tokens)
    et = PEER_EXPERT_TILE
    assert et // N_KEYS == 8, "one sublane tile of first-key rows per expert tile"
    dense = pl.BlockSpec((PEER_HEADS * N_KEYS, tb), lambda i, j: (0, i))
    packed = pl.BlockSpec((PEER_HEADS * N_KEYS // 2, tb), lambda i, j: (0, i))
    return pl.pallas_call(
        functools.partial(_peer_kernel, alpha),
        out_shape=jax.ShapeDtypeStruct((tokens, D_MODEL), F32),
        grid=(tokens // tb, N_EXPERTS // et),
        in_specs=[pl.BlockSpec((D_MODEL, tb), lambda i, j: (0, i)),
                  pl.BlockSpec((et, D_MODEL), lambda i, j: (j, 0)),
                  pl.BlockSpec((et, D_MODEL), lambda i, j: (j, 0)),
                  dense, dense, packed, packed,
                  pl.BlockSpec((tb, D_MODEL), lambda i, j: (i, 0)),
                  pl.BlockSpec((1, 6, D_MODEL), lambda i, j: ((i * tb) // seq_len % mod.shape[0], 0, 0)),
                  _const_spec(lw["ln2g"].shape), _const_spec(lw["ln2b"].shape)],
        out_specs=pl.BlockSpec((tb, D_MODEL), lambda i, j: (i, 0)),
        scratch_shapes=[pltpu.VMEM((D_MODEL, tb), F32),
                        pltpu.VMEM((et, tb), F32),
                        pltpu.VMEM((et, tb), BF16)],
        compiler_params=_params("parallel", "arbitrary"),
        name="peer_dense",
    )(ht, lw["pu"], lw["pvt"], *route, x1, mod, lw["ln2g"], lw["ln2b"])


def _rope_tables(seq_len):
    rows = seq_len // GRID_W
    r = jnp.repeat(jnp.arange(rows, dtype=F32), GRID_W)
    col = jnp.tile(jnp.arange(GRID_W, dtype=F32), rows)
    inv = ROPE_THETA ** (-jnp.arange(ROPE_FREQS, dtype=F32) / ROPE_FREQS)
    ang = jnp.stack([r[:, None] * inv, col[:, None] * inv], axis=1)
    cos, sin = jnp.cos(ang), jnp.sin(ang)
    cos_h = jnp.concatenate([cos, cos], axis=-1).reshape(seq_len, HEAD_DIM)
    sin_h = jnp.concatenate([-sin, sin], axis=-1).reshape(seq_len, HEAD_DIM)
    return jnp.tile(cos_h, (1, N_HEADS)), jnp.tile(sin_h, (1, N_HEADS))


def _layer_weights(l, w_in, q_norm, k_norm, gate_w2, gate_b, gla_norm, w_attn_o, w_gla_o, w_out,
                   ln1_g, ln1_b, ln2_g, ln2_b, peer_wq, peer_sub_keys, peer_u, peer_v):
    w = w_in[l].astype(BF16)
    o_q, o_k, o_g, o_lr, o_gm = 0, D_MODEL, D_MODEL + 2 * KV_WIDTH, 0, 0
    o_lr = o_g + 2 * GLA_QK_WIDTH + 2 * GLA_V_WIDTH
    o_gm = o_lr + 2 * GATE_RANK
    head_id = np.arange(D_MODEL) // HEAD_DIM
    mq = jnp.asarray((head_id[:, None] == head_id[None, :]).astype(np.float32) / HEAD_DIM, BF16)
    w2 = jnp.zeros((LANES, 2 * GLA_QK_WIDTH), F32)
    w2 = w2.at[:GATE_RANK, :GLA_QK_WIDTH].set(gate_w2[l, 0])
    w2 = w2.at[GATE_RANK:2 * GATE_RANK, GLA_QK_WIDTH:].set(gate_w2[l, 1])
    row = lambda a: a.reshape(1, -1)
    return dict(
        wq=w[:, o_q:o_k], wkv=w[:, o_k:o_g], wg=w[:, o_g:o_lr],
        wglr=jnp.pad(w[:, o_lr:o_gm], ((0, 0), (0, LANES - 2 * GATE_RANK))),
        wgm=w[:, o_gm:], mq=mq,
        qg=row(jnp.tile(q_norm[l], N_HEADS)), kg=row(jnp.tile(k_norm[l], KV_HEADS)),
        w2=w2.astype(BF16), gb=row(gate_b[l]),
        wa=w_attn_o[l].astype(BF16), wl=w_gla_o[l].astype(BF16), wo=w_out[l].astype(BF16),
        gn=row(jnp.tile(gla_norm[l], GLA_HEADS)),
        ln1g=row(ln1_g[l]), ln1b=row(ln1_b[l]), ln2g=row(ln2_g[l]), ln2b=row(ln2_b[l]),
        pwq=peer_wq[l].T.astype(BF16), pkeys=peer_sub_keys[l].astype(BF16),
        pu=peer_u[l].astype(BF16), pvt=peer_v[l].astype(BF16),
    )


def _trunk_layer(x, mod, lw, batch, seq_len, alpha, consts, ctx, layer):
    rope_tabs = None if ctx is None else consts["rope"]
    q, k, v, gq, gk, gv, go, la, gm = _inproj(x, mod, lw, seq_len, rope_tabs)
    if ctx is None:
        attn = _attention(q, k, v, batch, seq_len, min(seq_len, 256), None, layer)
        og, states = _gla(gq, gk, gv, la, batch, seq_len, consts["tri"], None, layer)
    else:
        attn = _attention(q, k, v, batch, seq_len, 128, ctx[:2], layer)
        og, states = _gla(gq, gk, gv, la, batch, seq_len, consts["tri"], ctx[2], layer)
    x1, ht = _postmix(x, mod, attn, og, go, gm, lw, seq_len, alpha)
    route = _peer_route(ht, lw)
    x2 = _peer_dense(ht, route, x1, mod, lw, seq_len, alpha)
    return x2, (k, v, states)


def kernel(x_prompt, x_sample, cache_k, cache_v, state_gla, c, c_ctx, ada_w, ada_b, w_in, q_norm, k_norm,
           gate_w2, gate_b, gla_norm, w_attn_o, w_gla_o, w_out, ln1_g, ln1_b, ln2_g, ln2_b,
           peer_wq, peer_sub_keys, peer_u, peer_v):
    depth = ada_w.shape[0]
    alpha = (2.0 * depth) ** 0.25
    batch, seq, _ = x_prompt.shape
    dec_batch, dec_seq, _ = x_sample.shape
    past = cache_k.shape[2]

    n_cond = 1 + dec_batch
    cond = jnp.concatenate([c_ctx[None], c, jnp.zeros((-n_cond % 8, D_MODEL), F32)], axis=0)
    mod = _ada_mod(cond, ada_w, ada_b).reshape(depth, cond.shape[0], 6, D_MODEL)

    idx = np.arange(GLA_CHUNK)
    tri = jnp.asarray(np.stack([idx[None, :] <= idx[:, None], idx[None, :] >= idx[:, None]]), BF16)
    consts = dict(tri=tri, rope=_rope_tables(dec_seq))
    weights = [_layer_weights(l, w_in, q_norm, k_norm, gate_w2, gate_b, gla_norm, w_attn_o, w_gla_o, w_out,
                              ln1_g, ln1_b, ln2_g, ln2_b, peer_wq, peer_sub_keys, peer_u, peer_v)
               for l in range(depth)]

    xp = x_prompt.reshape(batch * seq, D_MODEL)
    ks, vs, ss = [], [], []
    for l in range(depth):
        xp, (k_l, v_l, s_l) = _trunk_layer(xp, mod[l, 0:1], weights[l], batch, seq, alpha, consts, None, l)
        ks.append(k_l.reshape(batch, seq, KV_HEADS, HEAD_DIM))
        vs.append(v_l.reshape(batch, seq, KV_HEADS, HEAD_DIM))
        ss.append(s_l)
    new_cache_k = jnp.stack(ks, axis=1)
    new_cache_v = jnp.stack(vs, axis=1)
    new_state = jnp.stack(ss, axis=1)

    ctx = (cache_k.reshape(dec_batch, depth, past, KV_WIDTH), cache_v.reshape(dec_batch, depth, past, KV_WIDTH),
           state_gla)
    xs = x_sample.reshape(dec_batch * dec_seq, D_MODEL)
    for l in range(depth):
        xs, _ = _trunk_layer(xs, mod[l, 1:1 + dec_batch], weights[l], dec_batch, dec_seq, alpha, consts, ctx, l)

    return (xp.reshape(batch, seq, D_MODEL), xs.reshape(dec_batch, dec_seq, D_MODEL),
            new_cache_k, new_cache_v, new_state)
```

```python
import functools

import numpy as np
import jax
import jax.numpy as jnp
from jax import lax
from jax.experimental import pallas as pl
from jax.experimental.pallas import tpu as pltpu

F32 = jnp.float32
BF16 = jnp.bfloat16

D_MODEL = 1024
HEAD_DIM = 64
N_HEADS = D_MODEL // HEAD_DIM
KV_HEADS = N_HEADS // 4
Q_PER_KV = N_HEADS // KV_HEADS
KV_WIDTH = KV_HEADS * HEAD_DIM
GRID_W = 64
ROPE_FREQS = HEAD_DIM // 4
ROPE_THETA = 10000.0
GLA_HEADS = 4
GLA_DK = D_MODEL // 2 // GLA_HEADS
GLA_DV = D_MODEL // GLA_HEADS
GLA_QK_WIDTH = GLA_HEADS * GLA_DK
GLA_V_WIDTH = GLA_HEADS * GLA_DV
GATE_RANK = 16
GATE_NORM = 16.0
GLA_CHUNK = 64
N_KEYS = 128
N_EXPERTS = N_KEYS * N_KEYS
PEER_HEADS = 8
PEER_TOPK = 16
PEER_QDIM = 256
PEER_HALF = PEER_QDIM // 2
LN_EPS = 1e-5
RMS_EPS = 1e-6

V7X_VMEM_BYTES = 64 * 1024 * 1024
VMEM_LIMIT = V7X_VMEM_BYTES - 8 * 1024 * 1024
LANES = 128

TOKEN_TILE = 256
PEER_TOKEN_TILE = 512
ROUTE_TOKEN_TILE = 512
PEER_EXPERT_TILE = 2048
PEER_CHUNK_KEYS = 4


def _dot(a, b):
    return jnp.dot(a, b, preferred_element_type=F32)


def _dot_nt(a, b):
    return lax.dot_general(a, b, (((1,), (1,)), ((), ())), preferred_element_type=F32)


def _dot_tn(a, b):
    return lax.dot_general(a, b, (((0,), (0,)), ((), ())), preferred_element_type=F32)


def _const_spec(shape):
    zeros = (0,) * len(shape)
    return pl.BlockSpec(shape, lambda *_: zeros)


def _params(*sem):
    return pltpu.CompilerParams(dimension_semantics=sem, vmem_limit_bytes=VMEM_LIMIT)


def _layer_norm(x, g, b):
    mu = jnp.mean(x, axis=-1, keepdims=True)
    xc = x - mu
    var = jnp.mean(xc * xc, axis=-1, keepdims=True)
    return xc * lax.rsqrt(var + LN_EPS) * g + b


def _ada_kernel(c_ref, w_ref, b_ref, o_ref):
    c = c_ref[...]
    s = c * jax.nn.sigmoid(c)
    o_ref[...] = _dot(s.astype(BF16), w_ref[...].astype(BF16)) + b_ref[...]


def _ada_mod(cond, ada_w, ada_b):
    depth = ada_w.shape[0]
    rows = cond.shape[0]
    return pl.pallas_call(
        _ada_kernel,
        out_shape=jax.ShapeDtypeStruct((depth, rows, 6 * D_MODEL), F32),
        grid=(depth, 6),
        in_specs=[
            pl.BlockSpec((rows, D_MODEL), lambda l, j: (0, 0)),
            pl.BlockSpec((None, D_MODEL, D_MODEL), lambda l, j: (l, 0, j)),
            pl.BlockSpec((None, 1, D_MODEL), lambda l, j: (l, 0, j)),
        ],
        out_specs=pl.BlockSpec((None, rows, D_MODEL), lambda l, j: (l, 0, j)),
        compiler_params=_params("parallel", "parallel"),
        name="ada_mod",
    )(cond, ada_w, ada_b.reshape(depth, 1, 6 * D_MODEL))


def _rope(t, cos, sin_signed):
    width = t.shape[-1]
    up = pltpu.roll(t, width - ROPE_FREQS, 1)
    dn = pltpu.roll(t, ROPE_FREQS, 1)
    lane = lax.broadcasted_iota(jnp.int32, t.shape, 1)
    partner = jnp.where((lane & ROPE_FREQS) == 0, up, dn)
    return t * cos + partner * sin_signed


def _inproj_kernel(rope, *refs):
    (x_ref, mod_ref, wq_ref, wkv_ref, wg_ref, wglr_ref, wgm_ref, mq_ref, qg_ref, kg_ref,
     w2_ref, gb_ref) = refs[:12]
    refs = refs[12:]
    if rope:
        cos_ref, sin_ref = refs[:2]
        refs = refs[2:]
    q_out, k_out, v_out, gq_out, gk_out, gv_out, go_out, la_out, gm_out = refs

    m = mod_ref[0]
    h = (x_ref[...] * (1.0 + m[1:2]) + m[0:1]).astype(BF16)

    q = _dot(h, wq_ref[...])
    qn = q * lax.rsqrt(_dot((q * q).astype(BF16), mq_ref[...]) + RMS_EPS) * qg_ref[...]
    kv = _dot(h, wkv_ref[...])
    k = kv[:, :KV_WIDTH]
    kn = k * lax.rsqrt(_dot((k * k).astype(BF16), mq_ref[:KV_WIDTH, :KV_WIDTH]) + RMS_EPS) * kg_ref[...]
    if rope:
        cos = cos_ref[...]
        sin = sin_ref[...]
        qn = _rope(qn, cos, sin)
        kn = _rope(kn, cos[:, :KV_WIDTH], sin[:, :KV_WIDTH])
    q_out[...] = qn * (HEAD_DIM ** -0.5)
    k_out[...] = kn
    v_out[...] = kv[:, KV_WIDTH:]

    g = _dot(h, wg_ref[...])
    gq_out[...] = g[:, :GLA_QK_WIDTH] * (GLA_DK ** -0.5)
    gk_out[...] = g[:, GLA_QK_WIDTH:2 * GLA_QK_WIDTH]
    gv_out[...] = g[:, 2 * GLA_QK_WIDTH:2 * GLA_QK_WIDTH + GLA_V_WIDTH]
    go_out[...] = g[:, 2 * GLA_QK_WIDTH + GLA_V_WIDTH:]

    glr = _dot(h, wglr_ref[...])
    z = _dot(glr.astype(BF16), w2_ref[...]) + gb_ref[...]
    la_out[...] = (jnp.minimum(z, 0.0) - jnp.log1p(jnp.exp(-jnp.abs(z)))) * (1.0 / GATE_NORM)
    gm_out[...] = jax.nn.sigmoid(_dot(h, wgm_ref[...]))


def _inproj(x, mod, lw, seq_len, rope_tabs):
    tokens = x.shape[0]
    tm = TOKEN_TILE
    rope = rope_tabs is not None
    row = lambda i: (i, 0)
    ins = [x, mod, lw["wq"], lw["wkv"], lw["wg"], lw["wglr"], lw["wgm"], lw["mq"], lw["qg"], lw["kg"],
           lw["w2"], lw["gb"]]
    in_specs = [pl.BlockSpec((tm, D_MODEL), row),
                pl.BlockSpec((1, 6, D_MODEL), lambda i: ((i * tm) // seq_len % mod.shape[0], 0, 0))]
    in_specs += [_const_spec(a.shape) for a in ins[2:]]
    if rope:
        per_seq = seq_len // tm
        ins += list(rope_tabs)
        in_specs += [pl.BlockSpec((tm, D_MODEL), lambda i: (i % per_seq, 0))] * 2
    widths = (D_MODEL, KV_WIDTH, KV_WIDTH, GLA_QK_WIDTH, GLA_QK_WIDTH, GLA_V_WIDTH, GLA_V_WIDTH,
              2 * GLA_QK_WIDTH, 2 * D_MODEL)
    return pl.pallas_call(
        functools.partial(_inproj_kernel, rope),
        out_shape=[jax.ShapeDtypeStruct((tokens, w), F32) for w in widths],
        grid=(tokens // tm,),
        in_specs=in_specs,
        out_specs=[pl.BlockSpec((tm, w), row) for w in widths],
        compiler_params=_params("parallel"),
        name="inproj_rope" if rope else "inproj",
    )(*ins)


def _attn_kernel(has_ctx, *refs):
    if has_ctx:
        q_ref, k_ref, v_ref, ck_ref, cv_ref, o_ref = refs
    else:
        q_ref, k_ref, v_ref, o_ref = refs
    tq = q_ref.shape[0]
    for g in range(KV_HEADS):
        gs = slice(g * HEAD_DIM, (g + 1) * HEAD_DIM)
        kg = k_ref[:, gs].astype(BF16)
        vg = v_ref[:, gs].astype(BF16)
        if has_ctx:
            kg = jnp.concatenate([kg, ck_ref[:, gs].astype(BF16)], axis=0)
            vg = jnp.concatenate([vg, cv_ref[:, gs].astype(BF16)], axis=0)
        heads = [q_ref[:, (Q_PER_KV * g + r) * HEAD_DIM:(Q_PER_KV * g + r + 1) * HEAD_DIM]
                 for r in range(Q_PER_KV)]
        qs = jnp.concatenate(heads, axis=0).astype(BF16)
        s = _dot_nt(qs, kg)
        p = jnp.exp(s - jnp.max(s, axis=-1, keepdims=True))
        o = _dot(p.astype(BF16), vg) / jnp.sum(p, axis=-1, keepdims=True)
        for r in range(Q_PER_KV):
            h0 = (Q_PER_KV * g + r) * HEAD_DIM
            o_ref[:, h0:h0 + HEAD_DIM] = o[r * tq:(r + 1) * tq]


def _attention(q, k, v, batch, seq_len, tq, ctx_kv, layer):
    tokens = q.shape[0]
    nq = seq_len // tq
    ins = [q, k, v]
    in_specs = [pl.BlockSpec((tq, D_MODEL), lambda b, i: (b * nq + i, 0)),
                pl.BlockSpec((seq_len, KV_WIDTH), lambda b, i: (b, 0)),
                pl.BlockSpec((seq_len, KV_WIDTH), lambda b, i: (b, 0))]
    if ctx_kv is not None:
        past = ctx_kv[0].shape[2]
        ins += list(ctx_kv)
        in_specs += [pl.BlockSpec((None, None, past, KV_WIDTH), lambda b, i: (b, layer, 0, 0))] * 2
    return pl.pallas_call(
        functools.partial(_attn_kernel, ctx_kv is not None),
        out_shape=jax.ShapeDtypeStruct((tokens, D_MODEL), F32),
        grid=(batch, nq),
        in_specs=in_specs,
        out_specs=pl.BlockSpec((tq, D_MODEL), lambda b, i: (b * nq + i, 0)),
        compiler_params=_params("parallel", "parallel"),
        name="attention_ctx" if ctx_kv is not None else "attention",
    )(*ins)


def _split3(x):
    hi = x.astype(BF16)
    r = x - hi.astype(F32)
    mid = r.astype(BF16)
    lo = (r - mid.astype(F32)).astype(BF16)
    return hi, mid, lo


def _gla_kernel(has_s0, *refs):
    gq_ref, gk_ref, gv_ref, la_ref, tri_ref = refs[:5]
    refs = refs[5:]
    if has_s0:
        s0_ref = refs[0]
        refs = refs[1:]
    o_ref, st_out, bf_ref, bb_ref, st_ref = refs
    seq_len = gq_ref.shape[0]
    n_chunks = seq_len // GLA_CHUNK
    tril = tri_ref[0]
    triu = tri_ref[1]

    def cumsum_chunk(c, carry):
        rows = pl.ds(pl.multiple_of(c * GLA_CHUNK, GLA_CHUNK), GLA_CHUNK)
        la = la_ref[rows, :]
        pf = _split3(la[:, :GLA_QK_WIDTH])
        pb = _split3(la[:, GLA_QK_WIDTH:])
        bf_ref[rows, :] = _dot(tril, pf[0]) + _dot(tril, pf[1]) + _dot(tril, pf[2])
        bb_ref[rows, :] = _dot(triu, pb[0]) + _dot(triu, pb[1]) + _dot(triu, pb[2])
        return carry

    lax.fori_loop(0, n_chunks, cumsum_chunk, 0)

    ri = lax.broadcasted_iota(jnp.int32, (GLA_CHUNK, GLA_CHUNK), 0)
    ci = lax.broadcasted_iota(jnp.int32, (GLA_CHUNK, GLA_CHUNK), 1)
    for d in range(2):
        for h in range(GLA_HEADS):
            if has_s0:
                st_ref[d * GLA_HEADS + h] = s0_ref[d, h].T
            else:
                st_ref[d * GLA_HEADS + h] = jnp.zeros((GLA_DV, GLA_DK), F32)
    o_ref[...] = jnp.zeros_like(o_ref)

    def step(i, carry):
        for d in range(2):
            b_ref = bf_ref if d == 0 else bb_ref
            keep = (ci <= ri) if d == 0 else (ci >= ri)
            c = i if d == 0 else n_chunks - 1 - i
            rows = pl.ds(pl.multiple_of(c * GLA_CHUNK, GLA_CHUNK), GLA_CHUNK)
            for h in range(GLA_HEADS):
                ks = slice(h * GLA_DK, (h + 1) * GLA_DK)
                vs = slice(h * GLA_DV, (h + 1) * GLA_DV)
                b = b_ref[rows, ks]
                bl = b[GLA_CHUNK - 1:GLA_CHUNK] if d == 0 else b[0:1]
                kk = gk_ref[rows, ks]
                v = gv_ref[rows, vs].astype(BF16)
                qe = (gq_ref[rows, ks] * jnp.exp(b)).astype(BF16)
                ke = (kk * jnp.exp(-b)).astype(BF16)
                kl = (kk * jnp.exp(bl - b)).astype(BF16)
                a = jnp.where(keep, _dot_nt(qe, ke), 0.0).astype(BF16)
                st = st_ref[d * GLA_HEADS + h]
                o_ref[rows, vs] += _dot(a, v) + _dot_nt(qe, st.astype(BF16))
                st_ref[d * GLA_HEADS + h] = st * jnp.exp(bl) + _dot_tn(v, kl)
        return carry

    lax.fori_loop(0, n_chunks, step, 0)
    for d in range(2):
        for h in range(GLA_HEADS):
            st_out[d, h] = st_ref[d * GLA_HEADS + h].T


def _gla(gq, gk, gv, la, batch, seq_len, tri, s0, layer):
    tokens = gq.shape[0]
    seq = lambda w: pl.BlockSpec((seq_len, w), lambda b: (b, 0))
    ins = [gq, gk, gv, la, tri]
    in_specs = [seq(GLA_QK_WIDTH), seq(GLA_QK_WIDTH), seq(GLA_V_WIDTH), seq(2 * GLA_QK_WIDTH),
                _const_spec(tri.shape)]
    if s0 is not None:
        ins.append(s0)
        in_specs.append(pl.BlockSpec((None, None, 2, GLA_HEADS, GLA_DK, GLA_DV),
                                     lambda b: (b, layer, 0, 0, 0, 0)))
    return pl.pallas_call(
        functools.partial(_gla_kernel, s0 is not None),
        out_shape=[jax.ShapeDtypeStruct((tokens, GLA_V_WIDTH), F32),
                   jax.ShapeDtypeStruct((batch, 2, GLA_HEADS, GLA_DK, GLA_DV), F32)],
        grid=(batch,),
        in_specs=in_specs,
        out_specs=[seq(GLA_V_WIDTH),
                   pl.BlockSpec((None, 2, GLA_HEADS, GLA_DK, GLA_DV), lambda b: (b, 0, 0, 0, 0))],
        scratch_shapes=[pltpu.VMEM((seq_len, GLA_QK_WIDTH), F32),
                        pltpu.VMEM((seq_len, GLA_QK_WIDTH), F32),
                        pltpu.VMEM((2 * GLA_HEADS, GLA_DV, GLA_DK), F32)],
        compiler_params=_params("parallel"),
        name="gla_s0" if s0 is not None else "gla",
    )(*ins)


def _postmix_kernel(alpha, x_ref, mod_ref, at_ref, og_ref, go_ref, gm_ref, wa_ref, wl_ref, wo_ref,
                    gn_ref, lg_ref, lb_ref, x1_ref, ht_ref):
    m = mod_ref[0]
    og = og_ref[...]
    parts = []
    for h in range(GLA_HEADS):
        oh = og[:, h * GLA_DV:(h + 1) * GLA_DV]
        parts.append(oh * lax.rsqrt(jnp.mean(oh * oh, axis=-1, keepdims=True) + RMS_EPS))
    go = go_ref[...]
    o = jnp.concatenate(parts, axis=-1) * gn_ref[...] * (go * jax.nn.sigmoid(go))
    gm = gm_ref[...]
    y = (gm[:, :D_MODEL] * _dot(at_ref[...].astype(BF16), wa_ref[...])
         + gm[:, D_MODEL:] * _dot(o.astype(BF16), wl_ref[...]))
    mix = _dot(y.astype(BF16), wo_ref[...])
    x1 = _layer_norm(alpha * x_ref[...] + m[2:3] * mix, lg_ref[...], lb_ref[...])
    x1_ref[...] = x1
    ht_ref[...] = (x1 * (1.0 + m[4:5]) + m[3:4]).T.astype(BF16)


def _postmix(x, mod, attn, og, go, gm, lw, seq_len, alpha):
    tokens = x.shape[0]
    tm = TOKEN_TILE
    row = lambda w: pl.BlockSpec((tm, w), lambda i: (i, 0))
    consts = [lw["wa"], lw["wl"], lw["wo"], lw["gn"], lw["ln1g"], lw["ln1b"]]
    return pl.pallas_call(
        functools.partial(_postmix_kernel, alpha),
        out_shape=[jax.ShapeDtypeStruct((tokens, D_MODEL), F32),
                   jax.ShapeDtypeStruct((D_MODEL, tokens), BF16)],
        grid=(tokens // tm,),
        in_specs=[row(D_MODEL),
                  pl.BlockSpec((1, 6, D_MODEL), lambda i: ((i * tm) // seq_len % mod.shape[0], 0, 0)),
                  row(D_MODEL), row(D_MODEL), row(D_MODEL), row(2 * D_MODEL)]
                 + [_const_spec(a.shape) for a in consts],
        out_specs=[row(D_MODEL), pl.BlockSpec((D_MODEL, tm), lambda i: (0, i))],
        compiler_params=_params("parallel"),
        name="postmix",
    )(x, mod, attn, og, go, gm, *consts)


def _peer_candidate_tables(lanes):
    groups = [[(0, r) for r in range(16)], [(r, 0) for r in range(16)]]
    for t in (1, 2, 3):
        groups.append([(t, r) for r in range(8)])
        if t < 3:
            groups.append([(r, t) for r in range(8)])
    seen = set()
    ci, neg = [], []
    for grp in groups:
        for (r1, r2) in grp:
            ok = (r1 + 1) * (r2 + 1) <= PEER_TOPK and (r1, r2) not in seen
            if ok:
                seen.add((r1, r2))
            ci.append(float(r1 * PEER_TOPK + r2) if ok else 1e9)
            neg.append(0.0 if ok else -np.inf)
    tab = np.stack([np.asarray(ci, np.float32), np.asarray(neg, np.float32)])
    return np.ascontiguousarray(np.broadcast_to(tab[:, :, None], tab.shape + (lanes,)))


def _extract_top(s, exact_ties):
    key = lax.broadcasted_iota(jnp.int32, s.shape, 0).astype(F32)
    slot = lax.broadcasted_iota(jnp.int32, (PEER_TOPK, s.shape[1]), 0)
    rank = jnp.full(s.shape, float(PEER_TOPK), F32)
    vals = jnp.zeros((PEER_TOPK, s.shape[1]), F32)
    for r in range(PEER_TOPK):
        m = jnp.max(s, axis=0, keepdims=True)
        hit = s == m
        if exact_ties:
            hit = key == jnp.min(jnp.where(hit, key, float(N_KEYS)), axis=0, keepdims=True)
        rank = jnp.where(hit, float(r), rank)
        s = jnp.where(hit, -jnp.inf, s)
        vals = jnp.where(slot == r, m, vals)
    ranked = jnp.sum(jnp.where(rank < float(PEER_TOPK), 1.0, 0.0), axis=0, keepdims=True)
    return vals, rank, ranked


def _candidate_counts(v1, v2, ci, neg, exact_ties):
    lo = slice(0, 8)
    cand = jnp.concatenate([
        v1[0:1] + v2, v1 + v2[0:1],
        v1[1:2] + v2[lo], v1[lo] + v2[1:2],
        v1[2:3] + v2[lo], v1[lo] + v2[2:3],
        v1[3:4] + v2[lo]], axis=0) + neg
    taken = jnp.zeros(cand.shape, F32)
    for _ in range(PEER_TOPK):
        m = jnp.max(cand, axis=0, keepdims=True)
        hit = cand == m
        if exact_ties:
            hit = ci == jnp.min(jnp.where(hit, ci, 2e9), axis=0, keepdims=True)
        taken = jnp.where(hit, 1.0, taken)
        cand = jnp.where(hit, -jnp.inf, cand)
    row_sum = lambda a, b: jnp.sum(taken[a:b], axis=0, keepdims=True)
    slot = lax.broadcasted_iota(jnp.int32, v1.shape, 0)
    counts = taken[16:32] + jnp.concatenate(
        [taken[40:48] + taken[56:64], jnp.zeros((8, v1.shape[1]), F32)], axis=0)
    counts += jnp.where(slot == 0, row_sum(0, 16), 0.0)
    counts += jnp.where(slot == 1, row_sum(32, 40), 0.0)
    counts += jnp.where(slot == 2, row_sum(48, 56), 0.0)
    counts += jnp.where(slot == 3, row_sum(64, 72), 0.0)
    return counts, jnp.sum(counts, axis=0, keepdims=True)


def _route_kernel(ht_ref, wq_ref, keys_ref, tab_ref, cnt_out, p1_out, rk_out, p2_out,
                  q_ref, v1_ref, v2_ref, rank1_ref, rank2_ref, counts_ref):
    q_ref[...] = _dot(wq_ref[...], ht_ref[...])
    ci = tab_ref[0]
    neg = tab_ref[1]

    def head(h, carry):
        r0 = pl.multiple_of(h * PEER_QDIM, PEER_QDIM)
        s1 = _dot(keys_ref[0], q_ref[pl.ds(r0, PEER_HALF), :].astype(BF16))
        s2 = _dot(keys_ref[1], q_ref[pl.ds(r0 + PEER_HALF, PEER_HALF), :].astype(BF16))

        def select(exact_ties):
            v1, rank1, n1 = _extract_top(s1, exact_ties)
            v2, rank2, n2 = _extract_top(s2, exact_ties)
            counts, n3 = _candidate_counts(v1, v2, ci, neg, exact_ties)
            v1_ref[...], v2_ref[...], counts_ref[...] = v1, v2, counts
            rank1_ref[...], rank2_ref[...] = rank1, rank2
            full = float(PEER_TOPK)
            return jnp.where((n1 == full) & (n2 == full) & (n3 == full), 0.0, 1.0)

        tied = jnp.max(select(False))

        @pl.when(tied > 0.0)
        def _():
            select(True)

        v1, v2, counts = v1_ref[...], v2_ref[...], counts_ref[...]
        rank1 = rank1_ref[...]
        cnt = jnp.zeros(s1.shape, F32)
        for r in range(PEER_TOPK):
            cnt = jnp.where(rank1 == float(r), counts[r:r + 1], cnt)
        e1 = jnp.exp(v1 - v1[0:1])
        e2 = jnp.exp(v2 - v2[0:1])
        inner = jnp.zeros(v1.shape, F32)
        for r in range(PEER_TOPK):
            inner += jnp.where(counts > float(r), e2[r:r + 1], 0.0)
        z = jnp.sum(e1 * inner, axis=0, keepdims=True)
        rows = pl.ds(pl.multiple_of(h * N_KEYS, N_KEYS), N_KEYS)
        cnt_out[rows, :] = cnt
        p1_out[rows, :] = jnp.exp(s1 - v1[0:1])
        half = pl.ds(pl.multiple_of(h * (N_KEYS // 2), N_KEYS // 2), N_KEYS // 2)
        rk_out[half, :] = pltpu.bitcast(rank2_ref[...].astype(BF16), jnp.uint32)
        p2_out[half, :] = pltpu.bitcast((jnp.exp(s2 - v2[0:1]) / z).astype(BF16), jnp.uint32)
        return carry

    lax.fori_loop(0, PEER_HEADS, head, 0)


def _peer_route(ht, lw):
    tokens = ht.shape[1]
    rt = min(ROUTE_TOKEN_TILE, tokens)
    tab = jnp.asarray(_peer_candidate_tables(rt))
    n_rows = PEER_HEADS * N_KEYS
    dense = lambda dt, rows: jax.ShapeDtypeStruct((rows, tokens), dt)
    out_spec = lambda rows: pl.BlockSpec((rows, rt), lambda i: (0, i))
    return pl.pallas_call(
        _route_kernel,
        out_shape=[dense(F32, n_rows), dense(F32, n_rows),
                   dense(jnp.uint32, n_rows // 2), dense(jnp.uint32, n_rows // 2)],
        grid=(tokens // rt,),
        in_specs=[pl.BlockSpec((D_MODEL, rt), lambda i: (0, i)),
                  _const_spec(lw["pwq"].shape), _const_spec(lw["pkeys"].shape), _const_spec(tab.shape)],
        out_specs=[out_spec(n_rows), out_spec(n_rows), out_spec(n_rows // 2), out_spec(n_rows // 2)],
        scratch_shapes=[pltpu.VMEM((PEER_HEADS * PEER_QDIM, rt), F32),
                        pltpu.VMEM((PEER_TOPK, rt), F32), pltpu.VMEM((PEER_TOPK, rt), F32),
                        pltpu.VMEM((N_KEYS, rt), F32), pltpu.VMEM((N_KEYS, rt), F32),
                        pltpu.VMEM((PEER_TOPK, rt), F32)],
        compiler_params=_params("parallel"),
        name="peer_route",
    )(ht, lw["pwq"], lw["pkeys"], tab)


def _gelu_tanh(x):
    k1 = -2.0 * 0.7978845608028654 * 1.4426950408889634
    e = jnp.exp2(x * (k1 + (k1 * 0.044715) * (x * x)))
    return x / (1.0 + e)


def _peer_kernel(alpha, ht_ref, u_ref, vt_ref, cnt_ref, p1_ref, rk_ref, p2_ref, x1_ref, mod_ref,
                 lg_ref, lb_ref, o_ref, acc_ref, a_ref, g_ref):
    j = pl.program_id(1)
    tb = ht_ref.shape[1]
    first_keys = u_ref.shape[0] // N_KEYS

    @pl.when(j == 0)
    def _():
        acc_ref[...] = jnp.zeros_like(acc_ref)

    head_rows = lambda h: pl.ds(pl.multiple_of(h * N_KEYS + j * first_keys, first_keys), first_keys)
    ht = ht_ref[...]
    n_chunks = first_keys // PEER_CHUNK_KEYS
    chunk_rows = lambda c: slice(c * PEER_CHUNK_KEYS * N_KEYS, (c + 1) * PEER_CHUNK_KEYS * N_KEYS)

    def activations(c):
        a_ref[chunk_rows(c), :] = _dot(u_ref[chunk_rows(c), :], ht)

    activations(0)
    for c in range(n_chunks):
        rows = chunk_rows(c)
        if c + 1 < n_chunks:
            activations(c + 1)
        for a in range(c * PEER_CHUNK_KEYS, (c + 1) * PEER_CHUNK_KEYS):
            for lg in range(tb // LANES):
                ls = slice(lg * LANES, (lg + 1) * LANES)
                row_a = lambda ref, h: jnp.broadcast_to(ref[head_rows(h), ls][a:a + 1], (16, LANES)).astype(BF16)
                groups = range(N_KEYS // 16)
                w = [jnp.zeros((16, LANES), BF16) for _ in groups]
                for h in range(PEER_HEADS):
                    cb = row_a(cnt_ref, h)
                    pb = row_a(p1_ref, h)
                    for g in groups:
                        ks = slice(h * (N_KEYS // 2) + g * 8, h * (N_KEYS // 2) + (g + 1) * 8)
                        rk = pltpu.bitcast(rk_ref[ks, ls], BF16)
                        p2 = pltpu.bitcast(p2_ref[ks, ls], BF16)
                        w[g] += jnp.where(rk < cb, p2, 0.0) * pb
                for g in groups:
                    er = slice(a * N_KEYS + g * 16, a * N_KEYS + (g + 1) * 16)
                    g_ref[er, ls] = w[g] * _gelu_tanh(a_ref[er, ls]).astype(BF16)
        acc_ref[...] += _dot_tn(vt_ref[rows, :], g_ref[rows, :])

    @pl.when(j == pl.num_programs(1) - 1)
    def _():
        m = mod_ref[0]
        o_ref[...] = _layer_norm(alpha * x1_ref[...] + m[5:6] * acc_ref[...].T, lg_ref[...], lb_ref[...])


def _peer_dense(ht, route, x1, mod, lw, seq_len, alpha):
    tokens = x1.shape[0]
    tb = min(PEER_TOKEN_TILE, tokens)
    et = PEER_EXPERT_TILE
    assert (et // N_KEYS) % 8 == 0, "whole sublane tiles of first-key rows per expert tile"
    dense = pl.BlockSpec((PEER_HEADS * N_KEYS, tb), lambda i, j: (0, i))
    packed = pl.BlockSpec((PEER_HEADS * N_KEYS // 2, tb), lambda i, j: (0, i))
    return pl.pallas_call(
        functools.partial(_peer_kernel, alpha),
        out_shape=jax.ShapeDtypeStruct((tokens, D_MODEL), F32),
        grid=(tokens // tb, N_EXPERTS // et),
        in_specs=[pl.BlockSpec((D_MODEL, tb), lambda i, j: (0, i)),
                  pl.BlockSpec((et, D_MODEL), lambda i, j: (j, 0)),
                  pl.BlockSpec((et, D_MODEL), lambda i, j: (j, 0)),
                  dense, dense, packed, packed,
                  pl.BlockSpec((tb, D_MODEL), lambda i, j: (i, 0)),
                  pl.BlockSpec((1, 6, D_MODEL), lambda i, j: ((i * tb) // seq_len % mod.shape[0], 0, 0)),
                  _const_spec(lw["ln2g"].shape), _const_spec(lw["ln2b"].shape)],
        out_specs=pl.BlockSpec((tb, D_MODEL), lambda i, j: (i, 0)),
        scratch_shapes=[pltpu.VMEM((D_MODEL, tb), F32),
                        pltpu.VMEM((et, tb), F32),
                        pltpu.VMEM((et, tb), BF16)],
        compiler_params=_params("parallel", "arbitrary"),
        name="peer_dense",
    )(ht, lw["pu"], lw["pvt"], *route, x1, mod, lw["ln2g"], lw["ln2b"])


def _rope_tables(seq_len):
    rows = seq_len // GRID_W
    r = jnp.repeat(jnp.arange(rows, dtype=F32), GRID_W)
    col = jnp.tile(jnp.arange(GRID_W, dtype=F32), rows)
    inv = ROPE_THETA ** (-jnp.arange(ROPE_FREQS, dtype=F32) / ROPE_FREQS)
    ang = jnp.stack([r[:, None] * inv, col[:, None] * inv], axis=1)
    cos, sin = jnp.cos(ang), jnp.sin(ang)
    cos_h = jnp.concatenate([cos, cos], axis=-1).reshape(seq_len, HEAD_DIM)
    sin_h = jnp.concatenate([-sin, sin], axis=-1).reshape(seq_len, HEAD_DIM)
    return jnp.tile(cos_h, (1, N_HEADS)), jnp.tile(sin_h, (1, N_HEADS))


def _layer_weights(l, w_in, q_norm, k_norm, gate_w2, gate_b, gla_norm, w_attn_o, w_gla_o, w_out,
                   ln1_g, ln1_b, ln2_g, ln2_b, peer_wq, peer_sub_keys, peer_u, peer_v):
    w = w_in[l].astype(BF16)
    o_q, o_k, o_g, o_lr, o_gm = 0, D_MODEL, D_MODEL + 2 * KV_WIDTH, 0, 0
    o_lr = o_g + 2 * GLA_QK_WIDTH + 2 * GLA_V_WIDTH
    o_gm = o_lr + 2 * GATE_RANK
    head_id = np.arange(D_MODEL) // HEAD_DIM
    mq = jnp.asarray((head_id[:, None] == head_id[None, :]).astype(np.float32) / HEAD_DIM, BF16)
    w2 = jnp.zeros((LANES, 2 * GLA_QK_WIDTH), F32)
    w2 = w2.at[:GATE_RANK, :GLA_QK_WIDTH].set(gate_w2[l, 0])
    w2 = w2.at[GATE_RANK:2 * GATE_RANK, GLA_QK_WIDTH:].set(gate_w2[l, 1])
    row = lambda a: a.reshape(1, -1)
    return dict(
        wq=w[:, o_q:o_k], wkv=w[:, o_k:o_g], wg=w[:, o_g:o_lr],
        wglr=jnp.pad(w[:, o_lr:o_gm], ((0, 0), (0, LANES - 2 * GATE_RANK))),
        wgm=w[:, o_gm:], mq=mq,
        qg=row(jnp.tile(q_norm[l], N_HEADS)), kg=row(jnp.tile(k_norm[l], KV_HEADS)),
        w2=w2.astype(BF16), gb=row(gate_b[l]),
        wa=w_attn_o[l].astype(BF16), wl=w_gla_o[l].astype(BF16), wo=w_out[l].astype(BF16),
        gn=row(jnp.tile(gla_norm[l], GLA_HEADS)),
        ln1g=row(ln1_g[l]), ln1b=row(ln1_b[l]), ln2g=row(ln2_g[l]), ln2b=row(ln2_b[l]),
        pwq=peer_wq[l].T.astype(BF16), pkeys=peer_sub_keys[l].astype(BF16),
        pu=peer_u[l].astype(BF16), pvt=peer_v[l].astype(BF16),
    )


def _trunk_layer(x, mod, lw, batch, seq_len, alpha, consts, ctx, layer):
    rope_tabs = None if ctx is None else consts["rope"]
    q, k, v, gq, gk, gv, go, la, gm = _inproj(x, mod, lw, seq_len, rope_tabs)
    if ctx is None:
        attn = _attention(q, k, v, batch, seq_len, min(seq_len, 256), None, layer)
        og, states = _gla(gq, gk, gv, la, batch, seq_len, consts["tri"], None, layer)
    else:
        attn = _attention(q, k, v, batch, seq_len, 128, ctx[:2], layer)
        og, states = _gla(gq, gk, gv, la, batch, seq_len, consts["tri"], ctx[2], layer)
    x1, ht = _postmix(x, mod, attn, og, go, gm, lw, seq_len, alpha)
    route = _peer_route(ht, lw)
    x2 = _peer_dense(ht, route, x1, mod, lw, seq_len, alpha)
    return x2, (k, v, states)


def kernel(x_prompt, x_sample, cache_k, cache_v, state_gla, c, c_ctx, ada_w, ada_b, w_in, q_norm, k_norm,
           gate_w2, gate_b, gla_norm, w_attn_o, w_gla_o, w_out, ln1_g, ln1_b, ln2_g, ln2_b,
           peer_wq, peer_sub_keys, peer_u, peer_v):
    depth = ada_w.shape[0]
    alpha = (2.0 * depth) ** 0.25
    batch, seq, _ = x_prompt.shape
    dec_batch, dec_seq, _ = x_sample.shape
    past = cache_k.shape[2]

    n_cond = 1 + dec_batch
    cond = jnp.concatenate([c_ctx[None], c, jnp.zeros((-n_cond % 8, D_MODEL), F32)], axis=0)
    mod = _ada_mod(cond, ada_w, ada_b).reshape(depth, cond.shape[0], 6, D_MODEL)

    idx = np.arange(GLA_CHUNK)
    tri = jnp.asarray(np.stack([idx[None, :] <= idx[:, None], idx[None, :] >= idx[:, None]]), BF16)
    consts = dict(tri=tri, rope=_rope_tables(dec_seq))
    weights = [_layer_weights(l, w_in, q_norm, k_norm, gate_w2, gate_b, gla_norm, w_attn_o, w_gla_o, w_out,
                              ln1_g, ln1_b, ln2_g, ln2_b, peer_wq, peer_sub_keys, peer_u, peer_v)
               for l in range(depth)]

    xp = x_prompt.reshape(batch * seq, D_MODEL)
    ks, vs, ss = [], [], []
    for l in range(depth):
        xp, (k_l, v_l, s_l) = _trunk_layer(xp, mod[l, 0:1], weights[l], batch, seq, alpha, consts, None, l)
        ks.append(k_l.reshape(batch, seq, KV_HEADS, HEAD_DIM))
        vs.append(v_l.reshape(batch, seq, KV_HEADS, HEAD_DIM))
        ss.append(s_l)
    new_cache_k = jnp.stack(ks, axis=1)
    new_cache_v = jnp.stack(vs, axis=1)
    new_state = jnp.stack(ss, axis=1)

    ctx = (cache_k.reshape(dec_batch, depth, past, KV_WIDTH), cache_v.reshape(dec_batch, depth, past, KV_WIDTH),
           state_gla)
    xs = x_sample.reshape(dec_batch * dec_seq, D_MODEL)
    for l in range(depth):
        xs, _ = _trunk_layer(xs, mod[l, 1:1 + dec_batch], weights[l], dec_batch, dec_seq, alpha, consts, ctx, l)

    return (xp.reshape(batch, seq, D_MODEL), xs.reshape(dec_batch, dec_seq, D_MODEL),
            new_cache_k, new_cache_v, new_state)
```

```python
import functools

import numpy as np
import jax
import jax.numpy as jnp
from jax import lax
from jax.experimental import pallas as pl
from jax.experimental.pallas import tpu as pltpu

F32 = jnp.float32
BF16 = jnp.bfloat16

D_MODEL = 1024
HEAD_DIM = 64
N_HEADS = D_MODEL // HEAD_DIM
KV_HEADS = N_HEADS // 4
Q_PER_KV = N_HEADS // KV_HEADS
KV_WIDTH = KV_HEADS * HEAD_DIM
GRID_W = 64
ROPE_FREQS = HEAD_DIM // 4
ROPE_THETA = 10000.0
GLA_HEADS = 4
GLA_DK = D_MODEL // 2 // GLA_HEADS
GLA_DV = D_MODEL // GLA_HEADS
GLA_QK_WIDTH = GLA_HEADS * GLA_DK
GLA_V_WIDTH = GLA_HEADS * GLA_DV
GATE_RANK = 16
GATE_NORM = 16.0
GLA_CHUNK = 64
N_KEYS = 128
N_EXPERTS = N_KEYS * N_KEYS
PEER_HEADS = 8
PEER_TOPK = 16
PEER_QDIM = 256
PEER_HALF = PEER_QDIM // 2
LN_EPS = 1e-5
RMS_EPS = 1e-6

V7X_VMEM_BYTES = 64 * 1024 * 1024
VMEM_LIMIT = V7X_VMEM_BYTES - 8 * 1024 * 1024
LANES = 128

TOKEN_TILE = 256
PEER_TOKEN_TILE = 512
ROUTE_TOKEN_TILE = 512
PEER_EXPERT_TILE = 1024
PEER_CHUNK_KEYS = 4


def _dot(a, b):
    return jnp.dot(a, b, preferred_element_type=F32)


def _dot_nt(a, b):
    return lax.dot_general(a, b, (((1,), (1,)), ((), ())), preferred_element_type=F32)


def _dot_tn(a, b):
    return lax.dot_general(a, b, (((0,), (0,)), ((), ())), preferred_element_type=F32)


def _const_spec(shape):
    zeros = (0,) * len(shape)
    return pl.BlockSpec(shape, lambda *_: zeros)


def _params(*sem):
    return pltpu.CompilerParams(dimension_semantics=sem, vmem_limit_bytes=VMEM_LIMIT)


def _layer_norm(x, g, b):
    mu = jnp.mean(x, axis=-1, keepdims=True)
    xc = x - mu
    var = jnp.mean(xc * xc, axis=-1, keepdims=True)
    return xc * lax.rsqrt(var + LN_EPS) * g + b


def _ada_kernel(c_ref, w_ref, b_ref, o_ref):
    c = c_ref[...]
    s = c * jax.nn.sigmoid(c)
    o_ref[...] = _dot(s.astype(BF16), w_ref[...].astype(BF16)) + b_ref[...]


def _ada_mod(cond, ada_w, ada_b):
    depth = ada_w.shape[0]
    rows = cond.shape[0]
    return pl.pallas_call(
        _ada_kernel,
        out_shape=jax.ShapeDtypeStruct((depth, rows, 6 * D_MODEL), F32),
        grid=(depth, 6),
        in_specs=[
            pl.BlockSpec((rows, D_MODEL), lambda l, j: (0, 0)),
            pl.BlockSpec((None, D_MODEL, D_MODEL), lambda l, j: (l, 0, j)),
            pl.BlockSpec((None, 1, D_MODEL), lambda l, j: (l, 0, j)),
        ],
        out_specs=pl.BlockSpec((None, rows, D_MODEL), lambda l, j: (l, 0, j)),
        compiler_params=_params("parallel", "parallel"),
        name="ada_mod",
    )(cond, ada_w, ada_b.reshape(depth, 1, 6 * D_MODEL))


def _rope(t, cos, sin_signed):
    width = t.shape[-1]
    up = pltpu.roll(t, width - ROPE_FREQS, 1)
    dn = pltpu.roll(t, ROPE_FREQS, 1)
    lane = lax.broadcasted_iota(jnp.int32, t.shape, 1)
    partner = jnp.where((lane & ROPE_FREQS) == 0, up, dn)
    return t * cos + partner * sin_signed


def _inproj_kernel(rope, *refs):
    (x_ref, mod_ref, wq_ref, wkv_ref, wg_ref, wglr_ref, wgm_ref, mq_ref, qg_ref, kg_ref,
     w2_ref, gb_ref) = refs[:12]
    refs = refs[12:]
    if rope:
        cos_ref, sin_ref = refs[:2]
        refs = refs[2:]
    q_out, k_out, v_out, gq_out, gk_out, gv_out, go_out, la_out, gm_out = refs

    m = mod_ref[0]
    h = (x_ref[...] * (1.0 + m[1:2]) + m[0:1]).astype(BF16)

    q = _dot(h, wq_ref[...])
    qn = q * lax.rsqrt(_dot((q * q).astype(BF16), mq_ref[...]) + RMS_EPS) * qg_ref[...]
    kv = _dot(h, wkv_ref[...])
    k = kv[:, :KV_WIDTH]
    kn = k * lax.rsqrt(_dot((k * k).astype(BF16), mq_ref[:KV_WIDTH, :KV_WIDTH]) + RMS_EPS) * kg_ref[...]
    if rope:
        cos = cos_ref[...]
        sin = sin_ref[...]
        qn = _rope(qn, cos, sin)
        kn = _rope(kn, cos[:, :KV_WIDTH], sin[:, :KV_WIDTH])
    q_out[...] = qn * (HEAD_DIM ** -0.5)
    k_out[...] = kn
    v_out[...] = kv[:, KV_WIDTH:]

    g = _dot(h, wg_ref[...])
    gq_out[...] = g[:, :GLA_QK_WIDTH] * (GLA_DK ** -0.5)
    gk_out[...] = g[:, GLA_QK_WIDTH:2 * GLA_QK_WIDTH]
    gv_out[...] = g[:, 2 * GLA_QK_WIDTH:2 * GLA_QK_WIDTH + GLA_V_WIDTH]
    go_out[...] = g[:, 2 * GLA_QK_WIDTH + GLA_V_WIDTH:]

    glr = _dot(h, wglr_ref[...])
    z = _dot(glr.astype(BF16), w2_ref[...]) + gb_ref[...]
    la_out[...] = (jnp.minimum(z, 0.0) - jnp.log1p(jnp.exp(-jnp.abs(z)))) * (1.0 / GATE_NORM)
    gm_out[...] = jax.nn.sigmoid(_dot(h, wgm_ref[...]))


def _inproj(x, mod, lw, seq_len, rope_tabs):
    tokens = x.shape[0]
    tm = TOKEN_TILE
    rope = rope_tabs is not None
    row = lambda i: (i, 0)
    ins = [x, mod, lw["wq"], lw["wkv"], lw["wg"], lw["wglr"], lw["wgm"], lw["mq"], lw["qg"], lw["kg"],
           lw["w2"], lw["gb"]]
    in_specs = [pl.BlockSpec((tm, D_MODEL), row),
                pl.BlockSpec((1, 6, D_MODEL), lambda i: ((i * tm) // seq_len % mod.shape[0], 0, 0))]
    in_specs += [_const_spec(a.shape) for a in ins[2:]]
    if rope:
        per_seq = seq_len // tm
        ins += list(rope_tabs)
        in_specs += [pl.BlockSpec((tm, D_MODEL), lambda i: (i % per_seq, 0))] * 2
    widths = (D_MODEL, KV_WIDTH, KV_WIDTH, GLA_QK_WIDTH, GLA_QK_WIDTH, GLA_V_WIDTH, GLA_V_WIDTH,
              2 * GLA_QK_WIDTH, 2 * D_MODEL)
    return pl.pallas_call(
        functools.partial(_inproj_kernel, rope),
        out_shape=[jax.ShapeDtypeStruct((tokens, w), F32) for w in widths],
        grid=(tokens // tm,),
        in_specs=in_specs,
        out_specs=[pl.BlockSpec((tm, w), row) for w in widths],
        compiler_params=_params("parallel"),
        name="inproj_rope" if rope else "inproj",
    )(*ins)


def _attn_kernel(has_ctx, *refs):
    if has_ctx:
        q_ref, k_ref, v_ref, ck_ref, cv_ref, o_ref = refs
    else:
        q_ref, k_ref, v_ref, o_ref = refs
    tq = q_ref.shape[0]
    for g in range(KV_HEADS):
        gs = slice(g * HEAD_DIM, (g + 1) * HEAD_DIM)
        kg = k_ref[:, gs].astype(BF16)
        vg = v_ref[:, gs].astype(BF16)
        if has_ctx:
            kg = jnp.concatenate([kg, ck_ref[:, gs].astype(BF16)], axis=0)
            vg = jnp.concatenate([vg, cv_ref[:, gs].astype(BF16)], axis=0)
        heads = [q_ref[:, (Q_PER_KV * g + r) * HEAD_DIM:(Q_PER_KV * g + r + 1) * HEAD_DIM]
                 for r in range(Q_PER_KV)]
        qs = jnp.concatenate(heads, axis=0).astype(BF16)
        s = _dot_nt(qs, kg)
        p = jnp.exp(s - jnp.max(s, axis=-1, keepdims=True))
        o = _dot(p.astype(BF16), vg) / jnp.sum(p, axis=-1, keepdims=True)
        for r in range(Q_PER_KV):
            h0 = (Q_PER_KV * g + r) * HEAD_DIM
            o_ref[:, h0:h0 + HEAD_DIM] = o[r * tq:(r + 1) * tq]


def _attention(q, k, v, batch, seq_len, tq, ctx_kv, layer):
    tokens = q.shape[0]
    nq = seq_len // tq
    ins = [q, k, v]
    in_specs = [pl.BlockSpec((tq, D_MODEL), lambda b, i: (b * nq + i, 0)),
                pl.BlockSpec((seq_len, KV_WIDTH), lambda b, i: (b, 0)),
                pl.BlockSpec((seq_len, KV_WIDTH), lambda b, i: (b, 0))]
    if ctx_kv is not None:
        past = ctx_kv[0].shape[2]
        ins += list(ctx_kv)
        in_specs += [pl.BlockSpec((None, None, past, KV_WIDTH), lambda b, i: (b, layer, 0, 0))] * 2
    return pl.pallas_call(
        functools.partial(_attn_kernel, ctx_kv is not None),
        out_shape=jax.ShapeDtypeStruct((tokens, D_MODEL), F32),
        grid=(batch, nq),
        in_specs=in_specs,
        out_specs=pl.BlockSpec((tq, D_MODEL), lambda b, i: (b * nq + i, 0)),
        compiler_params=_params("parallel", "parallel"),
        name="attention_ctx" if ctx_kv is not None else "attention",
    )(*ins)


def _split3(x):
    hi = x.astype(BF16)
    r = x - hi.astype(F32)
    mid = r.astype(BF16)
    lo = (r - mid.astype(F32)).astype(BF16)
    return hi, mid, lo


def _gla_kernel(has_s0, *refs):
    gq_ref, gk_ref, gv_ref, la_ref, tri_ref = refs[:5]
    refs = refs[5:]
    if has_s0:
        s0_ref = refs[0]
        refs = refs[1:]
    o_ref, st_out, bf_ref, bb_ref, st_ref = refs
    seq_len = gq_ref.shape[0]
    n_chunks = seq_len // GLA_CHUNK
    tril = tri_ref[0]
    triu = tri_ref[1]

    def cumsum_chunk(c, carry):
        rows = pl.ds(pl.multiple_of(c * GLA_CHUNK, GLA_CHUNK), GLA_CHUNK)
        la = la_ref[rows, :]
        pf = _split3(la[:, :GLA_QK_WIDTH])
        pb = _split3(la[:, GLA_QK_WIDTH:])
        bf_ref[rows, :] = _dot(tril, pf[0]) + _dot(tril, pf[1]) + _dot(tril, pf[2])
        bb_ref[rows, :] = _dot(triu, pb[0]) + _dot(triu, pb[1]) + _dot(triu, pb[2])
        return carry

    lax.fori_loop(0, n_chunks, cumsum_chunk, 0)

    ri = lax.broadcasted_iota(jnp.int32, (GLA_CHUNK, GLA_CHUNK), 0)
    ci = lax.broadcasted_iota(jnp.int32, (GLA_CHUNK, GLA_CHUNK), 1)
    for d in range(2):
        for h in range(GLA_HEADS):
            if has_s0:
                st_ref[d * GLA_HEADS + h] = s0_ref[d, h].T
            else:
                st_ref[d * GLA_HEADS + h] = jnp.zeros((GLA_DV, GLA_DK), F32)
    o_ref[...] = jnp.zeros_like(o_ref)

    def step(i, carry):
        for d in range(2):
            b_ref = bf_ref if d == 0 else bb_ref
            keep = (ci <= ri) if d == 0 else (ci >= ri)
            c = i if d == 0 else n_chunks - 1 - i
            rows = pl.ds(pl.multiple_of(c * GLA_CHUNK, GLA_CHUNK), GLA_CHUNK)
            for h in range(GLA_HEADS):
                ks = slice(h * GLA_DK, (h + 1) * GLA_DK)
                vs = slice(h * GLA_DV, (h + 1) * GLA_DV)
                b = b_ref[rows, ks]
                bl = b[GLA_CHUNK - 1:GLA_CHUNK] if d == 0 else b[0:1]
                kk = gk_ref[rows, ks]
                v = gv_ref[rows, vs].astype(BF16)
                qe = (gq_ref[rows, ks] * jnp.exp(b)).astype(BF16)
                ke = (kk * jnp.exp(-b)).astype(BF16)
                kl = (kk * jnp.exp(bl - b)).astype(BF16)
                a = jnp.where(keep, _dot_nt(qe, ke), 0.0).astype(BF16)
                st = st_ref[d * GLA_HEADS + h]
                o_ref[rows, vs] += _dot(a, v) + _dot_nt(qe, st.astype(BF16))
                st_ref[d * GLA_HEADS + h] = st * jnp.exp(bl) + _dot_tn(v, kl)
        return carry

    lax.fori_loop(0, n_chunks, step, 0)
    for d in range(2):
        for h in range(GLA_HEADS):
            st_out[d, h] = st_ref[d * GLA_HEADS + h].T


def _gla(gq, gk, gv, la, batch, seq_len, tri, s0, layer):
    tokens = gq.shape[0]
    seq = lambda w: pl.BlockSpec((seq_len, w), lambda b: (b, 0))
    ins = [gq, gk, gv, la, tri]
    in_specs = [seq(GLA_QK_WIDTH), seq(GLA_QK_WIDTH), seq(GLA_V_WIDTH), seq(2 * GLA_QK_WIDTH),
                _const_spec(tri.shape)]
    if s0 is not None:
        ins.append(s0)
        in_specs.append(pl.BlockSpec((None, None, 2, GLA_HEADS, GLA_DK, GLA_DV),
                                     lambda b: (b, layer, 0, 0, 0, 0)))
    return pl.pallas_call(
        functools.partial(_gla_kernel, s0 is not None),
        out_shape=[jax.ShapeDtypeStruct((tokens, GLA_V_WIDTH), F32),
                   jax.ShapeDtypeStruct((batch, 2, GLA_HEADS, GLA_DK, GLA_DV), F32)],
        grid=(batch,),
        in_specs=in_specs,
        out_specs=[seq(GLA_V_WIDTH),
                   pl.BlockSpec((None, 2, GLA_HEADS, GLA_DK, GLA_DV), lambda b: (b, 0, 0, 0, 0))],
        scratch_shapes=[pltpu.VMEM((seq_len, GLA_QK_WIDTH), F32),
                        pltpu.VMEM((seq_len, GLA_QK_WIDTH), F32),
                        pltpu.VMEM((2 * GLA_HEADS, GLA_DV, GLA_DK), F32)],
        compiler_params=_params("parallel"),
        name="gla_s0" if s0 is not None else "gla",
    )(*ins)


def _postmix_kernel(alpha, x_ref, mod_ref, at_ref, og_ref, go_ref, gm_ref, wa_ref, wl_ref, wo_ref,
                    gn_ref, lg_ref, lb_ref, x1_ref, ht_ref):
    m = mod_ref[0]
    og = og_ref[...]
    parts = []
    for h in range(GLA_HEADS):
        oh = og[:, h * GLA_DV:(h + 1) * GLA_DV]
        parts.append(oh * lax.rsqrt(jnp.mean(oh * oh, axis=-1, keepdims=True) + RMS_EPS))
    go = go_ref[...]
    o = jnp.concatenate(parts, axis=-1) * gn_ref[...] * (go * jax.nn.sigmoid(go))
    gm = gm_ref[...]
    y = (gm[:, :D_MODEL] * _dot(at_ref[...].astype(BF16), wa_ref[...])
         + gm[:, D_MODEL:] * _dot(o.astype(BF16), wl_ref[...]))
    mix = _dot(y.astype(BF16), wo_ref[...])
    x1 = _layer_norm(alpha * x_ref[...] + m[2:3] * mix, lg_ref[...], lb_ref[...])
    x1_ref[...] = x1
    ht_ref[...] = (x1 * (1.0 + m[4:5]) + m[3:4]).T.astype(BF16)


def _postmix(x, mod, attn, og, go, gm, lw, seq_len, alpha):
    tokens = x.shape[0]
    tm = TOKEN_TILE
    row = lambda w: pl.BlockSpec((tm, w), lambda i: (i, 0))
    consts = [lw["wa"], lw["wl"], lw["wo"], lw["gn"], lw["ln1g"], lw["ln1b"]]
    return pl.pallas_call(
        functools.partial(_postmix_kernel, alpha),
        out_shape=[jax.ShapeDtypeStruct((tokens, D_MODEL), F32),
                   jax.ShapeDtypeStruct((D_MODEL, tokens), BF16)],
        grid=(tokens // tm,),
        in_specs=[row(D_MODEL),
                  pl.BlockSpec((1, 6, D_MODEL), lambda i: ((i * tm) // seq_len % mod.shape[0], 0, 0)),
                  row(D_MODEL), row(D_MODEL), row(D_MODEL), row(2 * D_MODEL)]
                 + [_const_spec(a.shape) for a in consts],
        out_specs=[row(D_MODEL), pl.BlockSpec((D_MODEL, tm), lambda i: (0, i))],
        compiler_params=_params("parallel"),
        name="postmix",
    )(x, mod, attn, og, go, gm, *consts)


def _peer_candidate_tables(lanes):
    groups = [[(0, r) for r in range(16)], [(r, 0) for r in range(16)]]
    for t in (1, 2, 3):
        groups.append([(t, r) for r in range(8)])
        if t < 3:
            groups.append([(r, t) for r in range(8)])
    seen = set()
    ci, neg = [], []
    for grp in groups:
        for (r1, r2) in grp:
            ok = (r1 + 1) * (r2 + 1) <= PEER_TOPK and (r1, r2) not in seen
            if ok:
                seen.add((r1, r2))
            ci.append(float(r1 * PEER_TOPK + r2) if ok else 1e9)
            neg.append(0.0 if ok else -np.inf)
    tab = np.stack([np.asarray(ci, np.float32), np.asarray(neg, np.float32)])
    return np.ascontiguousarray(np.broadcast_to(tab[:, :, None], tab.shape + (lanes,)))


def _extract_top(s, exact_ties):
    key = lax.broadcasted_iota(jnp.int32, s.shape, 0).astype(F32)
    slot = lax.broadcasted_iota(jnp.int32, (PEER_TOPK, s.shape[1]), 0)
    rank = jnp.full(s.shape, float(PEER_TOPK), F32)
    vals = jnp.zeros((PEER_TOPK, s.shape[1]), F32)
    for r in range(PEER_TOPK):
        m = jnp.max(s, axis=0, keepdims=True)
        hit = s == m
        if exact_ties:
            hit = key == jnp.min(jnp.where(hit, key, float(N_KEYS)), axis=0, keepdims=True)
        rank = jnp.where(hit, float(r), rank)
        s = jnp.where(hit, -jnp.inf, s)
        vals = jnp.where(slot == r, m, vals)
    ranked = jnp.sum(jnp.where(rank < float(PEER_TOPK), 1.0, 0.0), axis=0, keepdims=True)
    return vals, rank, ranked


def _candidate_counts(v1, v2, ci, neg, exact_ties):
    lo = slice(0, 8)
    cand = jnp.concatenate([
        v1[0:1] + v2, v1 + v2[0:1],
        v1[1:2] + v2[lo], v1[lo] + v2[1:2],
        v1[2:3] + v2[lo], v1[lo] + v2[2:3],
        v1[3:4] + v2[lo]], axis=0) + neg
    taken = jnp.zeros(cand.shape, F32)
    for _ in range(PEER_TOPK):
        m = jnp.max(cand, axis=0, keepdims=True)
        hit = cand == m
        if exact_ties:
            hit = ci == jnp.min(jnp.where(hit, ci, 2e9), axis=0, keepdims=True)
        taken = jnp.where(hit, 1.0, taken)
        cand = jnp.where(hit, -jnp.inf, cand)
    row_sum = lambda a, b: jnp.sum(taken[a:b], axis=0, keepdims=True)
    slot = lax.broadcasted_iota(jnp.int32, v1.shape, 0)
    counts = taken[16:32] + jnp.concatenate(
        [taken[40:48] + taken[56:64], jnp.zeros((8, v1.shape[1]), F32)], axis=0)
    counts += jnp.where(slot == 0, row_sum(0, 16), 0.0)
    counts += jnp.where(slot == 1, row_sum(32, 40), 0.0)
    counts += jnp.where(slot == 2, row_sum(48, 56), 0.0)
    counts += jnp.where(slot == 3, row_sum(64, 72), 0.0)
    return counts, jnp.sum(counts, axis=0, keepdims=True)


def _route_kernel(ht_ref, wq_ref, keys_ref, tab_ref, cnt_out, p1_out, rk_out, p2_out,
                  q_ref, v1_ref, v2_ref, rank1_ref, rank2_ref, counts_ref):
    q_ref[...] = _dot(wq_ref[...], ht_ref[...])
    ci = tab_ref[0]
    neg = tab_ref[1]

    def head(h, carry):
        r0 = pl.multiple_of(h * PEER_QDIM, PEER_QDIM)
        s1 = _dot(keys_ref[0], q_ref[pl.ds(r0, PEER_HALF), :].astype(BF16))
        s2 = _dot(keys_ref[1], q_ref[pl.ds(r0 + PEER_HALF, PEER_HALF), :].astype(BF16))

        def select(exact_ties):
            v1, rank1, n1 = _extract_top(s1, exact_ties)
            v2, rank2, n2 = _extract_top(s2, exact_ties)
            counts, n3 = _candidate_counts(v1, v2, ci, neg, exact_ties)
            v1_ref[...], v2_ref[...], counts_ref[...] = v1, v2, counts
            rank1_ref[...], rank2_ref[...] = rank1, rank2
            full = float(PEER_TOPK)
            return jnp.where((n1 == full) & (n2 == full) & (n3 == full), 0.0, 1.0)

        tied = jnp.max(select(False))

        @pl.when(tied > 0.0)
        def _():
            select(True)

        v1, v2, counts = v1_ref[...], v2_ref[...], counts_ref[...]
        rank1 = rank1_ref[...]
        used = jnp.sum(jnp.where(counts > 0.0, 1.0, 0.0), axis=0, keepdims=True)
        cnt = jnp.where(rank1 < used, 1.0, 0.0)
        for r in range(PEER_TOPK // 2):
            cnt = jnp.where(rank1 == float(r), counts[r:r + 1], cnt)
        e1 = jnp.exp(v1 - v1[0:1])
        e2 = jnp.exp(v2 - v2[0:1])
        inner = jnp.zeros(v1.shape, F32)
        for r in range(PEER_TOPK):
            inner += jnp.where(counts > float(r), e2[r:r + 1], 0.0)
        z = jnp.sum(e1 * inner, axis=0, keepdims=True)
        rows = pl.ds(pl.multiple_of(h * N_KEYS, N_KEYS), N_KEYS)
        cnt_out[rows, :] = cnt
        p1_out[rows, :] = jnp.exp(s1 - v1[0:1])
        half = pl.ds(pl.multiple_of(h * (N_KEYS // 2), N_KEYS // 2), N_KEYS // 2)
        rk_out[half, :] = pltpu.bitcast(rank2_ref[...].astype(BF16), jnp.uint32)
        p2_out[half, :] = pltpu.bitcast((jnp.exp(s2 - v2[0:1]) / z).astype(BF16), jnp.uint32)
        return carry

    lax.fori_loop(0, PEER_HEADS, head, 0)


def _peer_route(ht, lw):
    tokens = ht.shape[1]
    rt = min(ROUTE_TOKEN_TILE, tokens)
    tab = jnp.asarray(_peer_candidate_tables(rt))
    n_rows = PEER_HEADS * N_KEYS
    dense = lambda dt, rows: jax.ShapeDtypeStruct((rows, tokens), dt)
    out_spec = lambda rows: pl.BlockSpec((rows, rt), lambda i: (0, i))
    return pl.pallas_call(
        _route_kernel,
        out_shape=[dense(F32, n_rows), dense(F32, n_rows),
                   dense(jnp.uint32, n_rows // 2), dense(jnp.uint32, n_rows // 2)],
        grid=(tokens // rt,),
        in_specs=[pl.BlockSpec((D_MODEL, rt), lambda i: (0, i)),
                  _const_spec(lw["pwq"].shape), _const_spec(lw["pkeys"].shape), _const_spec(tab.shape)],
        out_specs=[out_spec(n_rows), out_spec(n_rows), out_spec(n_rows // 2), out_spec(n_rows // 2)],
        scratch_shapes=[pltpu.VMEM((PEER_HEADS * PEER_QDIM, rt), F32),
                        pltpu.VMEM((PEER_TOPK, rt), F32), pltpu.VMEM((PEER_TOPK, rt), F32),
                        pltpu.VMEM((N_KEYS, rt), F32), pltpu.VMEM((N_KEYS, rt), F32),
                        pltpu.VMEM((PEER_TOPK, rt), F32)],
        compiler_params=_params("parallel"),
        name="peer_route",
    )(ht, lw["pwq"], lw["pkeys"], tab)


def _gelu_tanh(x):
    k1 = -2.0 * 0.7978845608028654 * 1.4426950408889634
    e = jnp.exp2(x * (k1 + (k1 * 0.044715) * (x * x)))
    return x / (1.0 + e)


def _peer_kernel(alpha, ht_ref, u_ref, vt_ref, cnt_ref, p1_ref, rk_ref, p2_ref, x1_ref, mod_ref,
                 lg_ref, lb_ref, o_ref, acc_ref, a_ref, g_ref):
    j = pl.program_id(1)
    tb = ht_ref.shape[1]
    first_keys = u_ref.shape[0] // N_KEYS

    @pl.when(j == 0)
    def _():
        acc_ref[...] = jnp.zeros_like(acc_ref)

    head_rows = lambda h: pl.ds(pl.multiple_of(h * N_KEYS + j * first_keys, first_keys), first_keys)
    ht = ht_ref[...]
    n_chunks = first_keys // PEER_CHUNK_KEYS
    chunk_rows = lambda c: slice(c * PEER_CHUNK_KEYS * N_KEYS, (c + 1) * PEER_CHUNK_KEYS * N_KEYS)

    def activations(c):
        a_ref[chunk_rows(c), :] = _dot(u_ref[chunk_rows(c), :], ht)

    activations(0)
    for c in range(n_chunks):
        rows = chunk_rows(c)
        if c + 1 < n_chunks:
            activations(c + 1)
        for a in range(c * PEER_CHUNK_KEYS, (c + 1) * PEER_CHUNK_KEYS):
            for lg in range(tb // LANES):
                ls = slice(lg * LANES, (lg + 1) * LANES)
                row_a = lambda ref, h: jnp.broadcast_to(ref[head_rows(h), ls][a:a + 1], (16, LANES)).astype(BF16)
                groups = range(N_KEYS // 16)
                w = [jnp.zeros((16, LANES), BF16) for _ in groups]
                for h in range(PEER_HEADS):
                    cb = row_a(cnt_ref, h)
                    pb = row_a(p1_ref, h)
                    for g in groups:
                        ks = slice(h * (N_KEYS // 2) + g * 8, h * (N_KEYS // 2) + (g + 1) * 8)
                        rk = pltpu.bitcast(rk_ref[ks, ls], BF16)
                        p2 = pltpu.bitcast(p2_ref[ks, ls], BF16)
                        w[g] += jnp.where(rk < cb, p2, 0.0) * pb
                for g in groups:
                    er = slice(a * N_KEYS + g * 16, a * N_KEYS + (g + 1) * 16)
                    g_ref[er, ls] = w[g] * _gelu_tanh(a_ref[er, ls].astype(BF16))
        acc_ref[...] += _dot_tn(vt_ref[rows, :], g_ref[rows, :])

    @pl.when(j == pl.num_programs(1) - 1)
    def _():
        m = mod_ref[0]
        o_ref[...] = _layer_norm(alpha * x1_ref[...] + m[5:6] * acc_ref[...].T, lg_ref[...], lb_ref[...])


def _peer_dense(ht, route, x1, mod, lw, seq_len, alpha):
    tokens = x1.shape[0]
    tb = min(PEER_TOKEN_TILE, tokens)
    et = PEER_EXPERT_TILE
    assert (et // N_KEYS) % 8 == 0, "whole sublane tiles of first-key rows per expert tile"
    dense = pl.BlockSpec((PEER_HEADS * N_KEYS, tb), lambda i, j: (0, i))
    packed = pl.BlockSpec((PEER_HEADS * N_KEYS // 2, tb), lambda i, j: (0, i))
    return pl.pallas_call(
        functools.partial(_peer_kernel, alpha),
        out_shape=jax.ShapeDtypeStruct((tokens, D_MODEL), F32),
        grid=(tokens // tb, N_EXPERTS // et),
        in_specs=[pl.BlockSpec((D_MODEL, tb), lambda i, j: (0, i)),
                  pl.BlockSpec((et, D_MODEL), lambda i, j: (j, 0)),
                  pl.BlockSpec((et, D_MODEL), lambda i, j: (j, 0)),
                  dense, dense, packed, packed,
                  pl.BlockSpec((tb, D_MODEL), lambda i, j: (i, 0)),
                  pl.BlockSpec((1, 6, D_MODEL), lambda i, j: ((i * tb) // seq_len % mod.shape[0], 0, 0)),
                  _const_spec(lw["ln2g"].shape), _const_spec(lw["ln2b"].shape)],
        out_specs=pl.BlockSpec((tb, D_MODEL), lambda i, j: (i, 0)),
        scratch_shapes=[pltpu.VMEM((D_MODEL, tb), F32),
                        pltpu.VMEM((et, tb), F32),
                        pltpu.VMEM((et, tb), BF16)],
        compiler_params=_params("parallel", "arbitrary"),
        name="peer_dense",
    )(ht, lw["pu"], lw["pvt"], *route, x1, mod, lw["ln2g"], lw["ln2b"])


def _rope_tables(seq_len):
    rows = seq_len // GRID_W
    r = jnp.repeat(jnp.arange(rows, dtype=F32), GRID_W)
    col = jnp.tile(jnp.arange(GRID_W, dtype=F32), rows)
    inv = ROPE_THETA ** (-jnp.arange(ROPE_FREQS, dtype=F32) / ROPE_FREQS)
    ang = jnp.stack([r[:, None] * inv, col[:, None] * inv], axis=1)
    cos, sin = jnp.cos(ang), jnp.sin(ang)
    cos_h = jnp.concatenate([cos, cos], axis=-1).reshape(seq_len, HEAD_DIM)
    sin_h = jnp.concatenate([-sin, sin], axis=-1).reshape(seq_len, HEAD_DIM)
    return jnp.tile(cos_h, (1, N_HEADS)), jnp.tile(sin_h, (1, N_HEADS))


def _layer_weights(l, w_in, q_norm, k_norm, gate_w2, gate_b, gla_norm, w_attn_o, w_gla_o, w_out,
                   ln1_g, ln1_b, ln2_g, ln2_b, peer_wq, peer_sub_keys, peer_u, peer_v):
    w = w_in[l].astype(BF16)
    o_q, o_k, o_g, o_lr, o_gm = 0, D_MODEL, D_MODEL + 2 * KV_WIDTH, 0, 0
    o_lr = o_g + 2 * GLA_QK_WIDTH + 2 * GLA_V_WIDTH
    o_gm = o_lr + 2 * GATE_RANK
    head_id = np.arange(D_MODEL) // HEAD_DIM
    mq = jnp.asarray((head_id[:, None] == head_id[None, :]).astype(np.float32) / HEAD_DIM, BF16)
    w2 = jnp.zeros((LANES, 2 * GLA_QK_WIDTH), F32)
    w2 = w2.at[:GATE_RANK, :GLA_QK_WIDTH].set(gate_w2[l, 0])
    w2 = w2.at[GATE_RANK:2 * GATE_RANK, GLA_QK_WIDTH:].set(gate_w2[l, 1])
    row = lambda a: a.reshape(1, -1)
    return dict(
        wq=w[:, o_q:o_k], wkv=w[:, o_k:o_g], wg=w[:, o_g:o_lr],
        wglr=jnp.pad(w[:, o_lr:o_gm], ((0, 0), (0, LANES - 2 * GATE_RANK))),
        wgm=w[:, o_gm:], mq=mq,
        qg=row(jnp.tile(q_norm[l], N_HEADS)), kg=row(jnp.tile(k_norm[l], KV_HEADS)),
        w2=w2.astype(BF16), gb=row(gate_b[l]),
        wa=w_attn_o[l].astype(BF16), wl=w_gla_o[l].astype(BF16), wo=w_out[l].astype(BF16),
        gn=row(jnp.tile(gla_norm[l], GLA_HEADS)),
        ln1g=row(ln1_g[l]), ln1b=row(ln1_b[l]), ln2g=row(ln2_g[l]), ln2b=row(ln2_b[l]),
        pwq=peer_wq[l].T.astype(BF16), pkeys=peer_sub_keys[l].astype(BF16),
        pu=peer_u[l].astype(BF16), pvt=peer_v[l].astype(BF16),
    )


def _trunk_layer(x, mod, lw, batch, seq_len, alpha, consts, ctx, layer):
    rope_tabs = None if ctx is None else consts["rope"]
    q, k, v, gq, gk, gv, go, la, gm = _inproj(x, mod, lw, seq_len, rope_tabs)
    if ctx is None:
        attn = _attention(q, k, v, batch, seq_len, min(seq_len, 256), None, layer)
        og, states = _gla(gq, gk, gv, la, batch, seq_len, consts["tri"], None, layer)
    else:
        attn = _attention(q, k, v, batch, seq_len, 128, ctx[:2], layer)
        og, states = _gla(gq, gk, gv, la, batch, seq_len, consts["tri"], ctx[2], layer)
    x1, ht = _postmix(x, mod, attn, og, go, gm, lw, seq_len, alpha)
    route = _peer_route(ht, lw)
    x2 = _peer_dense(ht, route, x1, mod, lw, seq_len, alpha)
    return x2, (k, v, states)


def kernel(x_prompt, x_sample, cache_k, cache_v, state_gla, c, c_ctx, ada_w, ada_b, w_in, q_norm, k_norm,
           gate_w2, gate_b, gla_norm, w_attn_o, w_gla_o, w_out, ln1_g, ln1_b, ln2_g, ln2_b,
           peer_wq, peer_sub_keys, peer_u, peer_v):
    depth = ada_w.shape[0]
    alpha = (2.0 * depth) ** 0.25
    batch, seq, _ = x_prompt.shape
    dec_batch, dec_seq, _ = x_sample.shape
    past = cache_k.shape[2]

    n_cond = 1 + dec_batch
    cond = jnp.concatenate([c_ctx[None], c, jnp.zeros((-n_cond % 8, D_MODEL), F32)], axis=0)
    mod = _ada_mod(cond, ada_w, ada_b).reshape(depth, cond.shape[0], 6, D_MODEL)

    idx = np.arange(GLA_CHUNK)
    tri = jnp.asarray(np.stack([idx[None, :] <= idx[:, None], idx[None, :] >= idx[:, None]]), BF16)
    consts = dict(tri=tri, rope=_rope_tables(dec_seq))
    weights = [_layer_weights(l, w_in, q_norm, k_norm, gate_w2, gate_b, gla_norm, w_attn_o, w_gla_o, w_out,
                              ln1_g, ln1_b, ln2_g, ln2_b, peer_wq, peer_sub_keys, peer_u, peer_v)
               for l in range(depth)]

    xp = x_prompt.reshape(batch * seq, D_MODEL)
    ks, vs, ss = [], [], []
    for l in range(depth):
        xp, (k_l, v_l, s_l) = _trunk_layer(xp, mod[l, 0:1], weights[l], batch, seq, alpha, consts, None, l)
        ks.append(k_l.reshape(batch, seq, KV_HEADS, HEAD_DIM))
        vs.append(v_l.reshape(batch, seq, KV_HEADS, HEAD_DIM))
        ss.append(s_l)
    new_cache_k = jnp.stack(ks, axis=1)
    new_cache_v = jnp.stack(vs, axis=1)
    new_state = jnp.stack(ss, axis=1)

    ctx = (cache_k.reshape(dec_batch, depth, past, KV_WIDTH), cache_v.reshape(dec_batch, depth, past, KV_WIDTH),
           state_gla)
    xs = x_sample.reshape(dec_batch * dec_seq, D_MODEL)
    for l in range(depth):
        xs, _ = _trunk_layer(xs, mod[l, 1:1 + dec_batch], weights[l], dec_batch, dec_seq, alpha, consts, ctx, l)

    return (xp.reshape(batch, seq, D_MODEL), xs.reshape(dec_batch, dec_seq, D_MODEL),
            new_cache_k, new_cache_v, new_state)
```

```python
import functools

import numpy as np
import jax
import jax.numpy as jnp
from jax import lax
from jax.experimental import pallas as pl
from jax.experimental.pallas import tpu as pltpu

F32 = jnp.float32
BF16 = jnp.bfloat16

D_MODEL = 1024
HEAD_DIM = 64
N_HEADS = D_MODEL // HEAD_DIM
KV_HEADS = N_HEADS // 4
Q_PER_KV = N_HEADS // KV_HEADS
KV_WIDTH = KV_HEADS * HEAD_DIM
GRID_W = 64
ROPE_FREQS = HEAD_DIM // 4
ROPE_THETA = 10000.0
GLA_HEADS = 4
GLA_DK = D_MODEL // 2 // GLA_HEADS
GLA_DV = D_MODEL // GLA_HEADS
GLA_QK_WIDTH = GLA_HEADS * GLA_DK
GLA_V_WIDTH = GLA_HEADS * GLA_DV
GATE_RANK = 16
GATE_NORM = 16.0
GLA_CHUNK = 64
N_KEYS = 128
N_EXPERTS = N_KEYS * N_KEYS
PEER_HEADS = 8
PEER_TOPK = 16
PEER_QDIM = 256
PEER_HALF = PEER_QDIM // 2
LN_EPS = 1e-5
RMS_EPS = 1e-6

V7X_VMEM_BYTES = 64 * 1024 * 1024
VMEM_LIMIT = V7X_VMEM_BYTES - 8 * 1024 * 1024
LANES = 128

TOKEN_TILE = 256
PEER_TOKEN_TILE = 512
ROUTE_TOKEN_TILE = 512
PEER_EXPERT_TILE = 1024
PEER_CHUNK_KEYS = 4


def _dot(a, b):
    return jnp.dot(a, b, preferred_element_type=F32)


def _dot_nt(a, b):
    return lax.dot_general(a, b, (((1,), (1,)), ((), ())), preferred_element_type=F32)


def _dot_tn(a, b):
    return lax.dot_general(a, b, (((0,), (0,)), ((), ())), preferred_element_type=F32)


def _const_spec(shape):
    zeros = (0,) * len(shape)
    return pl.BlockSpec(shape, lambda *_: zeros)


def _params(*sem):
    return pltpu.CompilerParams(dimension_semantics=sem, vmem_limit_bytes=VMEM_LIMIT)


def _layer_norm(x, g, b):
    mu = jnp.mean(x, axis=-1, keepdims=True)
    xc = x - mu
    var = jnp.mean(xc * xc, axis=-1, keepdims=True)
    return xc * lax.rsqrt(var + LN_EPS) * g + b


def _ada_kernel(c_ref, w_ref, b_ref, o_ref):
    c = c_ref[...]
    s = c * jax.nn.sigmoid(c)
    o_ref[...] = _dot(s.astype(BF16), w_ref[...].astype(BF16)) + b_ref[...]


def _ada_mod(cond, ada_w, ada_b):
    depth = ada_w.shape[0]
    rows = cond.shape[0]
    return pl.pallas_call(
        _ada_kernel,
        out_shape=jax.ShapeDtypeStruct((depth, rows, 6 * D_MODEL), F32),
        grid=(depth, 6),
        in_specs=[
            pl.BlockSpec((rows, D_MODEL), lambda l, j: (0, 0)),
            pl.BlockSpec((None, D_MODEL, D_MODEL), lambda l, j: (l, 0, j)),
            pl.BlockSpec((None, 1, D_MODEL), lambda l, j: (l, 0, j)),
        ],
        out_specs=pl.BlockSpec((None, rows, D_MODEL), lambda l, j: (l, 0, j)),
        compiler_params=_params("parallel", "parallel"),
        name="ada_mod",
    )(cond, ada_w, ada_b.reshape(depth, 1, 6 * D_MODEL))


def _rope(t, cos, sin_signed):
    width = t.shape[-1]
    up = pltpu.roll(t, width - ROPE_FREQS, 1)
    dn = pltpu.roll(t, ROPE_FREQS, 1)
    lane = lax.broadcasted_iota(jnp.int32, t.shape, 1)
    partner = jnp.where((lane & ROPE_FREQS) == 0, up, dn)
    return t * cos + partner * sin_signed


def _inproj_kernel(rope, *refs):
    (x_ref, mod_ref, wq_ref, wkv_ref, wg_ref, wglr_ref, wgm_ref, mq_ref, qg_ref, kg_ref,
     w2_ref, gb_ref) = refs[:12]
    refs = refs[12:]
    if rope:
        cos_ref, sin_ref = refs[:2]
        refs = refs[2:]
    q_out, k_out, v_out, gq_out, gk_out, gv_out, go_out, la_out, gm_out = refs

    m = mod_ref[0]
    h = (x_ref[...] * (1.0 + m[1:2]) + m[0:1]).astype(BF16)

    q = _dot(h, wq_ref[...])
    qn = q * lax.rsqrt(_dot((q * q).astype(BF16), mq_ref[...]) + RMS_EPS) * qg_ref[...]
    kv = _dot(h, wkv_ref[...])
    k = kv[:, :KV_WIDTH]
    kn = k * lax.rsqrt(_dot((k * k).astype(BF16), mq_ref[:KV_WIDTH, :KV_WIDTH]) + RMS_EPS) * kg_ref[...]
    if rope:
        cos = cos_ref[...]
        sin = sin_ref[...]
        qn = _rope(qn, cos, sin)
        kn = _rope(kn, cos[:, :KV_WIDTH], sin[:, :KV_WIDTH])
    q_out[...] = qn * (HEAD_DIM ** -0.5)
    k_out[...] = kn
    v_out[...] = kv[:, KV_WIDTH:]

    g = _dot(h, wg_ref[...])
    gq_out[...] = g[:, :GLA_QK_WIDTH] * (GLA_DK ** -0.5)
    gk_out[...] = g[:, GLA_QK_WIDTH:2 * GLA_QK_WIDTH]
    gv_out[...] = g[:, 2 * GLA_QK_WIDTH:2 * GLA_QK_WIDTH + GLA_V_WIDTH]
    go_out[...] = g[:, 2 * GLA_QK_WIDTH + GLA_V_WIDTH:]

    glr = _dot(h, wglr_ref[...])
    z = _dot(glr.astype(BF16), w2_ref[...]) + gb_ref[...]
    la_out[...] = (jnp.minimum(z, 0.0) - jnp.log1p(jnp.exp(-jnp.abs(z)))) * (1.0 / GATE_NORM)
    gm_out[...] = jax.nn.sigmoid(_dot(h, wgm_ref[...]))


def _inproj(x, mod, lw, seq_len, rope_tabs):
    tokens = x.shape[0]
    tm = TOKEN_TILE
    rope = rope_tabs is not None
    row = lambda i: (i, 0)
    ins = [x, mod, lw["wq"], lw["wkv"], lw["wg"], lw["wglr"], lw["wgm"], lw["mq"], lw["qg"], lw["kg"],
           lw["w2"], lw["gb"]]
    in_specs = [pl.BlockSpec((tm, D_MODEL), row),
                pl.BlockSpec((1, 6, D_MODEL), lambda i: ((i * tm) // seq_len % mod.shape[0], 0, 0))]
    in_specs += [_const_spec(a.shape) for a in ins[2:]]
    if rope:
        per_seq = seq_len // tm
        ins += list(rope_tabs)
        in_specs += [pl.BlockSpec((tm, D_MODEL), lambda i: (i % per_seq, 0))] * 2
    widths = (D_MODEL, KV_WIDTH, KV_WIDTH, GLA_QK_WIDTH, GLA_QK_WIDTH, GLA_V_WIDTH, GLA_V_WIDTH,
              2 * GLA_QK_WIDTH, 2 * D_MODEL)
    return pl.pallas_call(
        functools.partial(_inproj_kernel, rope),
        out_shape=[jax.ShapeDtypeStruct((tokens, w), F32) for w in widths],
        grid=(tokens // tm,),
        in_specs=in_specs,
        out_specs=[pl.BlockSpec((tm, w), row) for w in widths],
        compiler_params=_params("parallel"),
        name="inproj_rope" if rope else "inproj",
    )(*ins)


def _attn_kernel(has_ctx, *refs):
    if has_ctx:
        q_ref, k_ref, v_ref, ck_ref, cv_ref, o_ref = refs
    else:
        q_ref, k_ref, v_ref, o_ref = refs
    tq = q_ref.shape[0]
    for g in range(KV_HEADS):
        gs = slice(g * HEAD_DIM, (g + 1) * HEAD_DIM)
        kg = k_ref[:, gs].astype(BF16)
        vg = v_ref[:, gs].astype(BF16)
        if has_ctx:
            kg = jnp.concatenate([kg, ck_ref[:, gs].astype(BF16)], axis=0)
            vg = jnp.concatenate([vg, cv_ref[:, gs].astype(BF16)], axis=0)
        heads = [q_ref[:, (Q_PER_KV * g + r) * HEAD_DIM:(Q_PER_KV * g + r + 1) * HEAD_DIM]
                 for r in range(Q_PER_KV)]
        qs = jnp.concatenate(heads, axis=0).astype(BF16)
        s = _dot_nt(qs, kg)
        p = jnp.exp(s - jnp.max(s, axis=-1, keepdims=True))
        o = _dot(p.astype(BF16), vg) / jnp.sum(p, axis=-1, keepdims=True)
        for r in range(Q_PER_KV):
            h0 = (Q_PER_KV * g + r) * HEAD_DIM
            o_ref[:, h0:h0 + HEAD_DIM] = o[r * tq:(r + 1) * tq]


def _attention(q, k, v, batch, seq_len, tq, ctx_kv, layer):
    tokens = q.shape[0]
    nq = seq_len // tq
    ins = [q, k, v]
    in_specs = [pl.BlockSpec((tq, D_MODEL), lambda b, i: (b * nq + i, 0)),
                pl.BlockSpec((seq_len, KV_WIDTH), lambda b, i: (b, 0)),
                pl.BlockSpec((seq_len, KV_WIDTH), lambda b, i: (b, 0))]
    if ctx_kv is not None:
        past = ctx_kv[0].shape[2]
        ins += list(ctx_kv)
        in_specs += [pl.BlockSpec((None, None, past, KV_WIDTH), lambda b, i: (b, layer, 0, 0))] * 2
    return pl.pallas_call(
        functools.partial(_attn_kernel, ctx_kv is not None),
        out_shape=jax.ShapeDtypeStruct((tokens, D_MODEL), F32),
        grid=(batch, nq),
        in_specs=in_specs,
        out_specs=pl.BlockSpec((tq, D_MODEL), lambda b, i: (b * nq + i, 0)),
        compiler_params=_params("parallel", "parallel"),
        name="attention_ctx" if ctx_kv is not None else "attention",
    )(*ins)


def _split3(x):
    hi = x.astype(BF16)
    r = x - hi.astype(F32)
    mid = r.astype(BF16)
    lo = (r - mid.astype(F32)).astype(BF16)
    return hi, mid, lo


def _gla_kernel(has_s0, *refs):
    gq_ref, gk_ref, gv_ref, la_ref, tri_ref = refs[:5]
    refs = refs[5:]
    if has_s0:
        s0_ref = refs[0]
        refs = refs[1:]
    o_ref, st_out, bf_ref, bb_ref, st_ref = refs
    seq_len = gq_ref.shape[0]
    n_chunks = seq_len // GLA_CHUNK
    tril = tri_ref[0]
    triu = tri_ref[1]

    def cumsum_chunk(c, carry):
        rows = pl.ds(pl.multiple_of(c * GLA_CHUNK, GLA_CHUNK), GLA_CHUNK)
        la = la_ref[rows, :]
        pf = _split3(la[:, :GLA_QK_WIDTH])
        pb = _split3(la[:, GLA_QK_WIDTH:])
        bf_ref[rows, :] = _dot(tril, pf[0]) + _dot(tril, pf[1]) + _dot(tril, pf[2])
        bb_ref[rows, :] = _dot(triu, pb[0]) + _dot(triu, pb[1]) + _dot(triu, pb[2])
        return carry

    lax.fori_loop(0, n_chunks, cumsum_chunk, 0)

    ri = lax.broadcasted_iota(jnp.int32, (GLA_CHUNK, GLA_CHUNK), 0)
    ci = lax.broadcasted_iota(jnp.int32, (GLA_CHUNK, GLA_CHUNK), 1)
    for d in range(2):
        for h in range(GLA_HEADS):
            if has_s0:
                st_ref[d * GLA_HEADS + h] = s0_ref[d, h].T
            else:
                st_ref[d * GLA_HEADS + h] = jnp.zeros((GLA_DV, GLA_DK), F32)
    o_ref[...] = jnp.zeros_like(o_ref)

    def step(i, carry):
        for d in range(2):
            b_ref = bf_ref if d == 0 else bb_ref
            keep = (ci <= ri) if d == 0 else (ci >= ri)
            c = i if d == 0 else n_chunks - 1 - i
            rows = pl.ds(pl.multiple_of(c * GLA_CHUNK, GLA_CHUNK), GLA_CHUNK)
            for h in range(GLA_HEADS):
                ks = slice(h * GLA_DK, (h + 1) * GLA_DK)
                vs = slice(h * GLA_DV, (h + 1) * GLA_DV)
                b = b_ref[rows, ks]
                bl = b[GLA_CHUNK - 1:GLA_CHUNK] if d == 0 else b[0:1]
                kk = gk_ref[rows, ks]
                v = gv_ref[rows, vs].astype(BF16)
                qe = (gq_ref[rows, ks] * jnp.exp(b)).astype(BF16)
                ke = (kk * jnp.exp(-b)).astype(BF16)
                kl = (kk * jnp.exp(bl - b)).astype(BF16)
                a = jnp.where(keep, _dot_nt(qe, ke), 0.0).astype(BF16)
                st = st_ref[d * GLA_HEADS + h]
                o_ref[rows, vs] += _dot(a, v) + _dot_nt(qe, st.astype(BF16))
                st_ref[d * GLA_HEADS + h] = st * jnp.exp(bl) + _dot_tn(v, kl)
        return carry

    lax.fori_loop(0, n_chunks, step, 0)
    for d in range(2):
        for h in range(GLA_HEADS):
            st_out[d, h] = st_ref[d * GLA_HEADS + h].T


def _gla(gq, gk, gv, la, batch, seq_len, tri, s0, layer):
    tokens = gq.shape[0]
    seq = lambda w: pl.BlockSpec((seq_len, w), lambda b: (b, 0))
    ins = [gq, gk, gv, la, tri]
    in_specs = [seq(GLA_QK_WIDTH), seq(GLA_QK_WIDTH), seq(GLA_V_WIDTH), seq(2 * GLA_QK_WIDTH),
                _const_spec(tri.shape)]
    if s0 is not None:
        ins.append(s0)
        in_specs.append(pl.BlockSpec((None, None, 2, GLA_HEADS, GLA_DK, GLA_DV),
                                     lambda b: (b, layer, 0, 0, 0, 0)))
    return pl.pallas_call(
        functools.partial(_gla_kernel, s0 is not None),
        out_shape=[jax.ShapeDtypeStruct((tokens, GLA_V_WIDTH), F32),
                   jax.ShapeDtypeStruct((batch, 2, GLA_HEADS, GLA_DK, GLA_DV), F32)],
        grid=(batch,),
        in_specs=in_specs,
        out_specs=[seq(GLA_V_WIDTH),
                   pl.BlockSpec((None, 2, GLA_HEADS, GLA_DK, GLA_DV), lambda b: (b, 0, 0, 0, 0))],
        scratch_shapes=[pltpu.VMEM((seq_len, GLA_QK_WIDTH), F32),
                        pltpu.VMEM((seq_len, GLA_QK_WIDTH), F32),
                        pltpu.VMEM((2 * GLA_HEADS, GLA_DV, GLA_DK), F32)],
        compiler_params=_params("parallel"),
        name="gla_s0" if s0 is not None else "gla",
    )(*ins)


def _postmix_kernel(alpha, x_ref, mod_ref, at_ref, og_ref, go_ref, gm_ref, wa_ref, wl_ref, wo_ref,
                    gn_ref, lg_ref, lb_ref, x1_ref, ht_ref):
    m = mod_ref[0]
    og = og_ref[...]
    parts = []
    for h in range(GLA_HEADS):
        oh = og[:, h * GLA_DV:(h + 1) * GLA_DV]
        parts.append(oh * lax.rsqrt(jnp.mean(oh * oh, axis=-1, keepdims=True) + RMS_EPS))
    go = go_ref[...]
    o = jnp.concatenate(parts, axis=-1) * gn_ref[...] * (go * jax.nn.sigmoid(go))
    gm = gm_ref[...]
    y = (gm[:, :D_MODEL] * _dot(at_ref[...].astype(BF16), wa_ref[...])
         + gm[:, D_MODEL:] * _dot(o.astype(BF16), wl_ref[...]))
    mix = _dot(y.astype(BF16), wo_ref[...])
    x1 = _layer_norm(alpha * x_ref[...] + m[2:3] * mix, lg_ref[...], lb_ref[...])
    x1_ref[...] = x1
    ht_ref[...] = (x1 * (1.0 + m[4:5]) + m[3:4]).T.astype(BF16)


def _postmix(x, mod, attn, og, go, gm, lw, seq_len, alpha):
    tokens = x.shape[0]
    tm = TOKEN_TILE
    row = lambda w: pl.BlockSpec((tm, w), lambda i: (i, 0))
    consts = [lw["wa"], lw["wl"], lw["wo"], lw["gn"], lw["ln1g"], lw["ln1b"]]
    return pl.pallas_call(
        functools.partial(_postmix_kernel, alpha),
        out_shape=[jax.ShapeDtypeStruct((tokens, D_MODEL), F32),
                   jax.ShapeDtypeStruct((D_MODEL, tokens), BF16)],
        grid=(tokens // tm,),
        in_specs=[row(D_MODEL),
                  pl.BlockSpec((1, 6, D_MODEL), lambda i: ((i * tm) // seq_len % mod.shape[0], 0, 0)),
                  row(D_MODEL), row(D_MODEL), row(D_MODEL), row(2 * D_MODEL)]
                 + [_const_spec(a.shape) for a in consts],
        out_specs=[row(D_MODEL), pl.BlockSpec((D_MODEL, tm), lambda i: (0, i))],
        compiler_params=_params("parallel"),
        name="postmix",
    )(x, mod, attn, og, go, gm, *consts)


def _peer_candidate_tables(lanes):
    groups = [[(0, r) for r in range(16)], [(r, 0) for r in range(16)]]
    for t in (1, 2, 3):
        groups.append([(t, r) for r in range(8)])
        if t < 3:
            groups.append([(r, t) for r in range(8)])
    seen = set()
    ci, neg = [], []
    for grp in groups:
        for (r1, r2) in grp:
            ok = (r1 + 1) * (r2 + 1) <= PEER_TOPK and (r1, r2) not in seen
            if ok:
                seen.add((r1, r2))
            ci.append(float(r1 * PEER_TOPK + r2) if ok else 1e9)
            neg.append(0.0 if ok else -np.inf)
    tab = np.stack([np.asarray(ci, np.float32), np.asarray(neg, np.float32)])
    return np.ascontiguousarray(np.broadcast_to(tab[:, :, None], tab.shape + (lanes,)))


def _extract_top(s, exact_ties):
    key = lax.broadcasted_iota(jnp.int32, s.shape, 0).astype(F32)
    slot = lax.broadcasted_iota(jnp.int32, (PEER_TOPK, s.shape[1]), 0)
    rank = jnp.full(s.shape, float(PEER_TOPK), F32)
    vals = jnp.zeros((PEER_TOPK, s.shape[1]), F32)
    for r in range(PEER_TOPK):
        m = jnp.max(s, axis=0, keepdims=True)
        hit = s == m
        if exact_ties:
            hit = key == jnp.min(jnp.where(hit, key, float(N_KEYS)), axis=0, keepdims=True)
        rank = jnp.where(hit, float(r), rank)
        s = jnp.where(hit, -jnp.inf, s)
        vals = jnp.where(slot == r, m, vals)
    ranked = jnp.sum(jnp.where(rank < float(PEER_TOPK), 1.0, 0.0), axis=0, keepdims=True)
    return vals, rank, ranked


def _candidate_counts(v1, v2, ci, neg, exact_ties):
    lo = slice(0, 8)
    cand = jnp.concatenate([
        v1[0:1] + v2, v1 + v2[0:1],
        v1[1:2] + v2[lo], v1[lo] + v2[1:2],
        v1[2:3] + v2[lo], v1[lo] + v2[2:3],
        v1[3:4] + v2[lo]], axis=0) + neg
    taken = jnp.zeros(cand.shape, F32)
    for _ in range(PEER_TOPK):
        m = jnp.max(cand, axis=0, keepdims=True)
        hit = cand == m
        if exact_ties:
            hit = ci == jnp.min(jnp.where(hit, ci, 2e9), axis=0, keepdims=True)
        taken = jnp.where(hit, 1.0, taken)
        cand = jnp.where(hit, -jnp.inf, cand)
    row_sum = lambda a, b: jnp.sum(taken[a:b], axis=0, keepdims=True)
    slot = lax.broadcasted_iota(jnp.int32, v1.shape, 0)
    counts = taken[16:32] + jnp.concatenate(
        [taken[40:48] + taken[56:64], jnp.zeros((8, v1.shape[1]), F32)], axis=0)
    counts += jnp.where(slot == 0, row_sum(0, 16), 0.0)
    counts += jnp.where(slot == 1, row_sum(32, 40), 0.0)
    counts += jnp.where(slot == 2, row_sum(48, 56), 0.0)
    counts += jnp.where(slot == 3, row_sum(64, 72), 0.0)
    return counts, jnp.sum(counts, axis=0, keepdims=True)


def _route_kernel(ht_ref, wq_ref, keys_ref, tab_ref, cnt_out, p1_out, rk_out, p2_out,
                  q_ref, v1_ref, v2_ref, rank1_ref, rank2_ref, counts_ref):
    q_ref[...] = _dot(wq_ref[...], ht_ref[...])
    ci = tab_ref[0]
    neg = tab_ref[1]

    def head(h, carry):
        r0 = pl.multiple_of(h * PEER_QDIM, PEER_QDIM)
        s1 = _dot(keys_ref[0], q_ref[pl.ds(r0, PEER_HALF), :].astype(BF16))
        s2 = _dot(keys_ref[1], q_ref[pl.ds(r0 + PEER_HALF, PEER_HALF), :].astype(BF16))

        def select(exact_ties):
            v1, rank1, n1 = _extract_top(s1, exact_ties)
            v2, rank2, n2 = _extract_top(s2, exact_ties)
            counts, n3 = _candidate_counts(v1, v2, ci, neg, exact_ties)
            v1_ref[...], v2_ref[...], counts_ref[...] = v1, v2, counts
            rank1_ref[...], rank2_ref[...] = rank1, rank2
            full = float(PEER_TOPK)
            return jnp.where((n1 == full) & (n2 == full) & (n3 == full), 0.0, 1.0)

        tied = jnp.max(select(False))

        @pl.when(tied > 0.0)
        def _():
            select(True)

        v1, v2, counts = v1_ref[...], v2_ref[...], counts_ref[...]
        rank1 = rank1_ref[...]
        used = jnp.sum(jnp.where(counts > 0.0, 1.0, 0.0), axis=0, keepdims=True)
        cnt = jnp.where(rank1 < used, 1.0, 0.0)
        for r in range(PEER_TOPK // 2):
            cnt = jnp.where(rank1 == float(r), counts[r:r + 1], cnt)
        e1 = jnp.exp(v1 - v1[0:1])
        e2 = jnp.exp(v2 - v2[0:1])
        inner = jnp.zeros(v1.shape, F32)
        for r in range(PEER_TOPK):
            inner += jnp.where(counts > float(r), e2[r:r + 1], 0.0)
        z = jnp.sum(e1 * inner, axis=0, keepdims=True)
        rows = pl.ds(pl.multiple_of(h * N_KEYS, N_KEYS), N_KEYS)
        cnt_out[rows, :] = cnt
        p1_out[rows, :] = jnp.exp(s1 - v1[0:1])
        half = pl.ds(pl.multiple_of(h * (N_KEYS // 2), N_KEYS // 2), N_KEYS // 2)
        rk_out[half, :] = pltpu.bitcast(rank2_ref[...].astype(BF16), jnp.uint32)
        p2_out[half, :] = pltpu.bitcast((jnp.exp(s2 - v2[0:1]) / z).astype(BF16), jnp.uint32)
        return carry

    lax.fori_loop(0, PEER_HEADS, head, 0)


def _peer_route(ht, lw):
    tokens = ht.shape[1]
    rt = min(ROUTE_TOKEN_TILE, tokens)
    tab = jnp.asarray(_peer_candidate_tables(rt))
    n_rows = PEER_HEADS * N_KEYS
    dense = lambda dt, rows: jax.ShapeDtypeStruct((rows, tokens), dt)
    out_spec = lambda rows: pl.BlockSpec((rows, rt), lambda i: (0, i))
    return pl.pallas_call(
        _route_kernel,
        out_shape=[dense(F32, n_rows), dense(F32, n_rows),
                   dense(jnp.uint32, n_rows // 2), dense(jnp.uint32, n_rows // 2)],
        grid=(tokens // rt,),
        in_specs=[pl.BlockSpec((D_MODEL, rt), lambda i: (0, i)),
                  _const_spec(lw["pwq"].shape), _const_spec(lw["pkeys"].shape), _const_spec(tab.shape)],
        out_specs=[out_spec(n_rows), out_spec(n_rows), out_spec(n_rows // 2), out_spec(n_rows // 2)],
        scratch_shapes=[pltpu.VMEM((PEER_HEADS * PEER_QDIM, rt), F32),
                        pltpu.VMEM((PEER_TOPK, rt), F32), pltpu.VMEM((PEER_TOPK, rt), F32),
                        pltpu.VMEM((N_KEYS, rt), F32), pltpu.VMEM((N_KEYS, rt), F32),
                        pltpu.VMEM((PEER_TOPK, rt), F32)],
        compiler_params=_params("parallel"),
        name="peer_route",
    )(ht, lw["pwq"], lw["pkeys"], tab)


def _gelu_tanh(x):
    k1 = -2.0 * 0.7978845608028654 * 1.4426950408889634
    e = jnp.exp2(x * (k1 + (k1 * 0.044715) * (x * x)))
    return x / (1.0 + e)


def _peer_kernel(alpha, ht_ref, u_ref, vt_ref, cnt_ref, p1_ref, rk_ref, p2_ref, x1_ref, mod_ref,
                 lg_ref, lb_ref, o_ref, acc_ref, a_ref, g_ref):
    j = pl.program_id(1)
    tb = ht_ref.shape[1]
    first_keys = u_ref.shape[0] // N_KEYS

    @pl.when(j == 0)
    def _():
        acc_ref[...] = jnp.zeros_like(acc_ref)

    head_rows = lambda h: pl.ds(pl.multiple_of(h * N_KEYS + j * first_keys, first_keys), first_keys)
    ht = ht_ref[...]
    n_chunks = first_keys // PEER_CHUNK_KEYS
    chunk_rows = lambda c: slice(c * PEER_CHUNK_KEYS * N_KEYS, (c + 1) * PEER_CHUNK_KEYS * N_KEYS)

    def activations(c):
        a_ref[chunk_rows(c), :] = _dot(u_ref[chunk_rows(c), :], ht)

    activations(0)
    for c in range(n_chunks):
        rows = chunk_rows(c)
        if c + 1 < n_chunks:
            activations(c + 1)
        for a in range(c * PEER_CHUNK_KEYS, (c + 1) * PEER_CHUNK_KEYS):
            for lg in range(tb // LANES):
                ls = slice(lg * LANES, (lg + 1) * LANES)
                row_a = lambda ref, h: jnp.broadcast_to(ref[head_rows(h), ls][a:a + 1], (16, LANES)).astype(BF16)
                groups = range(N_KEYS // 16)
                w = [jnp.zeros((16, LANES), BF16) for _ in groups]
                for h in range(PEER_HEADS):
                    cb = row_a(cnt_ref, h)
                    pb = row_a(p1_ref, h)
                    for g in groups:
                        ks = slice(h * (N_KEYS // 2) + g * 8, h * (N_KEYS // 2) + (g + 1) * 8)
                        rk = pltpu.bitcast(rk_ref[ks, ls], BF16)
                        p2 = pltpu.bitcast(p2_ref[ks, ls], BF16)
                        w[g] += jnp.where(rk < cb, p2, 0.0) * pb
                for g in groups:
                    er = slice(a * N_KEYS + g * 16, a * N_KEYS + (g + 1) * 16)
                    g_ref[er, ls] = w[g] * _gelu_tanh(a_ref[er, ls].astype(BF16))
        acc_ref[...] += _dot_tn(vt_ref[rows, :], g_ref[rows, :])

    @pl.when(j == pl.num_programs(1) - 1)
    def _():
        m = mod_ref[0]
        o_ref[...] = _layer_norm(alpha * x1_ref[...] + m[5:6] * acc_ref[...].T, lg_ref[...], lb_ref[...])


def _peer_dense(ht, route, x1, mod, lw, tables, layer, seq_len, alpha):
    tokens = x1.shape[0]
    tb = min(PEER_TOKEN_TILE, tokens)
    et = PEER_EXPERT_TILE
    assert (et // N_KEYS) % 8 == 0, "whole sublane tiles of first-key rows per expert tile"
    dense = pl.BlockSpec((PEER_HEADS * N_KEYS, tb), lambda i, j: (0, i))
    packed = pl.BlockSpec((PEER_HEADS * N_KEYS // 2, tb), lambda i, j: (0, i))
    return pl.pallas_call(
        functools.partial(_peer_kernel, alpha),
        out_shape=jax.ShapeDtypeStruct((tokens, D_MODEL), F32),
        grid=(tokens // tb, N_EXPERTS // et),
        in_specs=[pl.BlockSpec((D_MODEL, tb), lambda i, j: (0, i)),
                  pl.BlockSpec((None, et, D_MODEL), lambda i, j: (layer, j, 0)),
                  pl.BlockSpec((None, et, D_MODEL), lambda i, j: (layer, j, 0)),
                  dense, dense, packed, packed,
                  pl.BlockSpec((tb, D_MODEL), lambda i, j: (i, 0)),
                  pl.BlockSpec((1, 6, D_MODEL), lambda i, j: ((i * tb) // seq_len % mod.shape[0], 0, 0)),
                  _const_spec(lw["ln2g"].shape), _const_spec(lw["ln2b"].shape)],
        out_specs=pl.BlockSpec((tb, D_MODEL), lambda i, j: (i, 0)),
        scratch_shapes=[pltpu.VMEM((D_MODEL, tb), F32),
                        pltpu.VMEM((et, tb), F32),
                        pltpu.VMEM((et, tb), BF16)],
        compiler_params=_params("parallel", "arbitrary"),
        name="peer_dense",
    )(ht, *tables, *route, x1, mod, lw["ln2g"], lw["ln2b"])


def _rope_tables(seq_len):
    rows = seq_len // GRID_W
    r = jnp.repeat(jnp.arange(rows, dtype=F32), GRID_W)
    col = jnp.tile(jnp.arange(GRID_W, dtype=F32), rows)
    inv = ROPE_THETA ** (-jnp.arange(ROPE_FREQS, dtype=F32) / ROPE_FREQS)
    ang = jnp.stack([r[:, None] * inv, col[:, None] * inv], axis=1)
    cos, sin = jnp.cos(ang), jnp.sin(ang)
    cos_h = jnp.concatenate([cos, cos], axis=-1).reshape(seq_len, HEAD_DIM)
    sin_h = jnp.concatenate([-sin, sin], axis=-1).reshape(seq_len, HEAD_DIM)
    return jnp.tile(cos_h, (1, N_HEADS)), jnp.tile(sin_h, (1, N_HEADS))


def _layer_weights(l, w_in, q_norm, k_norm, gate_w2, gate_b, gla_norm, w_attn_o, w_gla_o, w_out,
                   ln1_g, ln1_b, ln2_g, ln2_b, peer_wq, peer_sub_keys, peer_u, peer_v):
    w = w_in[l].astype(BF16)
    o_q, o_k, o_g, o_lr, o_gm = 0, D_MODEL, D_MODEL + 2 * KV_WIDTH, 0, 0
    o_lr = o_g + 2 * GLA_QK_WIDTH + 2 * GLA_V_WIDTH
    o_gm = o_lr + 2 * GATE_RANK
    head_id = np.arange(D_MODEL) // HEAD_DIM
    mq = jnp.asarray((head_id[:, None] == head_id[None, :]).astype(np.float32) / HEAD_DIM, BF16)
    w2 = jnp.zeros((LANES, 2 * GLA_QK_WIDTH), F32)
    w2 = w2.at[:GATE_RANK, :GLA_QK_WIDTH].set(gate_w2[l, 0])
    w2 = w2.at[GATE_RANK:2 * GATE_RANK, GLA_QK_WIDTH:].set(gate_w2[l, 1])
    row = lambda a: a.reshape(1, -1)
    return dict(
        wq=w[:, o_q:o_k], wkv=w[:, o_k:o_g], wg=w[:, o_g:o_lr],
        wglr=jnp.pad(w[:, o_lr:o_gm], ((0, 0), (0, LANES - 2 * GATE_RANK))),
        wgm=w[:, o_gm:], mq=mq,
        qg=row(jnp.tile(q_norm[l], N_HEADS)), kg=row(jnp.tile(k_norm[l], KV_HEADS)),
        w2=w2.astype(BF16), gb=row(gate_b[l]),
        wa=w_attn_o[l].astype(BF16), wl=w_gla_o[l].astype(BF16), wo=w_out[l].astype(BF16),
        gn=row(jnp.tile(gla_norm[l], GLA_HEADS)),
        ln1g=row(ln1_g[l]), ln1b=row(ln1_b[l]), ln2g=row(ln2_g[l]), ln2b=row(ln2_b[l]),
        pwq=peer_wq[l].T.astype(BF16), pkeys=peer_sub_keys[l].astype(BF16),
    )


def _trunk_layer(x, mod, lw, batch, seq_len, alpha, consts, ctx, layer):
    rope_tabs = None if ctx is None else consts["rope"]
    q, k, v, gq, gk, gv, go, la, gm = _inproj(x, mod, lw, seq_len, rope_tabs)
    if ctx is None:
        attn = _attention(q, k, v, batch, seq_len, min(seq_len, 256), None, layer)
        og, states = _gla(gq, gk, gv, la, batch, seq_len, consts["tri"], None, layer)
    else:
        attn = _attention(q, k, v, batch, seq_len, 128, ctx[:2], layer)
        og, states = _gla(gq, gk, gv, la, batch, seq_len, consts["tri"], ctx[2], layer)
    x1, ht = _postmix(x, mod, attn, og, go, gm, lw, seq_len, alpha)
    route = _peer_route(ht, lw)
    x2 = _peer_dense(ht, route, x1, mod, lw, consts["tables"], layer, seq_len, alpha)
    return x2, (k, v, states)


def kernel(x_prompt, x_sample, cache_k, cache_v, state_gla, c, c_ctx, ada_w, ada_b, w_in, q_norm, k_norm,
           gate_w2, gate_b, gla_norm, w_attn_o, w_gla_o, w_out, ln1_g, ln1_b, ln2_g, ln2_b,
           peer_wq, peer_sub_keys, peer_u, peer_v):
    depth = ada_w.shape[0]
    alpha = (2.0 * depth) ** 0.25
    batch, seq, _ = x_prompt.shape
    dec_batch, dec_seq, _ = x_sample.shape
    past = cache_k.shape[2]

    n_cond = 1 + dec_batch
    cond = jnp.concatenate([c_ctx[None], c, jnp.zeros((-n_cond % 8, D_MODEL), F32)], axis=0)
    mod = _ada_mod(cond, ada_w, ada_b).reshape(depth, cond.shape[0], 6, D_MODEL)

    idx = np.arange(GLA_CHUNK)
    tri = jnp.asarray(np.stack([idx[None, :] <= idx[:, None], idx[None, :] >= idx[:, None]]), BF16)
    consts = dict(tri=tri, rope=_rope_tables(dec_seq), tables=(peer_u.astype(BF16), peer_v.astype(BF16)))
    weights = [_layer_weights(l, w_in, q_norm, k_norm, gate_w2, gate_b, gla_norm, w_attn_o, w_gla_o, w_out,
                              ln1_g, ln1_b, ln2_g, ln2_b, peer_wq, peer_sub_keys, peer_u, peer_v)
               for l in range(depth)]

    xp = x_prompt.reshape(batch * seq, D_MODEL)
    ks, vs, ss = [], [], []
    for l in range(depth):
        xp, (k_l, v_l, s_l) = _trunk_layer(xp, mod[l, 0:1], weights[l], batch, seq, alpha, consts, None, l)
        ks.append(k_l.reshape(batch, seq, KV_HEADS, HEAD_DIM))
        vs.append(v_l.reshape(batch, seq, KV_HEADS, HEAD_DIM))
        ss.append(s_l)
    new_cache_k = jnp.stack(ks, axis=1)
    new_cache_v = jnp.stack(vs, axis=1)
    new_state = jnp.stack(ss, axis=1)

    ctx = (cache_k.reshape(dec_batch, depth, past, KV_WIDTH), cache_v.reshape(dec_batch, depth, past, KV_WIDTH),
           state_gla)
    xs = x_sample.reshape(dec_batch * dec_seq, D_MODEL)
    for l in range(depth):
        xs, _ = _trunk_layer(xs, mod[l, 1:1 + dec_batch], weights[l], dec_batch, dec_seq, alpha, consts, ctx, l)

    return (xp.reshape(batch, seq, D_MODEL), xs.reshape(dec_batch, dec_seq, D_MODEL),
            new_cache_k, new_cache_v, new_state)
```

```python
import functools

import numpy as np
import jax
import jax.numpy as jnp
from jax import lax
from jax.experimental import pallas as pl
from jax.experimental.pallas import tpu as pltpu

F32 = jnp.float32
BF16 = jnp.bfloat16

D_MODEL = 1024
HEAD_DIM = 64
N_HEADS = D_MODEL // HEAD_DIM
KV_HEADS = N_HEADS // 4
Q_PER_KV = N_HEADS // KV_HEADS
KV_WIDTH = KV_HEADS * HEAD_DIM
GRID_W = 64
ROPE_FREQS = HEAD_DIM // 4
ROPE_THETA = 10000.0
GLA_HEADS = 4
GLA_DK = D_MODEL // 2 // GLA_HEADS
GLA_DV = D_MODEL // GLA_HEADS
GLA_QK_WIDTH = GLA_HEADS * GLA_DK
GLA_V_WIDTH = GLA_HEADS * GLA_DV
GATE_RANK = 16
GATE_NORM = 16.0
GLA_CHUNK = 64
N_KEYS = 128
N_EXPERTS = N_KEYS * N_KEYS
PEER_HEADS = 8
PEER_TOPK = 16
PEER_QDIM = 256
PEER_HALF = PEER_QDIM // 2
LN_EPS = 1e-5
RMS_EPS = 1e-6

V7X_VMEM_BYTES = 64 * 1024 * 1024
VMEM_LIMIT = V7X_VMEM_BYTES - 8 * 1024 * 1024
LANES = 128

TOKEN_TILE = 256
PEER_TOKEN_TILE = 512
ROUTE_TOKEN_TILE = 512
PEER_EXPERT_TILE = 1024
PEER_CHUNK_KEYS = 4


def _dot(a, b):
    return jnp.dot(a, b, preferred_element_type=F32)


def _dot_nt(a, b):
    return lax.dot_general(a, b, (((1,), (1,)), ((), ())), preferred_element_type=F32)


def _dot_tn(a, b):
    return lax.dot_general(a, b, (((0,), (0,)), ((), ())), preferred_element_type=F32)


def _const_spec(shape):
    zeros = (0,) * len(shape)
    return pl.BlockSpec(shape, lambda *_: zeros)


def _params(*sem):
    return pltpu.CompilerParams(dimension_semantics=sem, vmem_limit_bytes=VMEM_LIMIT)


def _layer_norm(x, g, b):
    mu = jnp.mean(x, axis=-1, keepdims=True)
    xc = x - mu
    var = jnp.mean(xc * xc, axis=-1, keepdims=True)
    return xc * lax.rsqrt(var + LN_EPS) * g + b


def _ada_kernel(c_ref, w_ref, b_ref, o_ref):
    c = c_ref[...]
    s = c * jax.nn.sigmoid(c)
    o_ref[...] = _dot(s.astype(BF16), w_ref[...].astype(BF16)) + b_ref[...]


def _ada_mod(cond, ada_w, ada_b):
    depth = ada_w.shape[0]
    rows = cond.shape[0]
    return pl.pallas_call(
        _ada_kernel,
        out_shape=jax.ShapeDtypeStruct((depth, rows, 6 * D_MODEL), F32),
        grid=(depth, 6),
        in_specs=[
            pl.BlockSpec((rows, D_MODEL), lambda l, j: (0, 0)),
            pl.BlockSpec((None, D_MODEL, D_MODEL), lambda l, j: (l, 0, j)),
            pl.BlockSpec((None, 1, D_MODEL), lambda l, j: (l, 0, j)),
        ],
        out_specs=pl.BlockSpec((None, rows, D_MODEL), lambda l, j: (l, 0, j)),
        compiler_params=_params("parallel", "parallel"),
        name="ada_mod",
    )(cond, ada_w, ada_b.reshape(depth, 1, 6 * D_MODEL))


def _rope(t, cos, sin_signed):
    width = t.shape[-1]
    up = pltpu.roll(t, width - ROPE_FREQS, 1)
    dn = pltpu.roll(t, ROPE_FREQS, 1)
    lane = lax.broadcasted_iota(jnp.int32, t.shape, 1)
    partner = jnp.where((lane & ROPE_FREQS) == 0, up, dn)
    return t * cos + partner * sin_signed


def _inproj_kernel(rope, *refs):
    (x_ref, mod_ref, wq_ref, wkv_ref, wg_ref, wglr_ref, wgm_ref, mq_ref, qg_ref, kg_ref,
     w2_ref, gb_ref) = refs[:12]
    refs = refs[12:]
    if rope:
        cos_ref, sin_ref = refs[:2]
        refs = refs[2:]
    q_out, k_out, v_out, gq_out, gk_out, gv_out, go_out, la_out, gm_out = refs

    m = mod_ref[0]
    h = (x_ref[...] * (1.0 + m[1:2]) + m[0:1]).astype(BF16)

    q = _dot(h, wq_ref[...])
    qn = q * lax.rsqrt(_dot((q * q).astype(BF16), mq_ref[...]) + RMS_EPS) * qg_ref[...]
    kv = _dot(h, wkv_ref[...])
    k = kv[:, :KV_WIDTH]
    kn = k * lax.rsqrt(_dot((k * k).astype(BF16), mq_ref[:KV_WIDTH, :KV_WIDTH]) + RMS_EPS) * kg_ref[...]
    if rope:
        cos = cos_ref[...]
        sin = sin_ref[...]
        qn = _rope(qn, cos, sin)
        kn = _rope(kn, cos[:, :KV_WIDTH], sin[:, :KV_WIDTH])
    q_out[...] = qn * (HEAD_DIM ** -0.5)
    k_out[...] = kn
    v_out[...] = kv[:, KV_WIDTH:]

    g = _dot(h, wg_ref[...])
    gq_out[...] = g[:, :GLA_QK_WIDTH] * (GLA_DK ** -0.5)
    gk_out[...] = g[:, GLA_QK_WIDTH:2 * GLA_QK_WIDTH]
    gv_out[...] = g[:, 2 * GLA_QK_WIDTH:2 * GLA_QK_WIDTH + GLA_V_WIDTH]
    go_out[...] = g[:, 2 * GLA_QK_WIDTH + GLA_V_WIDTH:]

    glr = _dot(h, wglr_ref[...])
    z = _dot(glr.astype(BF16), w2_ref[...]) + gb_ref[...]
    la_out[...] = (jnp.minimum(z, 0.0) - jnp.log1p(jnp.exp(-jnp.abs(z)))) * (1.0 / GATE_NORM)
    gm_out[...] = jax.nn.sigmoid(_dot(h, wgm_ref[...]))


def _inproj(x, mod, lw, seq_len, rope_tabs):
    tokens = x.shape[0]
    tm = TOKEN_TILE
    rope = rope_tabs is not None
    row = lambda i: (i, 0)
    ins = [x, mod, lw["wq"], lw["wkv"], lw["wg"], lw["wglr"], lw["wgm"], lw["mq"], lw["qg"], lw["kg"],
           lw["w2"], lw["gb"]]
    in_specs = [pl.BlockSpec((tm, D_MODEL), row),
                pl.BlockSpec((1, 6, D_MODEL), lambda i: ((i * tm) // seq_len % mod.shape[0], 0, 0))]
    in_specs += [_const_spec(a.shape) for a in ins[2:]]
    if rope:
        per_seq = seq_len // tm
        ins += list(rope_tabs)
        in_specs += [pl.BlockSpec((tm, D_MODEL), lambda i: (i % per_seq, 0))] * 2
    widths = (D_MODEL, KV_WIDTH, KV_WIDTH, GLA_QK_WIDTH, GLA_QK_WIDTH, GLA_V_WIDTH, GLA_V_WIDTH,
              2 * GLA_QK_WIDTH, 2 * D_MODEL)
    return pl.pallas_call(
        functools.partial(_inproj_kernel, rope),
        out_shape=[jax.ShapeDtypeStruct((tokens, w), F32) for w in widths],
        grid=(tokens // tm,),
        in_specs=in_specs,
        out_specs=[pl.BlockSpec((tm, w), row) for w in widths],
        compiler_params=_params("parallel"),
        name="inproj_rope" if rope else "inproj",
    )(*ins)


def _attn_kernel(has_ctx, *refs):
    if has_ctx:
        q_ref, k_ref, v_ref, ck_ref, cv_ref, o_ref = refs
    else:
        q_ref, k_ref, v_ref, o_ref = refs
    tq = q_ref.shape[0]
    for g in range(KV_HEADS):
        gs = slice(g * HEAD_DIM, (g + 1) * HEAD_DIM)
        kg = k_ref[:, gs].astype(BF16)
        vg = v_ref[:, gs].astype(BF16)
        if has_ctx:
            kg = jnp.concatenate([kg, ck_ref[:, gs].astype(BF16)], axis=0)
            vg = jnp.concatenate([vg, cv_ref[:, gs].astype(BF16)], axis=0)
        heads = [q_ref[:, (Q_PER_KV * g + r) * HEAD_DIM:(Q_PER_KV * g + r + 1) * HEAD_DIM]
                 for r in range(Q_PER_KV)]
        qs = jnp.concatenate(heads, axis=0).astype(BF16)
        s = _dot_nt(qs, kg)
        p = jnp.exp(s - jnp.max(s, axis=-1, keepdims=True))
        o = _dot(p.astype(BF16), vg) / jnp.sum(p, axis=-1, keepdims=True)
        for r in range(Q_PER_KV):
            h0 = (Q_PER_KV * g + r) * HEAD_DIM
            o_ref[:, h0:h0 + HEAD_DIM] = o[r * tq:(r + 1) * tq]


def _attention(q, k, v, batch, seq_len, tq, ctx_kv, layer):
    tokens = q.shape[0]
    nq = seq_len // tq
    ins = [q, k, v]
    in_specs = [pl.BlockSpec((tq, D_MODEL), lambda b, i: (b * nq + i, 0)),
                pl.BlockSpec((seq_len, KV_WIDTH), lambda b, i: (b, 0)),
                pl.BlockSpec((seq_len, KV_WIDTH), lambda b, i: (b, 0))]
    if ctx_kv is not None:
        past = ctx_kv[0].shape[2]
        ins += list(ctx_kv)
        in_specs += [pl.BlockSpec((None, None, past, KV_WIDTH), lambda b, i: (b, layer, 0, 0))] * 2
    return pl.pallas_call(
        functools.partial(_attn_kernel, ctx_kv is not None),
        out_shape=jax.ShapeDtypeStruct((tokens, D_MODEL), F32),
        grid=(batch, nq),
        in_specs=in_specs,
        out_specs=pl.BlockSpec((tq, D_MODEL), lambda b, i: (b * nq + i, 0)),
        compiler_params=_params("parallel", "parallel"),
        name="attention_ctx" if ctx_kv is not None else "attention",
    )(*ins)


def _split3(x):
    hi = x.astype(BF16)
    r = x - hi.astype(F32)
    mid = r.astype(BF16)
    lo = (r - mid.astype(F32)).astype(BF16)
    return hi, mid, lo


def _gla_kernel(has_s0, *refs):
    gq_ref, gk_ref, gv_ref, la_ref, tri_ref = refs[:5]
    refs = refs[5:]
    if has_s0:
        s0_ref = refs[0]
        refs = refs[1:]
    o_ref, st_out, bf_ref, bb_ref, st_ref = refs
    seq_len = gq_ref.shape[0]
    n_chunks = seq_len // GLA_CHUNK
    tril = tri_ref[0]
    triu = tri_ref[1]

    def cumsum_chunk(c, carry):
        rows = pl.ds(pl.multiple_of(c * GLA_CHUNK, GLA_CHUNK), GLA_CHUNK)
        la = la_ref[rows, :]
        pf = _split3(la[:, :GLA_QK_WIDTH])
        pb = _split3(la[:, GLA_QK_WIDTH:])
        bf_ref[rows, :] = _dot(tril, pf[0]) + _dot(tril, pf[1]) + _dot(tril, pf[2])
        bb_ref[rows, :] = _dot(triu, pb[0]) + _dot(triu, pb[1]) + _dot(triu, pb[2])
        return carry

    lax.fori_loop(0, n_chunks, cumsum_chunk, 0)

    ri = lax.broadcasted_iota(jnp.int32, (GLA_CHUNK, GLA_CHUNK), 0)
    ci = lax.broadcasted_iota(jnp.int32, (GLA_CHUNK, GLA_CHUNK), 1)
    for d in range(2):
        for h in range(GLA_HEADS):
            if has_s0:
                st_ref[d * GLA_HEADS + h] = s0_ref[d, h].T
            else:
                st_ref[d * GLA_HEADS + h] = jnp.zeros((GLA_DV, GLA_DK), F32)
    o_ref[...] = jnp.zeros_like(o_ref)

    def step(i, carry):
        for d in range(2):
            b_ref = bf_ref if d == 0 else bb_ref
            keep = (ci <= ri) if d == 0 else (ci >= ri)
            c = i if d == 0 else n_chunks - 1 - i
            rows = pl.ds(pl.multiple_of(c * GLA_CHUNK, GLA_CHUNK), GLA_CHUNK)
            for h in range(GLA_HEADS):
                ks = slice(h * GLA_DK, (h + 1) * GLA_DK)
                vs = slice(h * GLA_DV, (h + 1) * GLA_DV)
                b = b_ref[rows, ks]
                bl = b[GLA_CHUNK - 1:GLA_CHUNK] if d == 0 else b[0:1]
                kk = gk_ref[rows, ks]
                v = gv_ref[rows, vs].astype(BF16)
                qe = (gq_ref[rows, ks] * jnp.exp(b)).astype(BF16)
                ke = (kk * jnp.exp(-b)).astype(BF16)
                kl = (kk * jnp.exp(bl - b)).astype(BF16)
                a = jnp.where(keep, _dot_nt(qe, ke), 0.0).astype(BF16)
                st = st_ref[d * GLA_HEADS + h]
                o_ref[rows, vs] += _dot(a, v) + _dot_nt(qe, st.astype(BF16))
                st_ref[d * GLA_HEADS + h] = st * jnp.exp(bl) + _dot_tn(v, kl)
        return carry

    lax.fori_loop(0, n_chunks, step, 0)
    for d in range(2):
        for h in range(GLA_HEADS):
            st_out[d, h] = st_ref[d * GLA_HEADS + h].T


def _gla(gq, gk, gv, la, batch, seq_len, tri, s0, layer):
    tokens = gq.shape[0]
    seq = lambda w: pl.BlockSpec((seq_len, w), lambda b: (b, 0))
    ins = [gq, gk, gv, la, tri]
    in_specs = [seq(GLA_QK_WIDTH), seq(GLA_QK_WIDTH), seq(GLA_V_WIDTH), seq(2 * GLA_QK_WIDTH),
                _const_spec(tri.shape)]
    if s0 is not None:
        ins.append(s0)
        in_specs.append(pl.BlockSpec((None, None, 2, GLA_HEADS, GLA_DK, GLA_DV),
                                     lambda b: (b, layer, 0, 0, 0, 0)))
    return pl.pallas_call(
        functools.partial(_gla_kernel, s0 is not None),
        out_shape=[jax.ShapeDtypeStruct((tokens, GLA_V_WIDTH), F32),
                   jax.ShapeDtypeStruct((batch, 2, GLA_HEADS, GLA_DK, GLA_DV), F32)],
        grid=(batch,),
        in_specs=in_specs,
        out_specs=[seq(GLA_V_WIDTH),
                   pl.BlockSpec((None, 2, GLA_HEADS, GLA_DK, GLA_DV), lambda b: (b, 0, 0, 0, 0))],
        scratch_shapes=[pltpu.VMEM((seq_len, GLA_QK_WIDTH), F32),
                        pltpu.VMEM((seq_len, GLA_QK_WIDTH), F32),
                        pltpu.VMEM((2 * GLA_HEADS, GLA_DV, GLA_DK), F32)],
        compiler_params=_params("parallel"),
        name="gla_s0" if s0 is not None else "gla",
    )(*ins)


def _postmix_kernel(alpha, x_ref, mod_ref, at_ref, og_ref, go_ref, gm_ref, wa_ref, wl_ref, wo_ref,
                    gn_ref, lg_ref, lb_ref, x1_ref, ht_ref):
    m = mod_ref[0]
    og = og_ref[...]
    parts = []
    for h in range(GLA_HEADS):
        oh = og[:, h * GLA_DV:(h + 1) * GLA_DV]
        parts.append(oh * lax.rsqrt(jnp.mean(oh * oh, axis=-1, keepdims=True) + RMS_EPS))
    go = go_ref[...]
    o = jnp.concatenate(parts, axis=-1) * gn_ref[...] * (go * jax.nn.sigmoid(go))
    gm = gm_ref[...]
    y = (gm[:, :D_MODEL] * _dot(at_ref[...].astype(BF16), wa_ref[...])
         + gm[:, D_MODEL:] * _dot(o.astype(BF16), wl_ref[...]))
    mix = _dot(y.astype(BF16), wo_ref[...])
    x1 = _layer_norm(alpha * x_ref[...] + m[2:3] * mix, lg_ref[...], lb_ref[...])
    x1_ref[...] = x1
    ht_ref[...] = (x1 * (1.0 + m[4:5]) + m[3:4]).T.astype(BF16)


def _postmix(x, mod, attn, og, go, gm, lw, seq_len, alpha):
    tokens = x.shape[0]
    tm = TOKEN_TILE
    row = lambda w: pl.BlockSpec((tm, w), lambda i: (i, 0))
    consts = [lw["wa"], lw["wl"], lw["wo"], lw["gn"], lw["ln1g"], lw["ln1b"]]
    return pl.pallas_call(
        functools.partial(_postmix_kernel, alpha),
        out_shape=[jax.ShapeDtypeStruct((tokens, D_MODEL), F32),
                   jax.ShapeDtypeStruct((D_MODEL, tokens), BF16)],
        grid=(tokens // tm,),
        in_specs=[row(D_MODEL),
                  pl.BlockSpec((1, 6, D_MODEL), lambda i: ((i * tm) // seq_len % mod.shape[0], 0, 0)),
                  row(D_MODEL), row(D_MODEL), row(D_MODEL), row(2 * D_MODEL)]
                 + [_const_spec(a.shape) for a in consts],
        out_specs=[row(D_MODEL), pl.BlockSpec((D_MODEL, tm), lambda i: (0, i))],
        compiler_params=_params("parallel"),
        name="postmix",
    )(x, mod, attn, og, go, gm, *consts)


def _peer_candidate_tables(lanes):
    groups = [[(0, r) for r in range(16)], [(r, 0) for r in range(16)]]
    for t in (1, 2, 3):
        groups.append([(t, r) for r in range(8)])
        if t < 3:
            groups.append([(r, t) for r in range(8)])
    seen = set()
    ci, neg = [], []
    for grp in groups:
        for (r1, r2) in grp:
            ok = (r1 + 1) * (r2 + 1) <= PEER_TOPK and (r1, r2) not in seen
            if ok:
                seen.add((r1, r2))
            ci.append(float(r1 * PEER_TOPK + r2) if ok else 1e9)
            neg.append(0.0 if ok else -np.inf)
    tab = np.stack([np.asarray(ci, np.float32), np.asarray(neg, np.float32)])
    return np.ascontiguousarray(np.broadcast_to(tab[:, :, None], tab.shape + (lanes,)))


def _extract_top(s, exact_ties):
    key = lax.broadcasted_iota(jnp.int32, s.shape, 0).astype(F32)
    slot = lax.broadcasted_iota(jnp.int32, (PEER_TOPK, s.shape[1]), 0)
    rank = jnp.full(s.shape, float(PEER_TOPK), F32)
    vals = jnp.zeros((PEER_TOPK, s.shape[1]), F32)
    for r in range(PEER_TOPK):
        m = jnp.max(s, axis=0, keepdims=True)
        hit = s == m
        if exact_ties:
            hit = key == jnp.min(jnp.where(hit, key, float(N_KEYS)), axis=0, keepdims=True)
        rank = jnp.where(hit, float(r), rank)
        s = jnp.where(hit, -jnp.inf, s)
        vals = jnp.where(slot == r, m, vals)
    ranked = jnp.sum(jnp.where(rank < float(PEER_TOPK), 1.0, 0.0), axis=0, keepdims=True)
    return vals, rank, ranked


def _candidate_counts(v1, v2, ci, neg, exact_ties):
    lo = slice(0, 8)
    cand = jnp.concatenate([
        v1[0:1] + v2, v1 + v2[0:1],
        v1[1:2] + v2[lo], v1[lo] + v2[1:2],
        v1[2:3] + v2[lo], v1[lo] + v2[2:3],
        v1[3:4] + v2[lo]], axis=0) + neg
    taken = jnp.zeros(cand.shape, F32)
    for _ in range(PEER_TOPK):
        m = jnp.max(cand, axis=0, keepdims=True)
        hit = cand == m
        if exact_ties:
            hit = ci == jnp.min(jnp.where(hit, ci, 2e9), axis=0, keepdims=True)
        taken = jnp.where(hit, 1.0, taken)
        cand = jnp.where(hit, -jnp.inf, cand)
    row_sum = lambda a, b: jnp.sum(taken[a:b], axis=0, keepdims=True)
    slot = lax.broadcasted_iota(jnp.int32, v1.shape, 0)
    counts = taken[16:32] + jnp.concatenate(
        [taken[40:48] + taken[56:64], jnp.zeros((8, v1.shape[1]), F32)], axis=0)
    counts += jnp.where(slot == 0, row_sum(0, 16), 0.0)
    counts += jnp.where(slot == 1, row_sum(32, 40), 0.0)
    counts += jnp.where(slot == 2, row_sum(48, 56), 0.0)
    counts += jnp.where(slot == 3, row_sum(64, 72), 0.0)
    return counts, jnp.sum(counts, axis=0, keepdims=True)


def _twin_bf16(x):
    hi = pltpu.bitcast(x.astype(BF16).astype(F32), jnp.uint32)
    return hi | (hi >> 16)


def _route_kernel(ht_ref, wq_ref, keys_ref, tab_ref, cnt_out, p1_out, rk_out, p2_out,
                  q_ref, v1_ref, v2_ref, rank1_ref, rank2_ref, counts_ref):
    q_ref[...] = _dot(wq_ref[...], ht_ref[...])
    ci = tab_ref[0]
    neg = tab_ref[1]

    def head(h, carry):
        r0 = pl.multiple_of(h * PEER_QDIM, PEER_QDIM)
        s1 = _dot(keys_ref[0], q_ref[pl.ds(r0, PEER_HALF), :].astype(BF16))
        s2 = _dot(keys_ref[1], q_ref[pl.ds(r0 + PEER_HALF, PEER_HALF), :].astype(BF16))

        def select(exact_ties):
            v1, rank1, n1 = _extract_top(s1, exact_ties)
            v2, rank2, n2 = _extract_top(s2, exact_ties)
            counts, n3 = _candidate_counts(v1, v2, ci, neg, exact_ties)
            v1_ref[...], v2_ref[...], counts_ref[...] = v1, v2, counts
            rank1_ref[...], rank2_ref[...] = rank1, rank2
            full = float(PEER_TOPK)
            return jnp.where((n1 == full) & (n2 == full) & (n3 == full), 0.0, 1.0)

        tied = jnp.max(select(False))

        @pl.when(tied > 0.0)
        def _():
            select(True)

        v1, v2, counts = v1_ref[...], v2_ref[...], counts_ref[...]
        rank1 = rank1_ref[...]
        used = jnp.sum(jnp.where(counts > 0.0, 1.0, 0.0), axis=0, keepdims=True)
        cnt = jnp.where(rank1 < used, 1.0, 0.0)
        for r in range(PEER_TOPK // 2):
            cnt = jnp.where(rank1 == float(r), counts[r:r + 1], cnt)
        e1 = jnp.exp(v1 - v1[0:1])
        e2 = jnp.exp(v2 - v2[0:1])
        inner = jnp.zeros(v1.shape, F32)
        for r in range(PEER_TOPK):
            inner += jnp.where(counts > float(r), e2[r:r + 1], 0.0)
        z = jnp.sum(e1 * inner, axis=0, keepdims=True)
        rows = pl.ds(pl.multiple_of(h * N_KEYS, N_KEYS), N_KEYS)
        cnt_out[rows, :] = _twin_bf16(cnt)
        p1_out[rows, :] = _twin_bf16(jnp.exp(s1 - v1[0:1]))
        half = pl.ds(pl.multiple_of(h * (N_KEYS // 2), N_KEYS // 2), N_KEYS // 2)
        rk_out[half, :] = pltpu.bitcast(rank2_ref[...].astype(BF16), jnp.uint32)
        p2_out[half, :] = pltpu.bitcast((jnp.exp(s2 - v2[0:1]) / z).astype(BF16), jnp.uint32)
        return carry

    lax.fori_loop(0, PEER_HEADS, head, 0)


def _peer_route(ht, lw):
    tokens = ht.shape[1]
    rt = min(ROUTE_TOKEN_TILE, tokens)
    tab = jnp.asarray(_peer_candidate_tables(rt))
    n_rows = PEER_HEADS * N_KEYS
    dense = lambda dt, rows: jax.ShapeDtypeStruct((rows, tokens), dt)
    out_spec = lambda rows: pl.BlockSpec((rows, rt), lambda i: (0, i))
    return pl.pallas_call(
        _route_kernel,
        out_shape=[dense(jnp.uint32, n_rows), dense(jnp.uint32, n_rows),
                   dense(jnp.uint32, n_rows // 2), dense(jnp.uint32, n_rows // 2)],
        grid=(tokens // rt,),
        in_specs=[pl.BlockSpec((D_MODEL, rt), lambda i: (0, i)),
                  _const_spec(lw["pwq"].shape), _const_spec(lw["pkeys"].shape), _const_spec(tab.shape)],
        out_specs=[out_spec(n_rows), out_spec(n_rows), out_spec(n_rows // 2), out_spec(n_rows // 2)],
        scratch_shapes=[pltpu.VMEM((PEER_HEADS * PEER_QDIM, rt), F32),
                        pltpu.VMEM((PEER_TOPK, rt), F32), pltpu.VMEM((PEER_TOPK, rt), F32),
                        pltpu.VMEM((N_KEYS, rt), F32), pltpu.VMEM((N_KEYS, rt), F32),
                        pltpu.VMEM((PEER_TOPK, rt), F32)],
        compiler_params=_params("parallel"),
        name="peer_route",
    )(ht, lw["pwq"], lw["pkeys"], tab)


def _gelu_tanh(x):
    k1 = -2.0 * 0.7978845608028654 * 1.4426950408889634
    e = jnp.exp2(x * (k1 + (k1 * 0.044715) * (x * x)))
    return x / (1.0 + e)


def _peer_kernel(alpha, ht_ref, u_ref, vt_ref, cnt_ref, p1_ref, rk_ref, p2_ref, x1_ref, mod_ref,
                 lg_ref, lb_ref, o_ref, acc_ref, a_ref, g_ref):
    j = pl.program_id(1)
    tb = ht_ref.shape[1]
    first_keys = u_ref.shape[0] // N_KEYS

    @pl.when(j == 0)
    def _():
        acc_ref[...] = jnp.zeros_like(acc_ref)

    head_rows = lambda h: pl.ds(pl.multiple_of(h * N_KEYS + j * first_keys, first_keys), first_keys)
    ht = ht_ref[...]
    n_chunks = first_keys // PEER_CHUNK_KEYS
    chunk_rows = lambda c: slice(c * PEER_CHUNK_KEYS * N_KEYS, (c + 1) * PEER_CHUNK_KEYS * N_KEYS)

    def activations(c):
        a_ref[chunk_rows(c), :] = _dot(u_ref[chunk_rows(c), :], ht).astype(BF16)

    activations(0)
    for c in range(n_chunks):
        rows = chunk_rows(c)
        if c + 1 < n_chunks:
            activations(c + 1)
        for a in range(c * PEER_CHUNK_KEYS, (c + 1) * PEER_CHUNK_KEYS):
            for lg in range(tb // LANES):
                ls = slice(lg * LANES, (lg + 1) * LANES)
                row_a = lambda ref, h: pltpu.bitcast(
                    jnp.broadcast_to(ref[head_rows(h), ls][a:a + 1], (8, LANES)), BF16)
                groups = range(N_KEYS // 16)
                w = [jnp.zeros((16, LANES), BF16) for _ in groups]
                for h in range(PEER_HEADS):
                    cb = row_a(cnt_ref, h)
                    pb = row_a(p1_ref, h)
                    for g in groups:
                        ks = slice(h * (N_KEYS // 2) + g * 8, h * (N_KEYS // 2) + (g + 1) * 8)
                        rk = pltpu.bitcast(rk_ref[ks, ls], BF16)
                        p2 = pltpu.bitcast(p2_ref[ks, ls], BF16)
                        w[g] += jnp.where(rk < cb, p2, 0.0) * pb
                for g in groups:
                    er = slice(a * N_KEYS + g * 16, a * N_KEYS + (g + 1) * 16)
                    g_ref[er, ls] = w[g] * _gelu_tanh(a_ref[er, ls])
        acc_ref[...] += _dot_tn(vt_ref[rows, :], g_ref[rows, :])

    @pl.when(j == pl.num_programs(1) - 1)
    def _():
        m = mod_ref[0]
        o_ref[...] = _layer_norm(alpha * x1_ref[...] + m[5:6] * acc_ref[...].T, lg_ref[...], lb_ref[...])


def _peer_dense(ht, route, x1, mod, lw, tables, layer, seq_len, alpha):
    tokens = x1.shape[0]
    tb = min(PEER_TOKEN_TILE, tokens)
    et = PEER_EXPERT_TILE
    assert (et // N_KEYS) % 8 == 0, "whole sublane tiles of first-key rows per expert tile"
    dense = pl.BlockSpec((PEER_HEADS * N_KEYS, tb), lambda i, j: (0, i))
    packed = pl.BlockSpec((PEER_HEADS * N_KEYS // 2, tb), lambda i, j: (0, i))
    return pl.pallas_call(
        functools.partial(_peer_kernel, alpha),
        out_shape=jax.ShapeDtypeStruct((tokens, D_MODEL), F32),
        grid=(tokens // tb, N_EXPERTS // et),
        in_specs=[pl.BlockSpec((D_MODEL, tb), lambda i, j: (0, i)),
                  pl.BlockSpec((None, et, D_MODEL), lambda i, j: (layer, j, 0)),
                  pl.BlockSpec((None, et, D_MODEL), lambda i, j: (layer, j, 0)),
                  dense, dense, packed, packed,
                  pl.BlockSpec((tb, D_MODEL), lambda i, j: (i, 0)),
                  pl.BlockSpec((1, 6, D_MODEL), lambda i, j: ((i * tb) // seq_len % mod.shape[0], 0, 0)),
                  _const_spec(lw["ln2g"].shape), _const_spec(lw["ln2b"].shape)],
        out_specs=pl.BlockSpec((tb, D_MODEL), lambda i, j: (i, 0)),
        scratch_shapes=[pltpu.VMEM((D_MODEL, tb), F32),
                        pltpu.VMEM((et, tb), BF16),
                        pltpu.VMEM((et, tb), BF16)],
        compiler_params=_params("parallel", "arbitrary"),
        name="peer_dense",
    )(ht, *tables, *route, x1, mod, lw["ln2g"], lw["ln2b"])


def _rope_tables(seq_len):
    rows = seq_len // GRID_W
    r = jnp.repeat(jnp.arange(rows, dtype=F32), GRID_W)
    col = jnp.tile(jnp.arange(GRID_W, dtype=F32), rows)
    inv = ROPE_THETA ** (-jnp.arange(ROPE_FREQS, dtype=F32) / ROPE_FREQS)
    ang = jnp.stack([r[:, None] * inv, col[:, None] * inv], axis=1)
    cos, sin = jnp.cos(ang), jnp.sin(ang)
    cos_h = jnp.concatenate([cos, cos], axis=-1).reshape(seq_len, HEAD_DIM)
    sin_h = jnp.concatenate([-sin, sin], axis=-1).reshape(seq_len, HEAD_DIM)
    return jnp.tile(cos_h, (1, N_HEADS)), jnp.tile(sin_h, (1, N_HEADS))


def _layer_weights(l, w_in, q_norm, k_norm, gate_w2, gate_b, gla_norm, w_attn_o, w_gla_o, w_out,
                   ln1_g, ln1_b, ln2_g, ln2_b, peer_wq, peer_sub_keys, peer_u, peer_v):
    w = w_in[l].astype(BF16)
    o_q, o_k, o_g, o_lr, o_gm = 0, D_MODEL, D_MODEL + 2 * KV_WIDTH, 0, 0
    o_lr = o_g + 2 * GLA_QK_WIDTH + 2 * GLA_V_WIDTH
    o_gm = o_lr + 2 * GATE_RANK
    head_id = np.arange(D_MODEL) // HEAD_DIM
    mq = jnp.asarray((head_id[:, None] == head_id[None, :]).astype(np.float32) / HEAD_DIM, BF16)
    w2 = jnp.zeros((LANES, 2 * GLA_QK_WIDTH), F32)
    w2 = w2.at[:GATE_RANK, :GLA_QK_WIDTH].set(gate_w2[l, 0])
    w2 = w2.at[GATE_RANK:2 * GATE_RANK, GLA_QK_WIDTH:].set(gate_w2[l, 1])
    row = lambda a: a.reshape(1, -1)
    return dict(
        wq=w[:, o_q:o_k], wkv=w[:, o_k:o_g], wg=w[:, o_g:o_lr],
        wglr=jnp.pad(w[:, o_lr:o_gm], ((0, 0), (0, LANES - 2 * GATE_RANK))),
        wgm=w[:, o_gm:], mq=mq,
        qg=row(jnp.tile(q_norm[l], N_HEADS)), kg=row(jnp.tile(k_norm[l], KV_HEADS)),
        w2=w2.astype(BF16), gb=row(gate_b[l]),
        wa=w_attn_o[l].astype(BF16), wl=w_gla_o[l].astype(BF16), wo=w_out[l].astype(BF16),
        gn=row(jnp.tile(gla_norm[l], GLA_HEADS)),
        ln1g=row(ln1_g[l]), ln1b=row(ln1_b[l]), ln2g=row(ln2_g[l]), ln2b=row(ln2_b[l]),
        pwq=peer_wq[l].T.astype(BF16), pkeys=peer_sub_keys[l].astype(BF16),
    )


def _trunk_layer(x, mod, lw, batch, seq_len, alpha, consts, ctx, layer):
    rope_tabs = None if ctx is None else consts["rope"]
    q, k, v, gq, gk, gv, go, la, gm = _inproj(x, mod, lw, seq_len, rope_tabs)
    if ctx is None:
        attn = _attention(q, k, v, batch, seq_len, min(seq_len, 256), None, layer)
        og, states = _gla(gq, gk, gv, la, batch, seq_len, consts["tri"], None, layer)
    else:
        attn = _attention(q, k, v, batch, seq_len, 128, ctx[:2], layer)
        og, states = _gla(gq, gk, gv, la, batch, seq_len, consts["tri"], ctx[2], layer)
    x1, ht = _postmix(x, mod, attn, og, go, gm, lw, seq_len, alpha)
    route = _peer_route(ht, lw)
    x2 = _peer_dense(ht, route, x1, mod, lw, consts["tables"], layer, seq_len, alpha)
    return x2, (k, v, states)


def kernel(x_prompt, x_sample, cache_k, cache_v, state_gla, c, c_ctx, ada_w, ada_b, w_in, q_norm, k_norm,
           gate_w2, gate_b, gla_norm, w_attn_o, w_gla_o, w_out, ln1_g, ln1_b, ln2_g, ln2_b,
           peer_wq, peer_sub_keys, peer_u, peer_v):
    depth = ada_w.shape[0]
    alpha = (2.0 * depth) ** 0.25
    batch, seq, _ = x_prompt.shape
    dec_batch, dec_seq, _ = x_sample.shape
    past = cache_k.shape[2]

    n_cond = 1 + dec_batch
    cond = jnp.concatenate([c_ctx[None], c, jnp.zeros((-n_cond % 8, D_MODEL), F32)], axis=0)
    mod = _ada_mod(cond, ada_w, ada_b).reshape(depth, cond.shape[0], 6, D_MODEL)

    idx = np.arange(GLA_CHUNK)
    tri = jnp.asarray(np.stack([idx[None, :] <= idx[:, None], idx[None, :] >= idx[:, None]]), BF16)
    consts = dict(tri=tri, rope=_rope_tables(dec_seq), tables=(peer_u.astype(BF16), peer_v.astype(BF16)))
    weights = [_layer_weights(l, w_in, q_norm, k_norm, gate_w2, gate_b, gla_norm, w_attn_o, w_gla_o, w_out,
                              ln1_g, ln1_b, ln2_g, ln2_b, peer_wq, peer_sub_keys, peer_u, peer_v)
               for l in range(depth)]

    xp = x_prompt.reshape(batch * seq, D_MODEL)
    ks, vs, ss = [], [], []
    for l in range(depth):
        xp, (k_l, v_l, s_l) = _trunk_layer(xp, mod[l, 0:1], weights[l], batch, seq, alpha, consts, None, l)
        ks.append(k_l.reshape(batch, seq, KV_HEADS, HEAD_DIM))
        vs.append(v_l.reshape(batch, seq, KV_HEADS, HEAD_DIM))
        ss.append(s_l)
    new_cache_k = jnp.stack(ks, axis=1)
    new_cache_v = jnp.stack(vs, axis=1)
    new_state = jnp.stack(ss, axis=1)

    ctx = (cache_k.reshape(dec_batch, depth, past, KV_WIDTH), cache_v.reshape(dec_batch, depth, past, KV_WIDTH),
           state_gla)
    xs = x_sample.reshape(dec_batch * dec_seq, D_MODEL)
    for l in range(depth):
        xs, _ = _trunk_layer(xs, mod[l, 1:1 + dec_batch], weights[l], dec_batch, dec_seq, alpha, consts, ctx, l)

    return (xp.reshape(batch, seq, D_MODEL), xs.reshape(dec_batch, dec_seq, D_MODEL),
            new_cache_k, new_cache_v, new_state)
```

```python
import functools

import numpy as np
import jax
import jax.numpy as jnp
from jax import lax
from jax.experimental import pallas as pl
from jax.experimental.pallas import tpu as pltpu

F32 = jnp.float32
BF16 = jnp.bfloat16

D_MODEL = 1024
HEAD_DIM = 64
N_HEADS = D_MODEL // HEAD_DIM
KV_HEADS = N_HEADS // 4
Q_PER_KV = N_HEADS // KV_HEADS
KV_WIDTH = KV_HEADS * HEAD_DIM
GRID_W = 64
ROPE_FREQS = HEAD_DIM // 4
ROPE_THETA = 10000.0
GLA_HEADS = 4
GLA_DK = D_MODEL // 2 // GLA_HEADS
GLA_DV = D_MODEL // GLA_HEADS
GLA_QK_WIDTH = GLA_HEADS * GLA_DK
GLA_V_WIDTH = GLA_HEADS * GLA_DV
GATE_RANK = 16
GATE_NORM = 16.0
GLA_CHUNK = 64
N_KEYS = 128
N_EXPERTS = N_KEYS * N_KEYS
PEER_HEADS = 8
PEER_TOPK = 16
PEER_QDIM = 256
PEER_HALF = PEER_QDIM // 2
LN_EPS = 1e-5
RMS_EPS = 1e-6

V7X_VMEM_BYTES = 64 * 1024 * 1024
VMEM_LIMIT = V7X_VMEM_BYTES - 8 * 1024 * 1024
LANES = 128

TOKEN_TILE = 256
PEER_TOKEN_TILE = 512
ROUTE_TOKEN_TILE = 512
PEER_EXPERT_TILE = 1024
PEER_CHUNK_KEYS = 4


def _dot(a, b):
    return jnp.dot(a, b, preferred_element_type=F32)


def _dot_nt(a, b):
    return lax.dot_general(a, b, (((1,), (1,)), ((), ())), preferred_element_type=F32)


def _dot_tn(a, b):
    return lax.dot_general(a, b, (((0,), (0,)), ((), ())), preferred_element_type=F32)


def _const_spec(shape):
    zeros = (0,) * len(shape)
    return pl.BlockSpec(shape, lambda *_: zeros)


def _params(*sem):
    return pltpu.CompilerParams(dimension_semantics=sem, vmem_limit_bytes=VMEM_LIMIT)


def _layer_norm(x, g, b):
    mu = jnp.mean(x, axis=-1, keepdims=True)
    xc = x - mu
    var = jnp.mean(xc * xc, axis=-1, keepdims=True)
    return xc * lax.rsqrt(var + LN_EPS) * g + b


def _ada_kernel(c_ref, w_ref, b_ref, o_ref):
    c = c_ref[...]
    s = c * jax.nn.sigmoid(c)
    o_ref[...] = _dot(s.astype(BF16), w_ref[...].astype(BF16)) + b_ref[...]


def _ada_mod(cond, ada_w, ada_b):
    depth = ada_w.shape[0]
    rows = cond.shape[0]
    return pl.pallas_call(
        _ada_kernel,
        out_shape=jax.ShapeDtypeStruct((depth, rows, 6 * D_MODEL), F32),
        grid=(depth, 6),
        in_specs=[
            pl.BlockSpec((rows, D_MODEL), lambda l, j: (0, 0)),
            pl.BlockSpec((None, D_MODEL, D_MODEL), lambda l, j: (l, 0, j)),
            pl.BlockSpec((None, 1, D_MODEL), lambda l, j: (l, 0, j)),
        ],
        out_specs=pl.BlockSpec((None, rows, D_MODEL), lambda l, j: (l, 0, j)),
        compiler_params=_params("parallel", "parallel"),
        name="ada_mod",
    )(cond, ada_w, ada_b.reshape(depth, 1, 6 * D_MODEL))


def _rope(t, cos, sin_signed):
    width = t.shape[-1]
    up = pltpu.roll(t, width - ROPE_FREQS, 1)
    dn = pltpu.roll(t, ROPE_FREQS, 1)
    lane = lax.broadcasted_iota(jnp.int32, t.shape, 1)
    partner = jnp.where((lane & ROPE_FREQS) == 0, up, dn)
    return t * cos + partner * sin_signed


def _inproj_kernel(rope, *refs):
    (x_ref, mod_ref, wq_ref, wkv_ref, wg_ref, wglr_ref, wgm_ref, mq_ref, qg_ref, kg_ref,
     w2_ref, gb_ref) = refs[:12]
    refs = refs[12:]
    if rope:
        cos_ref, sin_ref = refs[:2]
        refs = refs[2:]
    q_out, k_out, v_out, gq_out, gk_out, gv_out, go_out, la_out, gm_out = refs

    m = mod_ref[0]
    h = (x_ref[...] * (1.0 + m[1:2]) + m[0:1]).astype(BF16)

    q = _dot(h, wq_ref[...])
    qn = q * lax.rsqrt(_dot((q * q).astype(BF16), mq_ref[...]) + RMS_EPS) * qg_ref[...]
    kv = _dot(h, wkv_ref[...])
    k = kv[:, :KV_WIDTH]
    kn = k * lax.rsqrt(_dot((k * k).astype(BF16), mq_ref[:KV_WIDTH, :KV_WIDTH]) + RMS_EPS) * kg_ref[...]
    if rope:
        cos = cos_ref[...]
        sin = sin_ref[...]
        qn = _rope(qn, cos, sin)
        kn = _rope(kn, cos[:, :KV_WIDTH], sin[:, :KV_WIDTH])
    q_out[...] = qn * (HEAD_DIM ** -0.5)
    k_out[...] = kn
    v_out[...] = kv[:, KV_WIDTH:]

    g = _dot(h, wg_ref[...])
    gq_out[...] = g[:, :GLA_QK_WIDTH] * (GLA_DK ** -0.5)
    gk_out[...] = g[:, GLA_QK_WIDTH:2 * GLA_QK_WIDTH]
    gv_out[...] = g[:, 2 * GLA_QK_WIDTH:2 * GLA_QK_WIDTH + GLA_V_WIDTH]
    go_out[...] = g[:, 2 * GLA_QK_WIDTH + GLA_V_WIDTH:]

    glr = _dot(h, wglr_ref[...])
    z = _dot(glr.astype(BF16), w2_ref[...]) + gb_ref[...]
    la_out[...] = (jnp.minimum(z, 0.0) - jnp.log1p(jnp.exp(-jnp.abs(z)))) * (1.0 / GATE_NORM)
    gm_out[...] = jax.nn.sigmoid(_dot(h, wgm_ref[...]))


def _inproj(x, mod, lw, seq_len, rope_tabs):
    tokens = x.shape[0]
    tm = TOKEN_TILE
    rope = rope_tabs is not None
    row = lambda i: (i, 0)
    ins = [x, mod, lw["wq"], lw["wkv"], lw["wg"], lw["wglr"], lw["wgm"], lw["mq"], lw["qg"], lw["kg"],
           lw["w2"], lw["gb"]]
    in_specs = [pl.BlockSpec((tm, D_MODEL), row),
                pl.BlockSpec((1, 6, D_MODEL), lambda i: ((i * tm) // seq_len % mod.shape[0], 0, 0))]
    in_specs += [_const_spec(a.shape) for a in ins[2:]]
    if rope:
        per_seq = seq_len // tm
        ins += list(rope_tabs)
        in_specs += [pl.BlockSpec((tm, D_MODEL), lambda i: (i % per_seq, 0))] * 2
    widths = (D_MODEL, KV_WIDTH, KV_WIDTH, GLA_QK_WIDTH, GLA_QK_WIDTH, GLA_V_WIDTH, GLA_V_WIDTH,
              2 * GLA_QK_WIDTH, 2 * D_MODEL)
    return pl.pallas_call(
        functools.partial(_inproj_kernel, rope),
        out_shape=[jax.ShapeDtypeStruct((tokens, w), F32) for w in widths],
        grid=(tokens // tm,),
        in_specs=in_specs,
        out_specs=[pl.BlockSpec((tm, w), row) for w in widths],
        compiler_params=_params("parallel"),
        name="inproj_rope" if rope else "inproj",
    )(*ins)


def _attn_kernel(has_ctx, *refs):
    if has_ctx:
        q_ref, k_ref, v_ref, ck_ref, cv_ref, o_ref = refs
    else:
        q_ref, k_ref, v_ref, o_ref = refs
    tq = q_ref.shape[0]
    for g in range(KV_HEADS):
        gs = slice(g * HEAD_DIM, (g + 1) * HEAD_DIM)
        kg = k_ref[:, gs].astype(BF16)
        vg = v_ref[:, gs].astype(BF16)
        if has_ctx:
            kg = jnp.concatenate([kg, ck_ref[:, gs].astype(BF16)], axis=0)
            vg = jnp.concatenate([vg, cv_ref[:, gs].astype(BF16)], axis=0)
        heads = [q_ref[:, (Q_PER_KV * g + r) * HEAD_DIM:(Q_PER_KV * g + r + 1) * HEAD_DIM]
                 for r in range(Q_PER_KV)]
        qs = jnp.concatenate(heads, axis=0).astype(BF16)
        s = _dot_nt(qs, kg)
        p = jnp.exp(s - jnp.max(s, axis=-1, keepdims=True))
        o = _dot(p.astype(BF16), vg) / jnp.sum(p, axis=-1, keepdims=True)
        for r in range(Q_PER_KV):
            h0 = (Q_PER_KV * g + r) * HEAD_DIM
            o_ref[:, h0:h0 + HEAD_DIM] = o[r * tq:(r + 1) * tq]


def _attention(q, k, v, batch, seq_len, tq, ctx_kv, layer):
    tokens = q.shape[0]
    nq = seq_len // tq
    ins = [q, k, v]
    in_specs = [pl.BlockSpec((tq, D_MODEL), lambda b, i: (b * nq + i, 0)),
                pl.BlockSpec((seq_len, KV_WIDTH), lambda b, i: (b, 0)),
                pl.BlockSpec((seq_len, KV_WIDTH), lambda b, i: (b, 0))]
    if ctx_kv is not None:
        past = ctx_kv[0].shape[2]
        ins += list(ctx_kv)
        in_specs += [pl.BlockSpec((None, None, past, KV_WIDTH), lambda b, i: (b, layer, 0, 0))] * 2
    return pl.pallas_call(
        functools.partial(_attn_kernel, ctx_kv is not None),
        out_shape=jax.ShapeDtypeStruct((tokens, D_MODEL), F32),
        grid=(batch, nq),
        in_specs=in_specs,
        out_specs=pl.BlockSpec((tq, D_MODEL), lambda b, i: (b * nq + i, 0)),
        compiler_params=_params("parallel", "parallel"),
        name="attention_ctx" if ctx_kv is not None else "attention",
    )(*ins)


def _split3(x):
    hi = x.astype(BF16)
    r = x - hi.astype(F32)
    mid = r.astype(BF16)
    lo = (r - mid.astype(F32)).astype(BF16)
    return hi, mid, lo


def _gla_kernel(has_s0, *refs):
    gq_ref, gk_ref, gv_ref, la_ref, tri_ref = refs[:5]
    refs = refs[5:]
    if has_s0:
        s0_ref = refs[0]
        refs = refs[1:]
    o_ref, st_out, bf_ref, bb_ref, st_ref = refs
    seq_len = gq_ref.shape[0]
    n_chunks = seq_len // GLA_CHUNK
    tril = tri_ref[0]
    triu = tri_ref[1]

    def cumsum_chunk(c, carry):
        rows = pl.ds(pl.multiple_of(c * GLA_CHUNK, GLA_CHUNK), GLA_CHUNK)
        la = la_ref[rows, :]
        pf = _split3(la[:, :GLA_QK_WIDTH])
        pb = _split3(la[:, GLA_QK_WIDTH:])
        bf_ref[rows, :] = _dot(tril, pf[0]) + _dot(tril, pf[1]) + _dot(tril, pf[2])
        bb_ref[rows, :] = _dot(triu, pb[0]) + _dot(triu, pb[1]) + _dot(triu, pb[2])
        return carry

    lax.fori_loop(0, n_chunks, cumsum_chunk, 0)

    ri = lax.broadcasted_iota(jnp.int32, (GLA_CHUNK, GLA_CHUNK), 0)
    ci = lax.broadcasted_iota(jnp.int32, (GLA_CHUNK, GLA_CHUNK), 1)
    for d in range(2):
        for h in range(GLA_HEADS):
            if has_s0:
                st_ref[d * GLA_HEADS + h] = s0_ref[d, h].T
            else:
                st_ref[d * GLA_HEADS + h] = jnp.zeros((GLA_DV, GLA_DK), F32)
    o_ref[...] = jnp.zeros_like(o_ref)

    def step(i, carry):
        for d in range(2):
            b_ref = bf_ref if d == 0 else bb_ref
            keep = (ci <= ri) if d == 0 else (ci >= ri)
            c = i if d == 0 else n_chunks - 1 - i
            rows = pl.ds(pl.multiple_of(c * GLA_CHUNK, GLA_CHUNK), GLA_CHUNK)
            for h in range(GLA_HEADS):
                ks = slice(h * GLA_DK, (h + 1) * GLA_DK)
                vs = slice(h * GLA_DV, (h + 1) * GLA_DV)
                b = b_ref[rows, ks]
                bl = b[GLA_CHUNK - 1:GLA_CHUNK] if d == 0 else b[0:1]
                kk = gk_ref[rows, ks]
                v = gv_ref[rows, vs].astype(BF16)
                qe = (gq_ref[rows, ks] * jnp.exp(b)).astype(BF16)
                ke = (kk * jnp.exp(-b)).astype(BF16)
                kl = (kk * jnp.exp(bl - b)).astype(BF16)
                a = jnp.where(keep, _dot_nt(qe, ke), 0.0).astype(BF16)
                st = st_ref[d * GLA_HEADS + h]
                o_ref[rows, vs] += _dot(a, v) + _dot_nt(qe, st.astype(BF16))
                st_ref[d * GLA_HEADS + h] = st * jnp.exp(bl) + _dot_tn(v, kl)
        return carry

    lax.fori_loop(0, n_chunks, step, 0)
    for d in range(2):
        for h in range(GLA_HEADS):
            st_out[d, h] = st_ref[d * GLA_HEADS + h].T


def _gla(gq, gk, gv, la, batch, seq_len, tri, s0, layer):
    tokens = gq.shape[0]
    seq = lambda w: pl.BlockSpec((seq_len, w), lambda b: (b, 0))
    ins = [gq, gk, gv, la, tri]
    in_specs = [seq(GLA_QK_WIDTH), seq(GLA_QK_WIDTH), seq(GLA_V_WIDTH), seq(2 * GLA_QK_WIDTH),
                _const_spec(tri.shape)]
    if s0 is not None:
        ins.append(s0)
        in_specs.append(pl.BlockSpec((None, None, 2, GLA_HEADS, GLA_DK, GLA_DV),
                                     lambda b: (b, layer, 0, 0, 0, 0)))
    return pl.pallas_call(
        functools.partial(_gla_kernel, s0 is not None),
        out_shape=[jax.ShapeDtypeStruct((tokens, GLA_V_WIDTH), F32),
                   jax.ShapeDtypeStruct((batch, 2, GLA_HEADS, GLA_DK, GLA_DV), F32)],
        grid=(batch,),
        in_specs=in_specs,
        out_specs=[seq(GLA_V_WIDTH),
                   pl.BlockSpec((None, 2, GLA_HEADS, GLA_DK, GLA_DV), lambda b: (b, 0, 0, 0, 0))],
        scratch_shapes=[pltpu.VMEM((seq_len, GLA_QK_WIDTH), F32),
                        pltpu.VMEM((seq_len, GLA_QK_WIDTH), F32),
                        pltpu.VMEM((2 * GLA_HEADS, GLA_DV, GLA_DK), F32)],
        compiler_params=_params("parallel"),
        name="gla_s0" if s0 is not None else "gla",
    )(*ins)


def _postmix_kernel(alpha, x_ref, mod_ref, at_ref, og_ref, go_ref, gm_ref, wa_ref, wl_ref, wo_ref,
                    gn_ref, lg_ref, lb_ref, x1_ref, ht_ref):
    m = mod_ref[0]
    og = og_ref[...]
    parts = []
    for h in range(GLA_HEADS):
        oh = og[:, h * GLA_DV:(h + 1) * GLA_DV]
        parts.append(oh * lax.rsqrt(jnp.mean(oh * oh, axis=-1, keepdims=True) + RMS_EPS))
    go = go_ref[...]
    o = jnp.concatenate(parts, axis=-1) * gn_ref[...] * (go * jax.nn.sigmoid(go))
    gm = gm_ref[...]
    y = (gm[:, :D_MODEL] * _dot(at_ref[...].astype(BF16), wa_ref[...])
         + gm[:, D_MODEL:] * _dot(o.astype(BF16), wl_ref[...]))
    mix = _dot(y.astype(BF16), wo_ref[...])
    x1 = _layer_norm(alpha * x_ref[...] + m[2:3] * mix, lg_ref[...], lb_ref[...])
    x1_ref[...] = x1
    ht_ref[...] = (x1 * (1.0 + m[4:5]) + m[3:4]).T.astype(BF16)


def _postmix(x, mod, attn, og, go, gm, lw, seq_len, alpha):
    tokens = x.shape[0]
    tm = TOKEN_TILE
    row = lambda w: pl.BlockSpec((tm, w), lambda i: (i, 0))
    consts = [lw["wa"], lw["wl"], lw["wo"], lw["gn"], lw["ln1g"], lw["ln1b"]]
    return pl.pallas_call(
        functools.partial(_postmix_kernel, alpha),
        out_shape=[jax.ShapeDtypeStruct((tokens, D_MODEL), F32),
                   jax.ShapeDtypeStruct((D_MODEL, tokens), BF16)],
        grid=(tokens // tm,),
        in_specs=[row(D_MODEL),
                  pl.BlockSpec((1, 6, D_MODEL), lambda i: ((i * tm) // seq_len % mod.shape[0], 0, 0)),
                  row(D_MODEL), row(D_MODEL), row(D_MODEL), row(2 * D_MODEL)]
                 + [_const_spec(a.shape) for a in consts],
        out_specs=[row(D_MODEL), pl.BlockSpec((D_MODEL, tm), lambda i: (0, i))],
        compiler_params=_params("parallel"),
        name="postmix",
    )(x, mod, attn, og, go, gm, *consts)


def _peer_candidate_tables(lanes):
    groups = [[(0, r) for r in range(16)], [(r, 0) for r in range(16)]]
    for t in (1, 2, 3):
        groups.append([(t, r) for r in range(8)])
        if t < 3:
            groups.append([(r, t) for r in range(8)])
    seen = set()
    ci, neg = [], []
    for grp in groups:
        for (r1, r2) in grp:
            ok = (r1 + 1) * (r2 + 1) <= PEER_TOPK and (r1, r2) not in seen
            if ok:
                seen.add((r1, r2))
            ci.append(float(r1 * PEER_TOPK + r2) if ok else 1e9)
            neg.append(0.0 if ok else -np.inf)
    tab = np.stack([np.asarray(ci, np.float32), np.asarray(neg, np.float32)])
    return np.ascontiguousarray(np.broadcast_to(tab[:, :, None], tab.shape + (lanes,)))


def _extract_top(s, exact_ties):
    key = lax.broadcasted_iota(jnp.int32, s.shape, 0).astype(F32)
    slot = lax.broadcasted_iota(jnp.int32, (PEER_TOPK, s.shape[1]), 0)
    rank = jnp.full(s.shape, float(PEER_TOPK), F32)
    vals = jnp.zeros((PEER_TOPK, s.shape[1]), F32)
    for r in range(PEER_TOPK):
        m = jnp.max(s, axis=0, keepdims=True)
        hit = s == m
        if exact_ties:
            hit = key == jnp.min(jnp.where(hit, key, float(N_KEYS)), axis=0, keepdims=True)
        rank = jnp.where(hit, float(r), rank)
        s = jnp.where(hit, -jnp.inf, s)
        vals = jnp.where(slot == r, m, vals)
    ranked = jnp.sum(jnp.where(rank < float(PEER_TOPK), 1.0, 0.0), axis=0, keepdims=True)
    return vals, rank, ranked


def _candidate_counts(v1, v2, ci, neg, exact_ties):
    lo = slice(0, 8)
    cand = jnp.concatenate([
        v1[0:1] + v2, v1 + v2[0:1],
        v1[1:2] + v2[lo], v1[lo] + v2[1:2],
        v1[2:3] + v2[lo], v1[lo] + v2[2:3],
        v1[3:4] + v2[lo]], axis=0) + neg
    taken = jnp.zeros(cand.shape, F32)
    for _ in range(PEER_TOPK):
        m = jnp.max(cand, axis=0, keepdims=True)
        hit = cand == m
        if exact_ties:
            hit = ci == jnp.min(jnp.where(hit, ci, 2e9), axis=0, keepdims=True)
        taken = jnp.where(hit, 1.0, taken)
        cand = jnp.where(hit, -jnp.inf, cand)
    row_sum = lambda a, b: jnp.sum(taken[a:b], axis=0, keepdims=True)
    slot = lax.broadcasted_iota(jnp.int32, v1.shape, 0)
    counts = taken[16:32] + jnp.concatenate(
        [taken[40:48] + taken[56:64], jnp.zeros((8, v1.shape[1]), F32)], axis=0)
    counts += jnp.where(slot == 0, row_sum(0, 16), 0.0)
    counts += jnp.where(slot == 1, row_sum(32, 40), 0.0)
    counts += jnp.where(slot == 2, row_sum(48, 56), 0.0)
    counts += jnp.where(slot == 3, row_sum(64, 72), 0.0)
    return counts, jnp.sum(counts, axis=0, keepdims=True)


def _skewed_group(lane_group, n_groups):
    return (lane_group + 1) % n_groups


def _route_kernel(ht_ref, wq_ref, keys_ref, tab_ref, cnt_out, p1_out, rk_out, p2_out,
                  q_ref, v1_ref, v2_ref, rank1_ref, rank2_ref, counts_ref):
    q_ref[...] = _dot(wq_ref[...], ht_ref[...])
    ci = tab_ref[0]
    neg = tab_ref[1]

    def head(h, carry):
        r0 = pl.multiple_of(h * PEER_QDIM, PEER_QDIM)
        s1 = _dot(keys_ref[0], q_ref[pl.ds(r0, PEER_HALF), :].astype(BF16))
        s2 = _dot(keys_ref[1], q_ref[pl.ds(r0 + PEER_HALF, PEER_HALF), :].astype(BF16))

        def select(exact_ties):
            v1, rank1, n1 = _extract_top(s1, exact_ties)
            v2, rank2, n2 = _extract_top(s2, exact_ties)
            counts, n3 = _candidate_counts(v1, v2, ci, neg, exact_ties)
            v1_ref[...], v2_ref[...], counts_ref[...] = v1, v2, counts
            rank1_ref[...], rank2_ref[...] = rank1, rank2
            full = float(PEER_TOPK)
            return jnp.where((n1 == full) & (n2 == full) & (n3 == full), 0.0, 1.0)

        tied = jnp.max(select(False))

        @pl.when(tied > 0.0)
        def _():
            select(True)

        v1, v2, counts = v1_ref[...], v2_ref[...], counts_ref[...]
        rank1 = rank1_ref[...]
        used = jnp.sum(jnp.where(counts > 0.0, 1.0, 0.0), axis=0, keepdims=True)
        cnt = jnp.where(rank1 < used, 1.0, 0.0)
        for r in range(PEER_TOPK // 2):
            cnt = jnp.where(rank1 == float(r), counts[r:r + 1], cnt)
        e1 = jnp.exp(v1 - v1[0:1])
        e2 = jnp.exp(v2 - v2[0:1])
        inner = jnp.zeros(v1.shape, F32)
        for r in range(PEER_TOPK):
            inner += jnp.where(counts > float(r), e2[r:r + 1], 0.0)
        z = jnp.sum(e1 * inner, axis=0, keepdims=True)
        rows = pl.ds(pl.multiple_of(h * N_KEYS, N_KEYS), N_KEYS)
        cnt_out[rows, :] = cnt
        p1_out[rows, :] = jnp.exp(s1 - v1[0:1])
        half = pl.ds(pl.multiple_of(h * (N_KEYS // 2), N_KEYS // 2), N_KEYS // 2)
        rk_out[half, :] = pltpu.bitcast(rank2_ref[...].astype(BF16), jnp.uint32)
        p2 = pltpu.bitcast((jnp.exp(s2 - v2[0:1]) / z).astype(BF16), jnp.uint32)
        n_groups = p2.shape[1] // LANES
        for lg in range(n_groups):
            dst = _skewed_group(lg, n_groups) * LANES
            p2_out[half, dst:dst + LANES] = p2[:, lg * LANES:(lg + 1) * LANES]
        return carry

    lax.fori_loop(0, PEER_HEADS, head, 0)


def _peer_route(ht, lw):
    tokens = ht.shape[1]
    rt = min(ROUTE_TOKEN_TILE, tokens)
    tab = jnp.asarray(_peer_candidate_tables(rt))
    n_rows = PEER_HEADS * N_KEYS
    dense = lambda dt, rows: jax.ShapeDtypeStruct((rows, tokens), dt)
    out_spec = lambda rows: pl.BlockSpec((rows, rt), lambda i: (0, i))
    return pl.pallas_call(
        _route_kernel,
        out_shape=[dense(F32, n_rows), dense(F32, n_rows),
                   dense(jnp.uint32, n_rows // 2), dense(jnp.uint32, n_rows // 2)],
        grid=(tokens // rt,),
        in_specs=[pl.BlockSpec((D_MODEL, rt), lambda i: (0, i)),
                  _const_spec(lw["pwq"].shape), _const_spec(lw["pkeys"].shape), _const_spec(tab.shape)],
        out_specs=[out_spec(n_rows), out_spec(n_rows), out_spec(n_rows // 2), out_spec(n_rows // 2)],
        scratch_shapes=[pltpu.VMEM((PEER_HEADS * PEER_QDIM, rt), F32),
                        pltpu.VMEM((PEER_TOPK, rt), F32), pltpu.VMEM((PEER_TOPK, rt), F32),
                        pltpu.VMEM((N_KEYS, rt), F32), pltpu.VMEM((N_KEYS, rt), F32),
                        pltpu.VMEM((PEER_TOPK, rt), F32)],
        compiler_params=_params("parallel"),
        name="peer_route",
    )(ht, lw["pwq"], lw["pkeys"], tab)


def _gelu_tanh(x):
    k1 = -2.0 * 0.7978845608028654 * 1.4426950408889634
    e = jnp.exp2((x * (k1 + (k1 * 0.044715) * (x * x))).astype(BF16))
    return x.astype(BF16) / (1.0 + e)


def _peer_kernel(alpha, ht_ref, u_ref, vt_ref, cnt_ref, p1_ref, rk_ref, p2_ref, x1_ref, mod_ref,
                 lg_ref, lb_ref, o_ref, acc_ref, a_ref, g_ref):
    j = pl.program_id(1)
    tb = ht_ref.shape[1]
    first_keys = u_ref.shape[0] // N_KEYS

    @pl.when(j == 0)
    def _():
        acc_ref[...] = jnp.zeros_like(acc_ref)

    head_rows = lambda h: pl.ds(pl.multiple_of(h * N_KEYS + j * first_keys, first_keys), first_keys)
    ht = ht_ref[...]
    n_chunks = first_keys // PEER_CHUNK_KEYS
    chunk_rows = lambda c: slice(c * PEER_CHUNK_KEYS * N_KEYS, (c + 1) * PEER_CHUNK_KEYS * N_KEYS)

    def activations(c):
        a_ref[chunk_rows(c), :] = _dot(u_ref[chunk_rows(c), :], ht)

    activations(0)
    for c in range(n_chunks):
        rows = chunk_rows(c)
        if c + 1 < n_chunks:
            activations(c + 1)
        for a in range(c * PEER_CHUNK_KEYS, (c + 1) * PEER_CHUNK_KEYS):
            for lg in range(tb // LANES):
                ls = slice(lg * LANES, (lg + 1) * LANES)
                lg2 = _skewed_group(lg, tb // LANES)
                ls2 = slice(lg2 * LANES, (lg2 + 1) * LANES)
                row_a = lambda ref, h: jnp.broadcast_to(ref[head_rows(h), ls][a:a + 1], (16, LANES)).astype(BF16)
                groups = range(N_KEYS // 16)
                w = [jnp.zeros((16, LANES), BF16) for _ in groups]
                for h in range(PEER_HEADS):
                    cb = row_a(cnt_ref, h)
                    pb = row_a(p1_ref, h)
                    for g in groups:
                        ks = slice(h * (N_KEYS // 2) + g * 8, h * (N_KEYS // 2) + (g + 1) * 8)
                        rk = pltpu.bitcast(rk_ref[ks, ls], BF16)
                        p2 = pltpu.bitcast(p2_ref[ks, ls2], BF16)
                        w[g] += jnp.where(rk < cb, p2, 0.0) * pb
                for g in groups:
                    er = slice(a * N_KEYS + g * 16, a * N_KEYS + (g + 1) * 16)
                    g_ref[er, ls] = w[g] * _gelu_tanh(a_ref[er, ls])
        acc_ref[...] += _dot_tn(vt_ref[rows, :], g_ref[rows, :])

    @pl.when(j == pl.num_programs(1) - 1)
    def _():
        m = mod_ref[0]
        o_ref[...] = _layer_norm(alpha * x1_ref[...] + m[5:6] * acc_ref[...].T, lg_ref[...], lb_ref[...])


def _peer_dense(ht, route, x1, mod, lw, tables, layer, seq_len, alpha):
    tokens = x1.shape[0]
    tb = min(PEER_TOKEN_TILE, tokens)
    assert tb == min(ROUTE_TOKEN_TILE, tokens), "the p2 lane-group skew is per routing block"
    et = PEER_EXPERT_TILE
    assert (et // N_KEYS) % 8 == 0, "whole sublane tiles of first-key rows per expert tile"
    dense = pl.BlockSpec((PEER_HEADS * N_KEYS, tb), lambda i, j: (0, i))
    packed = pl.BlockSpec((PEER_HEADS * N_KEYS // 2, tb), lambda i, j: (0, i))
    return pl.pallas_call(
        functools.partial(_peer_kernel, alpha),
        out_shape=jax.ShapeDtypeStruct((tokens, D_MODEL), F32),
        grid=(tokens // tb, N_EXPERTS // et),
        in_specs=[pl.BlockSpec((D_MODEL, tb), lambda i, j: (0, i)),
                  pl.BlockSpec((None, et, D_MODEL), lambda i, j: (layer, j, 0)),
                  pl.BlockSpec((None, et, D_MODEL), lambda i, j: (layer, j, 0)),
                  dense, dense, packed, packed,
                  pl.BlockSpec((tb, D_MODEL), lambda i, j: (i, 0)),
                  pl.BlockSpec((1, 6, D_MODEL), lambda i, j: ((i * tb) // seq_len % mod.shape[0], 0, 0)),
                  _const_spec(lw["ln2g"].shape), _const_spec(lw["ln2b"].shape)],
        out_specs=pl.BlockSpec((tb, D_MODEL), lambda i, j: (i, 0)),
        scratch_shapes=[pltpu.VMEM((D_MODEL, tb), F32),
                        pltpu.VMEM((et, tb), F32),
                        pltpu.VMEM((et, tb), BF16)],
        compiler_params=_params("parallel", "arbitrary"),
        name="peer_dense",
    )(ht, *tables, *route, x1, mod, lw["ln2g"], lw["ln2b"])


def _rope_tables(seq_len):
    rows = seq_len // GRID_W
    r = jnp.repeat(jnp.arange(rows, dtype=F32), GRID_W)
    col = jnp.tile(jnp.arange(GRID_W, dtype=F32), rows)
    inv = ROPE_THETA ** (-jnp.arange(ROPE_FREQS, dtype=F32) / ROPE_FREQS)
    ang = jnp.stack([r[:, None] * inv, col[:, None] * inv], axis=1)
    cos, sin = jnp.cos(ang), jnp.sin(ang)
    cos_h = jnp.concatenate([cos, cos], axis=-1).reshape(seq_len, HEAD_DIM)
    sin_h = jnp.concatenate([-sin, sin], axis=-1).reshape(seq_len, HEAD_DIM)
    return jnp.tile(cos_h, (1, N_HEADS)), jnp.tile(sin_h, (1, N_HEADS))


def _layer_weights(l, w_in, q_norm, k_norm, gate_w2, gate_b, gla_norm, w_attn_o, w_gla_o, w_out,
                   ln1_g, ln1_b, ln2_g, ln2_b, peer_wq, peer_sub_keys, peer_u, peer_v):
    w = w_in[l].astype(BF16)
    o_q, o_k, o_g, o_lr, o_gm = 0, D_MODEL, D_MODEL + 2 * KV_WIDTH, 0, 0
    o_lr = o_g + 2 * GLA_QK_WIDTH + 2 * GLA_V_WIDTH
    o_gm = o_lr + 2 * GATE_RANK
    head_id = np.arange(D_MODEL) // HEAD_DIM
    mq = jnp.asarray((head_id[:, None] == head_id[None, :]).astype(np.float32) / HEAD_DIM, BF16)
    w2 = jnp.zeros((LANES, 2 * GLA_QK_WIDTH), F32)
    w2 = w2.at[:GATE_RANK, :GLA_QK_WIDTH].set(gate_w2[l, 0])
    w2 = w2.at[GATE_RANK:2 * GATE_RANK, GLA_QK_WIDTH:].set(gate_w2[l, 1])
    row = lambda a: a.reshape(1, -1)
    return dict(
        wq=w[:, o_q:o_k], wkv=w[:, o_k:o_g], wg=w[:, o_g:o_lr],
        wglr=jnp.pad(w[:, o_lr:o_gm], ((0, 0), (0, LANES - 2 * GATE_RANK))),
        wgm=w[:, o_gm:], mq=mq,
        qg=row(jnp.tile(q_norm[l], N_HEADS)), kg=row(jnp.tile(k_norm[l], KV_HEADS)),
        w2=w2.astype(BF16), gb=row(gate_b[l]),
        wa=w_attn_o[l].astype(BF16), wl=w_gla_o[l].astype(BF16), wo=w_out[l].astype(BF16),
        gn=row(jnp.tile(gla_norm[l], GLA_HEADS)),
        ln1g=row(ln1_g[l]), ln1b=row(ln1_b[l]), ln2g=row(ln2_g[l]), ln2b=row(ln2_b[l]),
        pwq=peer_wq[l].T.astype(BF16), pkeys=peer_sub_keys[l].astype(BF16),
    )


def _trunk_layer(x, mod, lw, batch, seq_len, alpha, consts, ctx, layer):
    rope_tabs = None if ctx is None else consts["rope"]
    q, k, v, gq, gk, gv, go, la, gm = _inproj(x, mod, lw, seq_len, rope_tabs)
    if ctx is None:
        attn = _attention(q, k, v, batch, seq_len, min(seq_len, 256), None, layer)
        og, states = _gla(gq, gk, gv, la, batch, seq_len, consts["tri"], None, layer)
    else:
        attn = _attention(q, k, v, batch, seq_len, 128, ctx[:2], layer)
        og, states = _gla(gq, gk, gv, la, batch, seq_len, consts["tri"], ctx[2], layer)
    x1, ht = _postmix(x, mod, attn, og, go, gm, lw, seq_len, alpha)
    route = _peer_route(ht, lw)
    x2 = _peer_dense(ht, route, x1, mod, lw, consts["tables"], layer, seq_len, alpha)
    return x2, (k, v, states)


def kernel(x_prompt, x_sample, cache_k, cache_v, state_gla, c, c_ctx, ada_w, ada_b, w_in, q_norm, k_norm,
           gate_w2, gate_b, gla_norm, w_attn_o, w_gla_o, w_out, ln1_g, ln1_b, ln2_g, ln2_b,
           peer_wq, peer_sub_keys, peer_u, peer_v):
    depth = ada_w.shape[0]
    alpha = (2.0 * depth) ** 0.25
    batch, seq, _ = x_prompt.shape
    dec_batch, dec_seq, _ = x_sample.shape
    past = cache_k.shape[2]

    n_cond = 1 + dec_batch
    cond = jnp.concatenate([c_ctx[None], c, jnp.zeros((-n_cond % 8, D_MODEL), F32)], axis=0)
    mod = _ada_mod(cond, ada_w, ada_b).reshape(depth, cond.shape[0], 6, D_MODEL)

    idx = np.arange(GLA_CHUNK)
    tri = jnp.asarray(np.stack([idx[None, :] <= idx[:, None], idx[None, :] >= idx[:, None]]), BF16)
    consts = dict(tri=tri, rope=_rope_tables(dec_seq), tables=(peer_u.astype(BF16), peer_v.astype(BF16)))
    weights = [_layer_weights(l, w_in, q_norm, k_norm, gate_w2, gate_b, gla_norm, w_attn_o, w_gla_o, w_out,
                              ln1_g, ln1_b, ln2_g, ln2_b, peer_wq, peer_sub_keys, peer_u, peer_v)
               for l in range(depth)]

    xp = x_prompt.reshape(batch * seq, D_MODEL)
    ks, vs, ss = [], [], []
    for l in range(depth):
        xp, (k_l, v_l, s_l) = _trunk_layer(xp, mod[l, 0:1], weights[l], batch, seq, alpha, consts, None, l)
        ks.append(k_l.reshape(batch, seq, KV_HEADS, HEAD_DIM))
        vs.append(v_l.reshape(batch, seq, KV_HEADS, HEAD_DIM))
        ss.append(s_l)
    new_cache_k = jnp.stack(ks, axis=1)
    new_cache_v = jnp.stack(vs, axis=1)
    new_state = jnp.stack(ss, axis=1)

    ctx = (cache_k.reshape(dec_batch, depth, past, KV_WIDTH), cache_v.reshape(dec_batch, depth, past, KV_WIDTH),
           state_gla)
    xs = x_sample.reshape(dec_batch * dec_seq, D_MODEL)
    for l in range(depth):
        xs, _ = _trunk_layer(xs, mod[l, 1:1 + dec_batch], weights[l], dec_batch, dec_seq, alpha, consts, ctx, l)

    return (xp.reshape(batch, seq, D_MODEL), xs.reshape(dec_batch, dec_seq, D_MODEL),
            new_cache_k, new_cache_v, new_state)
```

```python
import functools

import numpy as np
import jax
import jax.numpy as jnp
from jax import lax
from jax.experimental import pallas as pl
from jax.experimental.pallas import tpu as pltpu

F32 = jnp.float32
BF16 = jnp.bfloat16

D_MODEL = 1024
HEAD_DIM = 64
N_HEADS = D_MODEL // HEAD_DIM
KV_HEADS = N_HEADS // 4
Q_PER_KV = N_HEADS // KV_HEADS
KV_WIDTH = KV_HEADS * HEAD_DIM
GRID_W = 64
ROPE_FREQS = HEAD_DIM // 4
ROPE_THETA = 10000.0
GLA_HEADS = 4
GLA_DK = D_MODEL // 2 // GLA_HEADS
GLA_DV = D_MODEL // GLA_HEADS
GLA_QK_WIDTH = GLA_HEADS * GLA_DK
GLA_V_WIDTH = GLA_HEADS * GLA_DV
GATE_RANK = 16
GATE_NORM = 16.0
GLA_CHUNK = 64
N_KEYS = 128
N_EXPERTS = N_KEYS * N_KEYS
PEER_HEADS = 8
PEER_TOPK = 16
PEER_QDIM = 256
PEER_HALF = PEER_QDIM // 2
LN_EPS = 1e-5
RMS_EPS = 1e-6

V7X_VMEM_BYTES = 64 * 1024 * 1024
VMEM_LIMIT = V7X_VMEM_BYTES - 8 * 1024 * 1024
LANES = 128

TOKEN_TILE = 256
PEER_TOKEN_TILE = 512
ROUTE_TOKEN_TILE = 512
PEER_EXPERT_TILE = 1024
PEER_CHUNK_KEYS = 4


def _dot(a, b):
    return jnp.dot(a, b, preferred_element_type=F32)


def _dot_nt(a, b):
    return lax.dot_general(a, b, (((1,), (1,)), ((), ())), preferred_element_type=F32)


def _dot_tn(a, b):
    return lax.dot_general(a, b, (((0,), (0,)), ((), ())), preferred_element_type=F32)


def _const_spec(shape):
    zeros = (0,) * len(shape)
    return pl.BlockSpec(shape, lambda *_: zeros)


def _params(*sem):
    return pltpu.CompilerParams(dimension_semantics=sem, vmem_limit_bytes=VMEM_LIMIT)


def _layer_norm(x, g, b):
    mu = jnp.mean(x, axis=-1, keepdims=True)
    xc = x - mu
    var = jnp.mean(xc * xc, axis=-1, keepdims=True)
    return xc * lax.rsqrt(var + LN_EPS) * g + b


def _ada_kernel(c_ref, w_ref, b_ref, o_ref):
    c = c_ref[...]
    s = c * jax.nn.sigmoid(c)
    o_ref[...] = _dot(s.astype(BF16), w_ref[...].astype(BF16)) + b_ref[...]


def _ada_mod(cond, ada_w, ada_b):
    depth = ada_w.shape[0]
    rows = cond.shape[0]
    return pl.pallas_call(
        _ada_kernel,
        out_shape=jax.ShapeDtypeStruct((depth, rows, 6 * D_MODEL), F32),
        grid=(depth, 6),
        in_specs=[
            pl.BlockSpec((rows, D_MODEL), lambda l, j: (0, 0)),
            pl.BlockSpec((None, D_MODEL, D_MODEL), lambda l, j: (l, 0, j)),
            pl.BlockSpec((None, 1, D_MODEL), lambda l, j: (l, 0, j)),
        ],
        out_specs=pl.BlockSpec((None, rows, D_MODEL), lambda l, j: (l, 0, j)),
        compiler_params=_params("parallel", "parallel"),
        name="ada_mod",
    )(cond, ada_w, ada_b.reshape(depth, 1, 6 * D_MODEL))


def _rope(t, cos, sin_signed):
    width = t.shape[-1]
    up = pltpu.roll(t, width - ROPE_FREQS, 1)
    dn = pltpu.roll(t, ROPE_FREQS, 1)
    lane = lax.broadcasted_iota(jnp.int32, t.shape, 1)
    partner = jnp.where((lane & ROPE_FREQS) == 0, up, dn)
    return t * cos + partner * sin_signed


def _inproj_kernel(rope, *refs):
    (x_ref, mod_ref, wq_ref, wkv_ref, wg_ref, wglr_ref, wgm_ref, mq_ref, qg_ref, kg_ref,
     w2_ref, gb_ref) = refs[:12]
    refs = refs[12:]
    if rope:
        cos_ref, sin_ref = refs[:2]
        refs = refs[2:]
    q_out, k_out, v_out, gq_out, gk_out, gv_out, go_out, la_out, gm_out = refs

    m = mod_ref[0]
    h = (x_ref[...] * (1.0 + m[1:2]) + m[0:1]).astype(BF16)

    q = _dot(h, wq_ref[...])
    qn = q * lax.rsqrt(_dot((q * q).astype(BF16), mq_ref[...]) + RMS_EPS) * qg_ref[...]
    kv = _dot(h, wkv_ref[...])
    k = kv[:, :KV_WIDTH]
    kn = k * lax.rsqrt(_dot((k * k).astype(BF16), mq_ref[:KV_WIDTH, :KV_WIDTH]) + RMS_EPS) * kg_ref[...]
    if rope:
        cos = cos_ref[...]
        sin = sin_ref[...]
        qn = _rope(qn, cos, sin)
        kn = _rope(kn, cos[:, :KV_WIDTH], sin[:, :KV_WIDTH])
    q_out[...] = qn * (HEAD_DIM ** -0.5)
    k_out[...] = kn
    v_out[...] = kv[:, KV_WIDTH:]

    g = _dot(h, wg_ref[...])
    gq_out[...] = g[:, :GLA_QK_WIDTH] * (GLA_DK ** -0.5)
    gk_out[...] = g[:, GLA_QK_WIDTH:2 * GLA_QK_WIDTH]
    gv_out[...] = g[:, 2 * GLA_QK_WIDTH:2 * GLA_QK_WIDTH + GLA_V_WIDTH]
    go_out[...] = g[:, 2 * GLA_QK_WIDTH + GLA_V_WIDTH:]

    glr = _dot(h, wglr_ref[...])
    z = _dot(glr.astype(BF16), w2_ref[...]) + gb_ref[...]
    la_out[...] = (jnp.minimum(z, 0.0) - jnp.log1p(jnp.exp(-jnp.abs(z)))) * (1.0 / GATE_NORM)
    gm_out[...] = jax.nn.sigmoid(_dot(h, wgm_ref[...]))


def _inproj(x, mod, lw, seq_len, rope_tabs):
    tokens = x.shape[0]
    tm = TOKEN_TILE
    rope = rope_tabs is not None
    row = lambda i: (i, 0)
    ins = [x, mod, lw["wq"], lw["wkv"], lw["wg"], lw["wglr"], lw["wgm"], lw["mq"], lw["qg"], lw["kg"],
           lw["w2"], lw["gb"]]
    in_specs = [pl.BlockSpec((tm, D_MODEL), row),
                pl.BlockSpec((1, 6, D_MODEL), lambda i: ((i * tm) // seq_len % mod.shape[0], 0, 0))]
    in_specs += [_const_spec(a.shape) for a in ins[2:]]
    if rope:
        per_seq = seq_len // tm
        ins += list(rope_tabs)
        in_specs += [pl.BlockSpec((tm, D_MODEL), lambda i: (i % per_seq, 0))] * 2
    widths = (D_MODEL, KV_WIDTH, KV_WIDTH, GLA_QK_WIDTH, GLA_QK_WIDTH, GLA_V_WIDTH, GLA_V_WIDTH,
              2 * GLA_QK_WIDTH, 2 * D_MODEL)
    return pl.pallas_call(
        functools.partial(_inproj_kernel, rope),
        out_shape=[jax.ShapeDtypeStruct((tokens, w), F32) for w in widths],
        grid=(tokens // tm,),
        in_specs=in_specs,
        out_specs=[pl.BlockSpec((tm, w), row) for w in widths],
        compiler_params=_params("parallel"),
        name="inproj_rope" if rope else "inproj",
    )(*ins)


def _attn_kernel(has_ctx, *refs):
    if has_ctx:
        q_ref, k_ref, v_ref, ck_ref, cv_ref, o_ref = refs
    else:
        q_ref, k_ref, v_ref, o_ref = refs
    tq = q_ref.shape[0]
    for g in range(KV_HEADS):
        gs = slice(g * HEAD_DIM, (g + 1) * HEAD_DIM)
        kg = k_ref[:, gs].astype(BF16)
        vg = v_ref[:, gs].astype(BF16)
        if has_ctx:
            kg = jnp.concatenate([kg, ck_ref[:, gs].astype(BF16)], axis=0)
            vg = jnp.concatenate([vg, cv_ref[:, gs].astype(BF16)], axis=0)
        heads = [q_ref[:, (Q_PER_KV * g + r) * HEAD_DIM:(Q_PER_KV * g + r + 1) * HEAD_DIM]
                 for r in range(Q_PER_KV)]
        qs = jnp.concatenate(heads, axis=0).astype(BF16)
        s = _dot_nt(qs, kg)
        p = jnp.exp(s - jnp.max(s, axis=-1, keepdims=True))
        o = _dot(p.astype(BF16), vg) / jnp.sum(p, axis=-1, keepdims=True)
        for r in range(Q_PER_KV):
            h0 = (Q_PER_KV * g + r) * HEAD_DIM
            o_ref[:, h0:h0 + HEAD_DIM] = o[r * tq:(r + 1) * tq]


def _attention(q, k, v, batch, seq_len, tq, ctx_kv, layer):
    tokens = q.shape[0]
    nq = seq_len // tq
    ins = [q, k, v]
    in_specs = [pl.BlockSpec((tq, D_MODEL), lambda b, i: (b * nq + i, 0)),
                pl.BlockSpec((seq_len, KV_WIDTH), lambda b, i: (b, 0)),
                pl.BlockSpec((seq_len, KV_WIDTH), lambda b, i: (b, 0))]
    if ctx_kv is not None:
        past = ctx_kv[0].shape[2]
        ins += list(ctx_kv)
        in_specs += [pl.BlockSpec((None, None, past, KV_WIDTH), lambda b, i: (b, layer, 0, 0))] * 2
    return pl.pallas_call(
        functools.partial(_attn_kernel, ctx_kv is not None),
        out_shape=jax.ShapeDtypeStruct((tokens, D_MODEL), F32),
        grid=(batch, nq),
        in_specs=in_specs,
        out_specs=pl.BlockSpec((tq, D_MODEL), lambda b, i: (b * nq + i, 0)),
        compiler_params=_params("parallel", "parallel"),
        name="attention_ctx" if ctx_kv is not None else "attention",
    )(*ins)


def _split3(x):
    hi = x.astype(BF16)
    r = x - hi.astype(F32)
    mid = r.astype(BF16)
    lo = (r - mid.astype(F32)).astype(BF16)
    return hi, mid, lo


def _gla_kernel(has_s0, *refs):
    gq_ref, gk_ref, gv_ref, la_ref, tri_ref = refs[:5]
    refs = refs[5:]
    if has_s0:
        s0_ref = refs[0]
        refs = refs[1:]
    o_ref, st_out, bf_ref, bb_ref, st_ref = refs
    seq_len = gq_ref.shape[0]
    n_chunks = seq_len // GLA_CHUNK
    tril = tri_ref[0]
    triu = tri_ref[1]

    def cumsum_chunk(c, carry):
        rows = pl.ds(pl.multiple_of(c * GLA_CHUNK, GLA_CHUNK), GLA_CHUNK)
        la = la_ref[rows, :]
        pf = _split3(la[:, :GLA_QK_WIDTH])
        pb = _split3(la[:, GLA_QK_WIDTH:])
        bf_ref[rows, :] = _dot(tril, pf[0]) + _dot(tril, pf[1]) + _dot(tril, pf[2])
        bb_ref[rows, :] = _dot(triu, pb[0]) + _dot(triu, pb[1]) + _dot(triu, pb[2])
        return carry

    lax.fori_loop(0, n_chunks, cumsum_chunk, 0)

    ri = lax.broadcasted_iota(jnp.int32, (GLA_CHUNK, GLA_CHUNK), 0)
    ci = lax.broadcasted_iota(jnp.int32, (GLA_CHUNK, GLA_CHUNK), 1)
    for d in range(2):
        for h in range(GLA_HEADS):
            if has_s0:
                st_ref[d * GLA_HEADS + h] = s0_ref[d, h].T
            else:
                st_ref[d * GLA_HEADS + h] = jnp.zeros((GLA_DV, GLA_DK), F32)
    o_ref[...] = jnp.zeros_like(o_ref)

    def step(i, carry):
        for d in range(2):
            b_ref = bf_ref if d == 0 else bb_ref
            keep = (ci <= ri) if d == 0 else (ci >= ri)
            c = i if d == 0 else n_chunks - 1 - i
            rows = pl.ds(pl.multiple_of(c * GLA_CHUNK, GLA_CHUNK), GLA_CHUNK)
            for h in range(GLA_HEADS):
                ks = slice(h * GLA_DK, (h + 1) * GLA_DK)
                vs = slice(h * GLA_DV, (h + 1) * GLA_DV)
                b = b_ref[rows, ks]
                bl = b[GLA_CHUNK - 1:GLA_CHUNK] if d == 0 else b[0:1]
                kk = gk_ref[rows, ks]
                v = gv_ref[rows, vs].astype(BF16)
                qe = (gq_ref[rows, ks] * jnp.exp(b)).astype(BF16)
                ke = (kk * jnp.exp(-b)).astype(BF16)
                kl = (kk * jnp.exp(bl - b)).astype(BF16)
                a = jnp.where(keep, _dot_nt(qe, ke), 0.0).astype(BF16)
                st = st_ref[d * GLA_HEADS + h]
                o_ref[rows, vs] += _dot(a, v) + _dot_nt(qe, st.astype(BF16))
                st_ref[d * GLA_HEADS + h] = st * jnp.exp(bl) + _dot_tn(v, kl)
        return carry

    lax.fori_loop(0, n_chunks, step, 0)
    for d in range(2):
        for h in range(GLA_HEADS):
            st_out[d, h] = st_ref[d * GLA_HEADS + h].T


def _gla(gq, gk, gv, la, batch, seq_len, tri, s0, layer):
    tokens = gq.shape[0]
    seq = lambda w: pl.BlockSpec((seq_len, w), lambda b: (b, 0))
    ins = [gq, gk, gv, la, tri]
    in_specs = [seq(GLA_QK_WIDTH), seq(GLA_QK_WIDTH), seq(GLA_V_WIDTH), seq(2 * GLA_QK_WIDTH),
                _const_spec(tri.shape)]
    if s0 is not None:
        ins.append(s0)
        in_specs.append(pl.BlockSpec((None, None, 2, GLA_HEADS, GLA_DK, GLA_DV),
                                     lambda b: (b, layer, 0, 0, 0, 0)))
    return pl.pallas_call(
        functools.partial(_gla_kernel, s0 is not None),
        out_shape=[jax.ShapeDtypeStruct((tokens, GLA_V_WIDTH), F32),
                   jax.ShapeDtypeStruct((batch, 2, GLA_HEADS, GLA_DK, GLA_DV), F32)],
        grid=(batch,),
        in_specs=in_specs,
        out_specs=[seq(GLA_V_WIDTH),
                   pl.BlockSpec((None, 2, GLA_HEADS, GLA_DK, GLA_DV), lambda b: (b, 0, 0, 0, 0))],
        scratch_shapes=[pltpu.VMEM((seq_len, GLA_QK_WIDTH), F32),
                        pltpu.VMEM((seq_len, GLA_QK_WIDTH), F32),
                        pltpu.VMEM((2 * GLA_HEADS, GLA_DV, GLA_DK), F32)],
        compiler_params=_params("parallel"),
        name="gla_s0" if s0 is not None else "gla",
    )(*ins)


def _postmix_kernel(alpha, x_ref, mod_ref, at_ref, og_ref, go_ref, gm_ref, wa_ref, wl_ref, wo_ref,
                    gn_ref, lg_ref, lb_ref, x1_ref, ht_ref):
    m = mod_ref[0]
    og = og_ref[...]
    parts = []
    for h in range(GLA_HEADS):
        oh = og[:, h * GLA_DV:(h + 1) * GLA_DV]
        parts.append(oh * lax.rsqrt(jnp.mean(oh * oh, axis=-1, keepdims=True) + RMS_EPS))
    go = go_ref[...]
    o = jnp.concatenate(parts, axis=-1) * gn_ref[...] * (go * jax.nn.sigmoid(go))
    gm = gm_ref[...]
    y = (gm[:, :D_MODEL] * _dot(at_ref[...].astype(BF16), wa_ref[...])
         + gm[:, D_MODEL:] * _dot(o.astype(BF16), wl_ref[...]))
    mix = _dot(y.astype(BF16), wo_ref[...])
    x1 = _layer_norm(alpha * x_ref[...] + m[2:3] * mix, lg_ref[...], lb_ref[...])
    x1_ref[...] = x1
    ht_ref[...] = (x1 * (1.0 + m[4:5]) + m[3:4]).T.astype(BF16)


def _postmix(x, mod, attn, og, go, gm, lw, seq_len, alpha):
    tokens = x.shape[0]
    tm = TOKEN_TILE
    row = lambda w: pl.BlockSpec((tm, w), lambda i: (i, 0))
    consts = [lw["wa"], lw["wl"], lw["wo"], lw["gn"], lw["ln1g"], lw["ln1b"]]
    return pl.pallas_call(
        functools.partial(_postmix_kernel, alpha),
        out_shape=[jax.ShapeDtypeStruct((tokens, D_MODEL), F32),
                   jax.ShapeDtypeStruct((D_MODEL, tokens), BF16)],
        grid=(tokens // tm,),
        in_specs=[row(D_MODEL),
                  pl.BlockSpec((1, 6, D_MODEL), lambda i: ((i * tm) // seq_len % mod.shape[0], 0, 0)),
                  row(D_MODEL), row(D_MODEL), row(D_MODEL), row(2 * D_MODEL)]
                 + [_const_spec(a.shape) for a in consts],
        out_specs=[row(D_MODEL), pl.BlockSpec((D_MODEL, tm), lambda i: (0, i))],
        compiler_params=_params("parallel"),
        name="postmix",
    )(x, mod, attn, og, go, gm, *consts)


def _peer_candidate_tables(lanes):
    groups = [[(0, r) for r in range(16)], [(r, 0) for r in range(16)]]
    for t in (1, 2, 3):
        groups.append([(t, r) for r in range(8)])
        if t < 3:
            groups.append([(r, t) for r in range(8)])
    seen = set()
    ci, neg = [], []
    for grp in groups:
        for (r1, r2) in grp:
            ok = (r1 + 1) * (r2 + 1) <= PEER_TOPK and (r1, r2) not in seen
            if ok:
                seen.add((r1, r2))
            ci.append(float(r1 * PEER_TOPK + r2) if ok else 1e9)
            neg.append(0.0 if ok else -np.inf)
    tab = np.stack([np.asarray(ci, np.float32), np.asarray(neg, np.float32)])
    return np.ascontiguousarray(np.broadcast_to(tab[:, :, None], tab.shape + (lanes,)))


def _extract_top(s, exact_ties):
    key = lax.broadcasted_iota(jnp.int32, s.shape, 0).astype(F32)
    slot = lax.broadcasted_iota(jnp.int32, (PEER_TOPK, s.shape[1]), 0)
    rank = jnp.full(s.shape, float(PEER_TOPK), F32)
    vals = jnp.zeros((PEER_TOPK, s.shape[1]), F32)
    for r in range(PEER_TOPK):
        m = jnp.max(s, axis=0, keepdims=True)
        hit = s == m
        if exact_ties:
            hit = key == jnp.min(jnp.where(hit, key, float(N_KEYS)), axis=0, keepdims=True)
        rank = jnp.where(hit, float(r), rank)
        s = jnp.where(hit, -jnp.inf, s)
        vals = jnp.where(slot == r, m, vals)
    ranked = jnp.sum(jnp.where(rank < float(PEER_TOPK), 1.0, 0.0), axis=0, keepdims=True)
    return vals, rank, ranked


def _candidate_counts(v1, v2, ci, neg, exact_ties):
    lo = slice(0, 8)
    cand = jnp.concatenate([
        v1[0:1] + v2, v1 + v2[0:1],
        v1[1:2] + v2[lo], v1[lo] + v2[1:2],
        v1[2:3] + v2[lo], v1[lo] + v2[2:3],
        v1[3:4] + v2[lo]], axis=0) + neg
    taken = jnp.zeros(cand.shape, F32)
    for _ in range(PEER_TOPK):
        m = jnp.max(cand, axis=0, keepdims=True)
        hit = cand == m
        if exact_ties:
            hit = ci == jnp.min(jnp.where(hit, ci, 2e9), axis=0, keepdims=True)
        taken = jnp.where(hit, 1.0, taken)
        cand = jnp.where(hit, -jnp.inf, cand)
    row_sum = lambda a, b: jnp.sum(taken[a:b], axis=0, keepdims=True)
    slot = lax.broadcasted_iota(jnp.int32, v1.shape, 0)
    counts = taken[16:32] + jnp.concatenate(
        [taken[40:48] + taken[56:64], jnp.zeros((8, v1.shape[1]), F32)], axis=0)
    counts += jnp.where(slot == 0, row_sum(0, 16), 0.0)
    counts += jnp.where(slot == 1, row_sum(32, 40), 0.0)
    counts += jnp.where(slot == 2, row_sum(48, 56), 0.0)
    counts += jnp.where(slot == 3, row_sum(64, 72), 0.0)
    return counts, jnp.sum(counts, axis=0, keepdims=True)


def _skewed_group(lane_group, n_groups):
    return (lane_group + 1) % n_groups


def _route_kernel(ht_ref, wq_ref, keys_ref, tab_ref, cnt_out, p1_out, rk_out, p2_out,
                  q_ref, v1_ref, v2_ref, rank1_ref, rank2_ref, counts_ref):
    q_ref[...] = _dot(wq_ref[...], ht_ref[...])
    ci = tab_ref[0]
    neg = tab_ref[1]

    def head(h, carry):
        r0 = pl.multiple_of(h * PEER_QDIM, PEER_QDIM)
        s1 = _dot(keys_ref[0], q_ref[pl.ds(r0, PEER_HALF), :].astype(BF16))
        s2 = _dot(keys_ref[1], q_ref[pl.ds(r0 + PEER_HALF, PEER_HALF), :].astype(BF16))

        def select(exact_ties):
            v1, rank1, n1 = _extract_top(s1, exact_ties)
            v2, rank2, n2 = _extract_top(s2, exact_ties)
            counts, n3 = _candidate_counts(v1, v2, ci, neg, exact_ties)
            v1_ref[...], v2_ref[...], counts_ref[...] = v1, v2, counts
            rank1_ref[...], rank2_ref[...] = rank1, rank2
            full = float(PEER_TOPK)
            return jnp.where((n1 == full) & (n2 == full) & (n3 == full), 0.0, 1.0)

        tied = jnp.max(select(False))

        @pl.when(tied > 0.0)
        def _():
            select(True)

        v1, v2, counts = v1_ref[...], v2_ref[...], counts_ref[...]
        rank1 = rank1_ref[...]
        used = jnp.sum(jnp.where(counts > 0.0, 1.0, 0.0), axis=0, keepdims=True)
        cnt = jnp.where(rank1 < used, 1.0, 0.0)
        for r in range(PEER_TOPK // 2):
            cnt = jnp.where(rank1 == float(r), counts[r:r + 1], cnt)
        e1 = jnp.exp(v1 - v1[0:1])
        e2 = jnp.exp(v2 - v2[0:1])
        inner = jnp.zeros(v1.shape, F32)
        for r in range(PEER_TOPK):
            inner += jnp.where(counts > float(r), e2[r:r + 1], 0.0)
        z = jnp.sum(e1 * inner, axis=0, keepdims=True)
        rows = pl.ds(pl.multiple_of(h * N_KEYS, N_KEYS), N_KEYS)
        cnt_out[rows, :] = cnt
        p1_out[rows, :] = jnp.exp(s1 - v1[0:1])
        half = pl.ds(pl.multiple_of(h * (N_KEYS // 2), N_KEYS // 2), N_KEYS // 2)
        rk_out[half, :] = pltpu.bitcast(rank2_ref[...].astype(BF16), jnp.uint32)
        p2 = pltpu.bitcast((jnp.exp(s2 - v2[0:1]) / z).astype(BF16), jnp.uint32)
        n_groups = p2.shape[1] // LANES
        for lg in range(n_groups):
            dst = _skewed_group(lg, n_groups) * LANES
            p2_out[half, dst:dst + LANES] = p2[:, lg * LANES:(lg + 1) * LANES]
        return carry

    lax.fori_loop(0, PEER_HEADS, head, 0)


def _peer_route(ht, lw):
    tokens = ht.shape[1]
    rt = min(ROUTE_TOKEN_TILE, tokens)
    tab = jnp.asarray(_peer_candidate_tables(rt))
    n_rows = PEER_HEADS * N_KEYS
    dense = lambda dt, rows: jax.ShapeDtypeStruct((rows, tokens), dt)
    out_spec = lambda rows: pl.BlockSpec((rows, rt), lambda i: (0, i))
    return pl.pallas_call(
        _route_kernel,
        out_shape=[dense(F32, n_rows), dense(F32, n_rows),
                   dense(jnp.uint32, n_rows // 2), dense(jnp.uint32, n_rows // 2)],
        grid=(tokens // rt,),
        in_specs=[pl.BlockSpec((D_MODEL, rt), lambda i: (0, i)),
                  _const_spec(lw["pwq"].shape), _const_spec(lw["pkeys"].shape), _const_spec(tab.shape)],
        out_specs=[out_spec(n_rows), out_spec(n_rows), out_spec(n_rows // 2), out_spec(n_rows // 2)],
        scratch_shapes=[pltpu.VMEM((PEER_HEADS * PEER_QDIM, rt), F32),
                        pltpu.VMEM((PEER_TOPK, rt), F32), pltpu.VMEM((PEER_TOPK, rt), F32),
                        pltpu.VMEM((N_KEYS, rt), F32), pltpu.VMEM((N_KEYS, rt), F32),
                        pltpu.VMEM((PEER_TOPK, rt), F32)],
        compiler_params=_params("parallel"),
        name="peer_route",
    )(ht, lw["pwq"], lw["pkeys"], tab)


def _bf16_parts(c):
    def rounded(v):
        bits = np.array([v], np.float32).view(np.uint32)
        bits = (bits + np.uint32(0x7FFF) + ((bits >> np.uint32(16)) & np.uint32(1))) & np.uint32(0xFFFF0000)
        return float(bits.view(np.float32)[0])
    hi = rounded(c)
    return hi, rounded(c - hi)


_GELU_K1 = -2.0 * 0.7978845608028654 * 1.4426950408889634
_GELU_K1_PARTS = _bf16_parts(_GELU_K1)
_GELU_K2_PARTS = _bf16_parts(_GELU_K1 * 0.044715)


def _gelu_tanh(x):
    t = x * x
    poly = (t * _GELU_K2_PARTS[0] + _GELU_K1_PARTS[0]) + (t * _GELU_K2_PARTS[1] + _GELU_K1_PARTS[1])
    return x / (1.0 + jnp.exp2(x * poly))


def _peer_kernel(alpha, ht_ref, u_ref, vt_ref, cnt_ref, p1_ref, rk_ref, p2_ref, x1_ref, mod_ref,
                 lg_ref, lb_ref, o_ref, acc_ref, a_ref, g_ref):
    j = pl.program_id(1)
    tb = ht_ref.shape[1]
    first_keys = u_ref.shape[0] // N_KEYS

    @pl.when(j == 0)
    def _():
        acc_ref[...] = jnp.zeros_like(acc_ref)

    head_rows = lambda h: pl.ds(pl.multiple_of(h * N_KEYS + j * first_keys, first_keys), first_keys)
    ht = ht_ref[...]
    n_chunks = first_keys // PEER_CHUNK_KEYS
    chunk_rows = lambda c: slice(c * PEER_CHUNK_KEYS * N_KEYS, (c + 1) * PEER_CHUNK_KEYS * N_KEYS)

    def activations(c):
        a_ref[chunk_rows(c), :] = _dot(u_ref[chunk_rows(c), :], ht)

    activations(0)
    for c in range(n_chunks):
        rows = chunk_rows(c)
        if c + 1 < n_chunks:
            activations(c + 1)
        for a in range(c * PEER_CHUNK_KEYS, (c + 1) * PEER_CHUNK_KEYS):
            for lg in range(tb // LANES):
                ls = slice(lg * LANES, (lg + 1) * LANES)
                lg2 = _skewed_group(lg, tb // LANES)
                ls2 = slice(lg2 * LANES, (lg2 + 1) * LANES)
                row_a = lambda ref, h: jnp.broadcast_to(ref[head_rows(h), ls][a:a + 1], (16, LANES)).astype(BF16)
                groups = range(N_KEYS // 16)
                w = [jnp.zeros((16, LANES), BF16) for _ in groups]
                for h in range(PEER_HEADS):
                    cb = row_a(cnt_ref, h)
                    pb = row_a(p1_ref, h)
                    for g in groups:
                        ks = slice(h * (N_KEYS // 2) + g * 8, h * (N_KEYS // 2) + (g + 1) * 8)
                        rk = pltpu.bitcast(rk_ref[ks, ls], BF16)
                        p2 = pltpu.bitcast(p2_ref[ks, ls2], BF16)
                        w[g] += jnp.where(rk < cb, p2, 0.0) * pb
                for g in groups:
                    er = slice(a * N_KEYS + g * 16, a * N_KEYS + (g + 1) * 16)
                    g_ref[er, ls] = w[g] * _gelu_tanh(a_ref[er, ls].astype(BF16))
        acc_ref[...] += _dot_tn(vt_ref[rows, :], g_ref[rows, :])

    @pl.when(j == pl.num_programs(1) - 1)
    def _():
        m = mod_ref[0]
        o_ref[...] = _layer_norm(alpha * x1_ref[...] + m[5:6] * acc_ref[...].T, lg_ref[...], lb_ref[...])


def _peer_dense(ht, route, x1, mod, lw, tables, layer, seq_len, alpha):
    tokens = x1.shape[0]
    tb = min(PEER_TOKEN_TILE, tokens)
    assert tb == min(ROUTE_TOKEN_TILE, tokens), "the p2 lane-group skew is per routing block"
    et = PEER_EXPERT_TILE
    assert (et // N_KEYS) % 8 == 0, "whole sublane tiles of first-key rows per expert tile"
    dense = pl.BlockSpec((PEER_HEADS * N_KEYS, tb), lambda i, j: (0, i))
    packed = pl.BlockSpec((PEER_HEADS * N_KEYS // 2, tb), lambda i, j: (0, i))
    return pl.pallas_call(
        functools.partial(_peer_kernel, alpha),
        out_shape=jax.ShapeDtypeStruct((tokens, D_MODEL), F32),
        grid=(tokens // tb, N_EXPERTS // et),
        in_specs=[pl.BlockSpec((D_MODEL, tb), lambda i, j: (0, i)),
                  pl.BlockSpec((None, et, D_MODEL), lambda i, j: (layer, j, 0)),
                  pl.BlockSpec((None, et, D_MODEL), lambda i, j: (layer, j, 0)),
                  dense, dense, packed, packed,
                  pl.BlockSpec((tb, D_MODEL), lambda i, j: (i, 0)),
                  pl.BlockSpec((1, 6, D_MODEL), lambda i, j: ((i * tb) // seq_len % mod.shape[0], 0, 0)),
                  _const_spec(lw["ln2g"].shape), _const_spec(lw["ln2b"].shape)],
        out_specs=pl.BlockSpec((tb, D_MODEL), lambda i, j: (i, 0)),
        scratch_shapes=[pltpu.VMEM((D_MODEL, tb), F32),
                        pltpu.VMEM((et, tb), F32),
                        pltpu.VMEM((et, tb), BF16)],
        compiler_params=_params("parallel", "arbitrary"),
        name="peer_dense",
    )(ht, *tables, *route, x1, mod, lw["ln2g"], lw["ln2b"])


def _rope_tables(seq_len):
    rows = seq_len // GRID_W
    r = jnp.repeat(jnp.arange(rows, dtype=F32), GRID_W)
    col = jnp.tile(jnp.arange(GRID_W, dtype=F32), rows)
    inv = ROPE_THETA ** (-jnp.arange(ROPE_FREQS, dtype=F32) / ROPE_FREQS)
    ang = jnp.stack([r[:, None] * inv, col[:, None] * inv], axis=1)
    cos, sin = jnp.cos(ang), jnp.sin(ang)
    cos_h = jnp.concatenate([cos, cos], axis=-1).reshape(seq_len, HEAD_DIM)
    sin_h = jnp.concatenate([-sin, sin], axis=-1).reshape(seq_len, HEAD_DIM)
    return jnp.tile(cos_h, (1, N_HEADS)), jnp.tile(sin_h, (1, N_HEADS))


def _layer_weights(l, w_in, q_norm, k_norm, gate_w2, gate_b, gla_norm, w_attn_o, w_gla_o, w_out,
                   ln1_g, ln1_b, ln2_g, ln2_b, peer_wq, peer_sub_keys, peer_u, peer_v):
    w = w_in[l].astype(BF16)
    o_q, o_k, o_g, o_lr, o_gm = 0, D_MODEL, D_MODEL + 2 * KV_WIDTH, 0, 0
    o_lr = o_g + 2 * GLA_QK_WIDTH + 2 * GLA_V_WIDTH
    o_gm = o_lr + 2 * GATE_RANK
    head_id = np.arange(D_MODEL) // HEAD_DIM
    mq = jnp.asarray((head_id[:, None] == head_id[None, :]).astype(np.float32) / HEAD_DIM, BF16)
    w2 = jnp.zeros((LANES, 2 * GLA_QK_WIDTH), F32)
    w2 = w2.at[:GATE_RANK, :GLA_QK_WIDTH].set(gate_w2[l, 0])
    w2 = w2.at[GATE_RANK:2 * GATE_RANK, GLA_QK_WIDTH:].set(gate_w2[l, 1])
    row = lambda a: a.reshape(1, -1)
    return dict(
        wq=w[:, o_q:o_k], wkv=w[:, o_k:o_g], wg=w[:, o_g:o_lr],
        wglr=jnp.pad(w[:, o_lr:o_gm], ((0, 0), (0, LANES - 2 * GATE_RANK))),
        wgm=w[:, o_gm:], mq=mq,
        qg=row(jnp.tile(q_norm[l], N_HEADS)), kg=row(jnp.tile(k_norm[l], KV_HEADS)),
        w2=w2.astype(BF16), gb=row(gate_b[l]),
        wa=w_attn_o[l].astype(BF16), wl=w_gla_o[l].astype(BF16), wo=w_out[l].astype(BF16),
        gn=row(jnp.tile(gla_norm[l], GLA_HEADS)),
        ln1g=row(ln1_g[l]), ln1b=row(ln1_b[l]), ln2g=row(ln2_g[l]), ln2b=row(ln2_b[l]),
        pwq=peer_wq[l].T.astype(BF16), pkeys=peer_sub_keys[l].astype(BF16),
    )


def _trunk_layer(x, mod, lw, batch, seq_len, alpha, consts, ctx, layer):
    rope_tabs = None if ctx is None else consts["rope"]
    q, k, v, gq, gk, gv, go, la, gm = _inproj(x, mod, lw, seq_len, rope_tabs)
    if ctx is None:
        attn = _attention(q, k, v, batch, seq_len, min(seq_len, 256), None, layer)
        og, states = _gla(gq, gk, gv, la, batch, seq_len, consts["tri"], None, layer)
    else:
        attn = _attention(q, k, v, batch, seq_len, 128, ctx[:2], layer)
        og, states = _gla(gq, gk, gv, la, batch, seq_len, consts["tri"], ctx[2], layer)
    x1, ht = _postmix(x, mod, attn, og, go, gm, lw, seq_len, alpha)
    route = _peer_route(ht, lw)
    x2 = _peer_dense(ht, route, x1, mod, lw, consts["tables"], layer, seq_len, alpha)
    return x2, (k, v, states)


def kernel(x_prompt, x_sample, cache_k, cache_v, state_gla, c, c_ctx, ada_w, ada_b, w_in, q_norm, k_norm,
           gate_w2, gate_b, gla_norm, w_attn_o, w_gla_o, w_out, ln1_g, ln1_b, ln2_g, ln2_b,
           peer_wq, peer_sub_keys, peer_u, peer_v):
    depth = ada_w.shape[0]
    alpha = (2.0 * depth) ** 0.25
    batch, seq, _ = x_prompt.shape
    dec_batch, dec_seq, _ = x_sample.shape
    past = cache_k.shape[2]

    n_cond = 1 + dec_batch
    cond = jnp.concatenate([c_ctx[None], c, jnp.zeros((-n_cond % 8, D_MODEL), F32)], axis=0)
    mod = _ada_mod(cond, ada_w, ada_b).reshape(depth, cond.shape[0], 6, D_MODEL)

    idx = np.arange(GLA_CHUNK)
    tri = jnp.asarray(np.stack([idx[None, :] <= idx[:, None], idx[None, :] >= idx[:, None]]), BF16)
    consts = dict(tri=tri, rope=_rope_tables(dec_seq), tables=(peer_u.astype(BF16), peer_v.astype(BF16)))
    weights = [_layer_weights(l, w_in, q_norm, k_norm, gate_w2, gate_b, gla_norm, w_attn_o, w_gla_o, w_out,
                              ln1_g, ln1_b, ln2_g, ln2_b, peer_wq, peer_sub_keys, peer_u, peer_v)
               for l in range(depth)]

    xp = x_prompt.reshape(batch * seq, D_MODEL)
    ks, vs, ss = [], [], []
    for l in range(depth):
        xp, (k_l, v_l, s_l) = _trunk_layer(xp, mod[l, 0:1], weights[l], batch, seq, alpha, consts, None, l)
        ks.append(k_l.reshape(batch, seq, KV_HEADS, HEAD_DIM))
        vs.append(v_l.reshape(batch, seq, KV_HEADS, HEAD_DIM))
        ss.append(s_l)
    new_cache_k = jnp.stack(ks, axis=1)
    new_cache_v = jnp.stack(vs, axis=1)
    new_state = jnp.stack(ss, axis=1)

    ctx = (cache_k.reshape(dec_batch, depth, past, KV_WIDTH), cache_v.reshape(dec_batch, depth, past, KV_WIDTH),
           state_gla)
    xs = x_sample.reshape(dec_batch * dec_seq, D_MODEL)
    for l in range(depth):
        xs, _ = _trunk_layer(xs, mod[l, 1:1 + dec_batch], weights[l], dec_batch, dec_seq, alpha, consts, ctx, l)

    return (xp.reshape(batch, seq, D_MODEL), xs.reshape(dec_batch, dec_seq, D_MODEL),
            new_cache_k, new_cache_v, new_state)
```

```python
import functools

import numpy as np
import jax
import jax.numpy as jnp
from jax import lax
from jax.experimental import pallas as pl
from jax.experimental.pallas import tpu as pltpu

F32 = jnp.float32
BF16 = jnp.bfloat16

D_MODEL = 1024
HEAD_DIM = 64
N_HEADS = D_MODEL // HEAD_DIM
KV_HEADS = N_HEADS // 4
Q_PER_KV = N_HEADS // KV_HEADS
KV_WIDTH = KV_HEADS * HEAD_DIM
GRID_W = 64
ROPE_FREQS = HEAD_DIM // 4
ROPE_THETA = 10000.0
GLA_HEADS = 4
GLA_DK = D_MODEL // 2 // GLA_HEADS
GLA_DV = D_MODEL // GLA_HEADS
GLA_QK_WIDTH = GLA_HEADS * GLA_DK
GLA_V_WIDTH = GLA_HEADS * GLA_DV
GATE_RANK = 16
GATE_NORM = 16.0
GLA_CHUNK = 64
N_KEYS = 128
N_EXPERTS = N_KEYS * N_KEYS
PEER_HEADS = 8
PEER_TOPK = 16
PEER_QDIM = 256
PEER_HALF = PEER_QDIM // 2
LN_EPS = 1e-5
RMS_EPS = 1e-6

V7X_VMEM_BYTES = 64 * 1024 * 1024
VMEM_LIMIT = V7X_VMEM_BYTES - 8 * 1024 * 1024
LANES = 128

TOKEN_TILE = 256
PEER_TOKEN_TILE = 512
ROUTE_TOKEN_TILE = 512
ROUTE_HEADS_PER_STEP = 2
PEER_EXPERT_TILE = 1024
PEER_CHUNK_KEYS = 4


def _dot(a, b):
    return jnp.dot(a, b, preferred_element_type=F32)


def _dot_nt(a, b):
    return lax.dot_general(a, b, (((1,), (1,)), ((), ())), preferred_element_type=F32)


def _dot_tn(a, b):
    return lax.dot_general(a, b, (((0,), (0,)), ((), ())), preferred_element_type=F32)


def _const_spec(shape):
    zeros = (0,) * len(shape)
    return pl.BlockSpec(shape, lambda *_: zeros)


def _params(*sem):
    return pltpu.CompilerParams(dimension_semantics=sem, vmem_limit_bytes=VMEM_LIMIT)


def _layer_norm(x, g, b):
    mu = jnp.mean(x, axis=-1, keepdims=True)
    xc = x - mu
    var = jnp.mean(xc * xc, axis=-1, keepdims=True)
    return xc * lax.rsqrt(var + LN_EPS) * g + b


def _ada_kernel(c_ref, w_ref, b_ref, o_ref):
    c = c_ref[...]
    s = c * jax.nn.sigmoid(c)
    o_ref[...] = _dot(s.astype(BF16), w_ref[...].astype(BF16)) + b_ref[...]


def _ada_mod(cond, ada_w, ada_b):
    depth = ada_w.shape[0]
    rows = cond.shape[0]
    return pl.pallas_call(
        _ada_kernel,
        out_shape=jax.ShapeDtypeStruct((depth, rows, 6 * D_MODEL), F32),
        grid=(depth, 6),
        in_specs=[
            pl.BlockSpec((rows, D_MODEL), lambda l, j: (0, 0)),
            pl.BlockSpec((None, D_MODEL, D_MODEL), lambda l, j: (l, 0, j)),
            pl.BlockSpec((None, 1, D_MODEL), lambda l, j: (l, 0, j)),
        ],
        out_specs=pl.BlockSpec((None, rows, D_MODEL), lambda l, j: (l, 0, j)),
        compiler_params=_params("parallel", "parallel"),
        name="ada_mod",
    )(cond, ada_w, ada_b.reshape(depth, 1, 6 * D_MODEL))


def _rope(t, cos, sin_signed):
    width = t.shape[-1]
    up = pltpu.roll(t, width - ROPE_FREQS, 1)
    dn = pltpu.roll(t, ROPE_FREQS, 1)
    lane = lax.broadcasted_iota(jnp.int32, t.shape, 1)
    partner = jnp.where((lane & ROPE_FREQS) == 0, up, dn)
    return t * cos + partner * sin_signed


def _inproj_kernel(rope, *refs):
    (x_ref, mod_ref, wq_ref, wkv_ref, wg_ref, wglr_ref, wgm_ref, mq_ref, qg_ref, kg_ref,
     w2_ref, gb_ref) = refs[:12]
    refs = refs[12:]
    if rope:
        cos_ref, sin_ref = refs[:2]
        refs = refs[2:]
    q_out, k_out, v_out, gq_out, gk_out, gv_out, go_out, la_out, gm_out = refs

    m = mod_ref[0]
    h = (x_ref[...] * (1.0 + m[1:2]) + m[0:1]).astype(BF16)

    q = _dot(h, wq_ref[...])
    qn = q * lax.rsqrt(_dot((q * q).astype(BF16), mq_ref[...]) + RMS_EPS) * qg_ref[...]
    kv = _dot(h, wkv_ref[...])
    k = kv[:, :KV_WIDTH]
    kn = k * lax.rsqrt(_dot((k * k).astype(BF16), mq_ref[:KV_WIDTH, :KV_WIDTH]) + RMS_EPS) * kg_ref[...]
    if rope:
        cos = cos_ref[...]
        sin = sin_ref[...]
        qn = _rope(qn, cos, sin)
        kn = _rope(kn, cos[:, :KV_WIDTH], sin[:, :KV_WIDTH])
    q_out[...] = qn * (HEAD_DIM ** -0.5)
    k_out[...] = kn
    v_out[...] = kv[:, KV_WIDTH:]

    g = _dot(h, wg_ref[...])
    gq_out[...] = g[:, :GLA_QK_WIDTH] * (GLA_DK ** -0.5)
    gk_out[...] = g[:, GLA_QK_WIDTH:2 * GLA_QK_WIDTH]
    gv_out[...] = g[:, 2 * GLA_QK_WIDTH:2 * GLA_QK_WIDTH + GLA_V_WIDTH]
    go_out[...] = g[:, 2 * GLA_QK_WIDTH + GLA_V_WIDTH:]

    glr = _dot(h, wglr_ref[...])
    z = _dot(glr.astype(BF16), w2_ref[...]) + gb_ref[...]
    la_out[...] = (jnp.minimum(z, 0.0) - jnp.log1p(jnp.exp(-jnp.abs(z)))) * (1.0 / GATE_NORM)
    gm_out[...] = jax.nn.sigmoid(_dot(h, wgm_ref[...]))


def _inproj(x, mod, lw, seq_len, rope_tabs):
    tokens = x.shape[0]
    tm = TOKEN_TILE
    rope = rope_tabs is not None
    row = lambda i: (i, 0)
    ins = [x, mod, lw["wq"], lw["wkv"], lw["wg"], lw["wglr"], lw["wgm"], lw["mq"], lw["qg"], lw["kg"],
           lw["w2"], lw["gb"]]
    in_specs = [pl.BlockSpec((tm, D_MODEL), row),
                pl.BlockSpec((1, 6, D_MODEL), lambda i: ((i * tm) // seq_len % mod.shape[0], 0, 0))]
    in_specs += [_const_spec(a.shape) for a in ins[2:]]
    if rope:
        per_seq = seq_len // tm
        ins += list(rope_tabs)
        in_specs += [pl.BlockSpec((tm, D_MODEL), lambda i: (i % per_seq, 0))] * 2
    widths = (D_MODEL, KV_WIDTH, KV_WIDTH, GLA_QK_WIDTH, GLA_QK_WIDTH, GLA_V_WIDTH, GLA_V_WIDTH,
              2 * GLA_QK_WIDTH, 2 * D_MODEL)
    return pl.pallas_call(
        functools.partial(_inproj_kernel, rope),
        out_shape=[jax.ShapeDtypeStruct((tokens, w), F32) for w in widths],
        grid=(tokens // tm,),
        in_specs=in_specs,
        out_specs=[pl.BlockSpec((tm, w), row) for w in widths],
        compiler_params=_params("parallel"),
        name="inproj_rope" if rope else "inproj",
    )(*ins)


def _attn_kernel(has_ctx, *refs):
    if has_ctx:
        q_ref, k_ref, v_ref, ck_ref, cv_ref, o_ref = refs
    else:
        q_ref, k_ref, v_ref, o_ref = refs
    tq = q_ref.shape[0]
    for g in range(KV_HEADS):
        gs = slice(g * HEAD_DIM, (g + 1) * HEAD_DIM)
        kg = k_ref[:, gs].astype(BF16)
        vg = v_ref[:, gs].astype(BF16)
        if has_ctx:
            kg = jnp.concatenate([kg, ck_ref[:, gs].astype(BF16)], axis=0)
            vg = jnp.concatenate([vg, cv_ref[:, gs].astype(BF16)], axis=0)
        heads = [q_ref[:, (Q_PER_KV * g + r) * HEAD_DIM:(Q_PER_KV * g + r + 1) * HEAD_DIM]
                 for r in range(Q_PER_KV)]
        qs = jnp.concatenate(heads, axis=0).astype(BF16)
        s = _dot_nt(qs, kg)
        p = jnp.exp(s - jnp.max(s, axis=-1, keepdims=True))
        o = _dot(p.astype(BF16), vg) / jnp.sum(p, axis=-1, keepdims=True)
        for r in range(Q_PER_KV):
            h0 = (Q_PER_KV * g + r) * HEAD_DIM
            o_ref[:, h0:h0 + HEAD_DIM] = o[r * tq:(r + 1) * tq]


def _attention(q, k, v, batch, seq_len, tq, ctx_kv, layer):
    tokens = q.shape[0]
    nq = seq_len // tq
    ins = [q, k, v]
    in_specs = [pl.BlockSpec((tq, D_MODEL), lambda b, i: (b * nq + i, 0)),
                pl.BlockSpec((seq_len, KV_WIDTH), lambda b, i: (b, 0)),
                pl.BlockSpec((seq_len, KV_WIDTH), lambda b, i: (b, 0))]
    if ctx_kv is not None:
        past = ctx_kv[0].shape[2]
        ins += list(ctx_kv)
        in_specs += [pl.BlockSpec((None, None, past, KV_WIDTH), lambda b, i: (b, layer, 0, 0))] * 2
    return pl.pallas_call(
        functools.partial(_attn_kernel, ctx_kv is not None),
        out_shape=jax.ShapeDtypeStruct((tokens, D_MODEL), F32),
        grid=(batch, nq),
        in_specs=in_specs,
        out_specs=pl.BlockSpec((tq, D_MODEL), lambda b, i: (b * nq + i, 0)),
        compiler_params=_params("parallel", "parallel"),
        name="attention_ctx" if ctx_kv is not None else "attention",
    )(*ins)


def _split3(x):
    hi = x.astype(BF16)
    r = x - hi.astype(F32)
    mid = r.astype(BF16)
    lo = (r - mid.astype(F32)).astype(BF16)
    return hi, mid, lo


def _gla_kernel(has_s0, *refs):
    gq_ref, gk_ref, gv_ref, la_ref, tri_ref = refs[:5]
    refs = refs[5:]
    if has_s0:
        s0_ref = refs[0]
        refs = refs[1:]
    o_ref, st_out, bf_ref, bb_ref, st_ref = refs
    seq_len = gq_ref.shape[0]
    n_chunks = seq_len // GLA_CHUNK
    tril = tri_ref[0]
    triu = tri_ref[1]

    def cumsum_chunk(c, carry):
        rows = pl.ds(pl.multiple_of(c * GLA_CHUNK, GLA_CHUNK), GLA_CHUNK)
        la = la_ref[rows, :]
        pf = _split3(la[:, :GLA_QK_WIDTH])
        pb = _split3(la[:, GLA_QK_WIDTH:])
        bf_ref[rows, :] = _dot(tril, pf[0]) + _dot(tril, pf[1]) + _dot(tril, pf[2])
        bb_ref[rows, :] = _dot(triu, pb[0]) + _dot(triu, pb[1]) + _dot(triu, pb[2])
        return carry

    lax.fori_loop(0, n_chunks, cumsum_chunk, 0)

    ri = lax.broadcasted_iota(jnp.int32, (GLA_CHUNK, GLA_CHUNK), 0)
    ci = lax.broadcasted_iota(jnp.int32, (GLA_CHUNK, GLA_CHUNK), 1)
    for d in range(2):
        for h in range(GLA_HEADS):
            if has_s0:
                st_ref[d * GLA_HEADS + h] = s0_ref[d, h].T
            else:
                st_ref[d * GLA_HEADS + h] = jnp.zeros((GLA_DV, GLA_DK), F32)
    o_ref[...] = jnp.zeros_like(o_ref)

    def step(i, carry):
        for d in range(2):
            b_ref = bf_ref if d == 0 else bb_ref
            keep = (ci <= ri) if d == 0 else (ci >= ri)
            c = i if d == 0 else n_chunks - 1 - i
            rows = pl.ds(pl.multiple_of(c * GLA_CHUNK, GLA_CHUNK), GLA_CHUNK)
            for h in range(GLA_HEADS):
                ks = slice(h * GLA_DK, (h + 1) * GLA_DK)
                vs = slice(h * GLA_DV, (h + 1) * GLA_DV)
                b = b_ref[rows, ks]
                bl = b[GLA_CHUNK - 1:GLA_CHUNK] if d == 0 else b[0:1]
                kk = gk_ref[rows, ks]
                v = gv_ref[rows, vs].astype(BF16)
                qe = (gq_ref[rows, ks] * jnp.exp(b)).astype(BF16)
                ke = (kk * jnp.exp(-b)).astype(BF16)
                kl = (kk * jnp.exp(bl - b)).astype(BF16)
                a = jnp.where(keep, _dot_nt(qe, ke), 0.0).astype(BF16)
                st = st_ref[d * GLA_HEADS + h]
                o_ref[rows, vs] += _dot(a, v) + _dot_nt(qe, st.astype(BF16))
                st_ref[d * GLA_HEADS + h] = st * jnp.exp(bl) + _dot_tn(v, kl)
        return carry

    lax.fori_loop(0, n_chunks, step, 0)
    for d in range(2):
        for h in range(GLA_HEADS):
            st_out[d, h] = st_ref[d * GLA_HEADS + h].T


def _gla(gq, gk, gv, la, batch, seq_len, tri, s0, layer):
    tokens = gq.shape[0]
    seq = lambda w: pl.BlockSpec((seq_len, w), lambda b: (b, 0))
    ins = [gq, gk, gv, la, tri]
    in_specs = [seq(GLA_QK_WIDTH), seq(GLA_QK_WIDTH), seq(GLA_V_WIDTH), seq(2 * GLA_QK_WIDTH),
                _const_spec(tri.shape)]
    if s0 is not None:
        ins.append(s0)
        in_specs.append(pl.BlockSpec((None, None, 2, GLA_HEADS, GLA_DK, GLA_DV),
                                     lambda b: (b, layer, 0, 0, 0, 0)))
    return pl.pallas_call(
        functools.partial(_gla_kernel, s0 is not None),
        out_shape=[jax.ShapeDtypeStruct((tokens, GLA_V_WIDTH), F32),
                   jax.ShapeDtypeStruct((batch, 2, GLA_HEADS, GLA_DK, GLA_DV), F32)],
        grid=(batch,),
        in_specs=in_specs,
        out_specs=[seq(GLA_V_WIDTH),
                   pl.BlockSpec((None, 2, GLA_HEADS, GLA_DK, GLA_DV), lambda b: (b, 0, 0, 0, 0))],
        scratch_shapes=[pltpu.VMEM((seq_len, GLA_QK_WIDTH), F32),
                        pltpu.VMEM((seq_len, GLA_QK_WIDTH), F32),
                        pltpu.VMEM((2 * GLA_HEADS, GLA_DV, GLA_DK), F32)],
        compiler_params=_params("parallel"),
        name="gla_s0" if s0 is not None else "gla",
    )(*ins)


def _postmix_kernel(alpha, x_ref, mod_ref, at_ref, og_ref, go_ref, gm_ref, wa_ref, wl_ref, wo_ref,
                    gn_ref, lg_ref, lb_ref, x1_ref, ht_ref):
    m = mod_ref[0]
    og = og_ref[...]
    parts = []
    for h in range(GLA_HEADS):
        oh = og[:, h * GLA_DV:(h + 1) * GLA_DV]
        parts.append(oh * lax.rsqrt(jnp.mean(oh * oh, axis=-1, keepdims=True) + RMS_EPS))
    go = go_ref[...]
    o = jnp.concatenate(parts, axis=-1) * gn_ref[...] * (go * jax.nn.sigmoid(go))
    gm = gm_ref[...]
    y = (gm[:, :D_MODEL] * _dot(at_ref[...].astype(BF16), wa_ref[...])
         + gm[:, D_MODEL:] * _dot(o.astype(BF16), wl_ref[...]))
    mix = _dot(y.astype(BF16), wo_ref[...])
    x1 = _layer_norm(alpha * x_ref[...] + m[2:3] * mix, lg_ref[...], lb_ref[...])
    x1_ref[...] = x1
    ht_ref[...] = (x1 * (1.0 + m[4:5]) + m[3:4]).T.astype(BF16)


def _postmix(x, mod, attn, og, go, gm, lw, seq_len, alpha):
    tokens = x.shape[0]
    tm = TOKEN_TILE
    row = lambda w: pl.BlockSpec((tm, w), lambda i: (i, 0))
    consts = [lw["wa"], lw["wl"], lw["wo"], lw["gn"], lw["ln1g"], lw["ln1b"]]
    return pl.pallas_call(
        functools.partial(_postmix_kernel, alpha),
        out_shape=[jax.ShapeDtypeStruct((tokens, D_MODEL), F32),
                   jax.ShapeDtypeStruct((D_MODEL, tokens), BF16)],
        grid=(tokens // tm,),
        in_specs=[row(D_MODEL),
                  pl.BlockSpec((1, 6, D_MODEL), lambda i: ((i * tm) // seq_len % mod.shape[0], 0, 0)),
                  row(D_MODEL), row(D_MODEL), row(D_MODEL), row(2 * D_MODEL)]
                 + [_const_spec(a.shape) for a in consts],
        out_specs=[row(D_MODEL), pl.BlockSpec((D_MODEL, tm), lambda i: (0, i))],
        compiler_params=_params("parallel"),
        name="postmix",
    )(x, mod, attn, og, go, gm, *consts)


def _peer_candidate_tables(lanes):
    groups = [[(0, r) for r in range(16)], [(r, 0) for r in range(16)]]
    for t in (1, 2, 3):
        groups.append([(t, r) for r in range(8)])
        if t < 3:
            groups.append([(r, t) for r in range(8)])
    seen = set()
    ci, neg = [], []
    for grp in groups:
        for (r1, r2) in grp:
            ok = (r1 + 1) * (r2 + 1) <= PEER_TOPK and (r1, r2) not in seen
            if ok:
                seen.add((r1, r2))
            ci.append(float(r1 * PEER_TOPK + r2) if ok else 1e9)
            neg.append(0.0 if ok else -np.inf)
    tab = np.stack([np.asarray(ci, np.float32), np.asarray(neg, np.float32)])
    return np.ascontiguousarray(np.broadcast_to(tab[:, :, None], tab.shape + (lanes,)))


def _extract_top(s, exact_ties):
    key = lax.broadcasted_iota(jnp.int32, s.shape, 0).astype(F32)
    slot = lax.broadcasted_iota(jnp.int32, (PEER_TOPK, s.shape[1]), 0)
    rank = jnp.full(s.shape, float(PEER_TOPK), F32)
    vals = jnp.zeros((PEER_TOPK, s.shape[1]), F32)
    for r in range(PEER_TOPK):
        m = jnp.max(s, axis=0, keepdims=True)
        hit = s == m
        if exact_ties:
            hit = key == jnp.min(jnp.where(hit, key, float(N_KEYS)), axis=0, keepdims=True)
        rank = jnp.where(hit, float(r), rank)
        s = jnp.where(hit, -jnp.inf, s)
        vals = jnp.where(slot == r, m, vals)
    ranked = jnp.sum(jnp.where(rank < float(PEER_TOPK), 1.0, 0.0), axis=0, keepdims=True)
    return vals, rank, ranked


def _candidate_counts(v1, v2, ci, neg, exact_ties):
    lo = slice(0, 8)
    cand = jnp.concatenate([
        v1[0:1] + v2, v1 + v2[0:1],
        v1[1:2] + v2[lo], v1[lo] + v2[1:2],
        v1[2:3] + v2[lo], v1[lo] + v2[2:3],
        v1[3:4] + v2[lo]], axis=0) + neg
    taken = jnp.zeros(cand.shape, F32)
    for _ in range(PEER_TOPK):
        m = jnp.max(cand, axis=0, keepdims=True)
        hit = cand == m
        if exact_ties:
            hit = ci == jnp.min(jnp.where(hit, ci, 2e9), axis=0, keepdims=True)
        taken = jnp.where(hit, 1.0, taken)
        cand = jnp.where(hit, -jnp.inf, cand)
    row_sum = lambda a, b: jnp.sum(taken[a:b], axis=0, keepdims=True)
    slot = lax.broadcasted_iota(jnp.int32, v1.shape, 0)
    counts = taken[16:32] + jnp.concatenate(
        [taken[40:48] + taken[56:64], jnp.zeros((8, v1.shape[1]), F32)], axis=0)
    counts += jnp.where(slot == 0, row_sum(0, 16), 0.0)
    counts += jnp.where(slot == 1, row_sum(32, 40), 0.0)
    counts += jnp.where(slot == 2, row_sum(48, 56), 0.0)
    counts += jnp.where(slot == 3, row_sum(64, 72), 0.0)
    return counts, jnp.sum(counts, axis=0, keepdims=True)


def _skewed_group(lane_group, n_groups):
    return (lane_group + 1) % n_groups


def _route_kernel(ht_ref, wq_ref, keys_ref, tab_ref, cnt_out, p1_out, rk_out, p2_out,
                  q_ref, v1_ref, v2_ref, rank1_ref, rank2_ref, counts_ref):
    q_ref[...] = _dot(wq_ref[...], ht_ref[...])
    ci = tab_ref[0]
    neg = tab_ref[1]

    def head_group(hg, carry):
        heads = [hg * ROUTE_HEADS_PER_STEP + i for i in range(ROUTE_HEADS_PER_STEP)]
        scores = []
        for h in heads:
            r0 = pl.multiple_of(h * PEER_QDIM, PEER_QDIM)
            scores.append((_dot(keys_ref[0], q_ref[pl.ds(r0, PEER_HALF), :].astype(BF16)),
                           _dot(keys_ref[1], q_ref[pl.ds(r0 + PEER_HALF, PEER_HALF), :].astype(BF16))))

        def select(exact_ties):
            tied = 0.0
            for i, (s1, s2) in enumerate(scores):
                v1, rank1, n1 = _extract_top(s1, exact_ties)
                v2, rank2, n2 = _extract_top(s2, exact_ties)
                counts, n3 = _candidate_counts(v1, v2, ci, neg, exact_ties)
                v1_ref[i], v2_ref[i], counts_ref[i] = v1, v2, counts
                rank1_ref[i], rank2_ref[i] = rank1, rank2
                full = float(PEER_TOPK)
                tied = jnp.maximum(tied, jnp.where((n1 == full) & (n2 == full) & (n3 == full), 0.0, 1.0))
            return tied

        tied = jnp.max(select(False))

        @pl.when(tied > 0.0)
        def _():
            select(True)

        for i, (h, (s1, s2)) in enumerate(zip(heads, scores)):
            v1, v2, counts = v1_ref[i], v2_ref[i], counts_ref[i]
            rank1 = rank1_ref[i]
            used = jnp.sum(jnp.where(counts > 0.0, 1.0, 0.0), axis=0, keepdims=True)
            cnt = jnp.where(rank1 < used, 1.0, 0.0)
            for r in range(PEER_TOPK // 2):
                cnt = jnp.where(rank1 == float(r), counts[r:r + 1], cnt)
            e1 = jnp.exp(v1 - v1[0:1])
            e2 = jnp.exp(v2 - v2[0:1])
            inner = jnp.zeros(v1.shape, F32)
            for r in range(PEER_TOPK):
                inner += jnp.where(counts > float(r), e2[r:r + 1], 0.0)
            z = jnp.sum(e1 * inner, axis=0, keepdims=True)
            rows = pl.ds(pl.multiple_of(h * N_KEYS, N_KEYS), N_KEYS)
            cnt_out[rows, :] = cnt
            p1_out[rows, :] = jnp.exp(s1 - v1[0:1])
            half = pl.ds(pl.multiple_of(h * (N_KEYS // 2), N_KEYS // 2), N_KEYS // 2)
            rk_out[half, :] = pltpu.bitcast(rank2_ref[i].astype(BF16), jnp.uint32)
            p2 = pltpu.bitcast((jnp.exp(s2 - v2[0:1]) / z).astype(BF16), jnp.uint32)
            n_groups = p2.shape[1] // LANES
            for lg in range(n_groups):
                dst = _skewed_group(lg, n_groups) * LANES
                p2_out[half, dst:dst + LANES] = p2[:, lg * LANES:(lg + 1) * LANES]
        return carry

    lax.fori_loop(0, PEER_HEADS // ROUTE_HEADS_PER_STEP, head_group, 0)


def _peer_route(ht, lw):
    tokens = ht.shape[1]
    rt = min(ROUTE_TOKEN_TILE, tokens)
    tab = jnp.asarray(_peer_candidate_tables(rt))
    n_rows = PEER_HEADS * N_KEYS
    dense = lambda dt, rows: jax.ShapeDtypeStruct((rows, tokens), dt)
    out_spec = lambda rows: pl.BlockSpec((rows, rt), lambda i: (0, i))
    return pl.pallas_call(
        _route_kernel,
        out_shape=[dense(F32, n_rows), dense(F32, n_rows),
                   dense(jnp.uint32, n_rows // 2), dense(jnp.uint32, n_rows // 2)],
        grid=(tokens // rt,),
        in_specs=[pl.BlockSpec((D_MODEL, rt), lambda i: (0, i)),
                  _const_spec(lw["pwq"].shape), _const_spec(lw["pkeys"].shape), _const_spec(tab.shape)],
        out_specs=[out_spec(n_rows), out_spec(n_rows), out_spec(n_rows // 2), out_spec(n_rows // 2)],
        scratch_shapes=[pltpu.VMEM((PEER_HEADS * PEER_QDIM, rt), F32),
                        pltpu.VMEM((ROUTE_HEADS_PER_STEP, PEER_TOPK, rt), F32),
                        pltpu.VMEM((ROUTE_HEADS_PER_STEP, PEER_TOPK, rt), F32),
                        pltpu.VMEM((ROUTE_HEADS_PER_STEP, N_KEYS, rt), F32),
                        pltpu.VMEM((ROUTE_HEADS_PER_STEP, N_KEYS, rt), F32),
                        pltpu.VMEM((ROUTE_HEADS_PER_STEP, PEER_TOPK, rt), F32)],
        compiler_params=_params("parallel"),
        name="peer_route",
    )(ht, lw["pwq"], lw["pkeys"], tab)


def _bf16_parts(c):
    def rounded(v):
        bits = np.array([v], np.float32).view(np.uint32)
        bits = (bits + np.uint32(0x7FFF) + ((bits >> np.uint32(16)) & np.uint32(1))) & np.uint32(0xFFFF0000)
        return float(bits.view(np.float32)[0])
    hi = rounded(c)
    return hi, rounded(c - hi)


_GELU_K1 = -2.0 * 0.7978845608028654 * 1.4426950408889634
_GELU_K1_PARTS = _bf16_parts(_GELU_K1)
_GELU_K2_PARTS = _bf16_parts(_GELU_K1 * 0.044715)


def _gelu_tanh(x):
    t = x * x
    poly = (t * _GELU_K2_PARTS[0] + _GELU_K1_PARTS[0]) + (t * _GELU_K2_PARTS[1] + _GELU_K1_PARTS[1])
    return x / (1.0 + jnp.exp2(x * poly))


def _peer_kernel(alpha, ht_ref, u_ref, vt_ref, cnt_ref, p1_ref, rk_ref, p2_ref, x1_ref, mod_ref,
                 lg_ref, lb_ref, o_ref, acc_ref, a_ref, g_ref):
    j = pl.program_id(1)
    tb = ht_ref.shape[1]
    first_keys = u_ref.shape[0] // N_KEYS

    @pl.when(j == 0)
    def _():
        acc_ref[...] = jnp.zeros_like(acc_ref)

    head_rows = lambda h: pl.ds(pl.multiple_of(h * N_KEYS + j * first_keys, first_keys), first_keys)
    ht = ht_ref[...]
    n_chunks = first_keys // PEER_CHUNK_KEYS
    chunk_rows = lambda c: slice(c * PEER_CHUNK_KEYS * N_KEYS, (c + 1) * PEER_CHUNK_KEYS * N_KEYS)

    def activations(c):
        a_ref[chunk_rows(c), :] = _dot(u_ref[chunk_rows(c), :], ht)

    activations(0)
    for c in range(n_chunks):
        rows = chunk_rows(c)
        if c + 1 < n_chunks:
            activations(c + 1)
        for a in range(c * PEER_CHUNK_KEYS, (c + 1) * PEER_CHUNK_KEYS):
            for lg in range(tb // LANES):
                ls = slice(lg * LANES, (lg + 1) * LANES)
                lg2 = _skewed_group(lg, tb // LANES)
                ls2 = slice(lg2 * LANES, (lg2 + 1) * LANES)
                row_a = lambda ref, h: jnp.broadcast_to(ref[head_rows(h), ls][a:a + 1], (16, LANES)).astype(BF16)
                groups = range(N_KEYS // 16)
                w = [jnp.zeros((16, LANES), BF16) for _ in groups]
                for h in range(PEER_HEADS):
                    cb = row_a(cnt_ref, h)
                    pb = row_a(p1_ref, h)
                    for g in groups:
                        ks = slice(h * (N_KEYS // 2) + g * 8, h * (N_KEYS // 2) + (g + 1) * 8)
                        rk = pltpu.bitcast(rk_ref[ks, ls], BF16)
                        p2 = pltpu.bitcast(p2_ref[ks, ls2], BF16)
                        w[g] += jnp.where(rk < cb, p2, 0.0) * pb
                for g in groups:
                    er = slice(a * N_KEYS + g * 16, a * N_KEYS + (g + 1) * 16)
                    g_ref[er, ls] = w[g] * _gelu_tanh(a_ref[er, ls].astype(BF16))
        acc_ref[...] += _dot_tn(vt_ref[rows, :], g_ref[rows, :])

    @pl.when(j == pl.num_programs(1) - 1)
    def _():
        m = mod_ref[0]
        o_ref[...] = _layer_norm(alpha * x1_ref[...] + m[5:6] * acc_ref[...].T, lg_ref[...], lb_ref[...])


def _peer_dense(ht, route, x1, mod, lw, tables, layer, seq_len, alpha):
    tokens = x1.shape[0]
    tb = min(PEER_TOKEN_TILE, tokens)
    assert tb == min(ROUTE_TOKEN_TILE, tokens), "the p2 lane-group skew is per routing block"
    et = PEER_EXPERT_TILE
    assert (et // N_KEYS) % 8 == 0, "whole sublane tiles of first-key rows per expert tile"
    dense = pl.BlockSpec((PEER_HEADS * N_KEYS, tb), lambda i, j: (0, i))
    packed = pl.BlockSpec((PEER_HEADS * N_KEYS // 2, tb), lambda i, j: (0, i))
    return pl.pallas_call(
        functools.partial(_peer_kernel, alpha),
        out_shape=jax.ShapeDtypeStruct((tokens, D_MODEL), F32),
        grid=(tokens // tb, N_EXPERTS // et),
        in_specs=[pl.BlockSpec((D_MODEL, tb), lambda i, j: (0, i)),
                  pl.BlockSpec((None, et, D_MODEL), lambda i, j: (layer, j, 0)),
                  pl.BlockSpec((None, et, D_MODEL), lambda i, j: (layer, j, 0)),
                  dense, dense, packed, packed,
                  pl.BlockSpec((tb, D_MODEL), lambda i, j: (i, 0)),
                  pl.BlockSpec((1, 6, D_MODEL), lambda i, j: ((i * tb) // seq_len % mod.shape[0], 0, 0)),
                  _const_spec(lw["ln2g"].shape), _const_spec(lw["ln2b"].shape)],
        out_specs=pl.BlockSpec((tb, D_MODEL), lambda i, j: (i, 0)),
        scratch_shapes=[pltpu.VMEM((D_MODEL, tb), F32),
                        pltpu.VMEM((et, tb), F32),
                        pltpu.VMEM((et, tb), BF16)],
        compiler_params=_params("parallel", "arbitrary"),
        name="peer_dense",
    )(ht, *tables, *route, x1, mod, lw["ln2g"], lw["ln2b"])


def _rope_tables(seq_len):
    rows = seq_len // GRID_W
    r = jnp.repeat(jnp.arange(rows, dtype=F32), GRID_W)
    col = jnp.tile(jnp.arange(GRID_W, dtype=F32), rows)
    inv = ROPE_THETA ** (-jnp.arange(ROPE_FREQS, dtype=F32) / ROPE_FREQS)
    ang = jnp.stack([r[:, None] * inv, col[:, None] * inv], axis=1)
    cos, sin = jnp.cos(ang), jnp.sin(ang)
    cos_h = jnp.concatenate([cos, cos], axis=-1).reshape(seq_len, HEAD_DIM)
    sin_h = jnp.concatenate([-sin, sin], axis=-1).reshape(seq_len, HEAD_DIM)
    return jnp.tile(cos_h, (1, N_HEADS)), jnp.tile(sin_h, (1, N_HEADS))


def _layer_weights(l, w_in, q_norm, k_norm, gate_w2, gate_b, gla_norm, w_attn_o, w_gla_o, w_out,
                   ln1_g, ln1_b, ln2_g, ln2_b, peer_wq, peer_sub_keys, peer_u, peer_v):
    w = w_in[l].astype(BF16)
    o_q, o_k, o_g, o_lr, o_gm = 0, D_MODEL, D_MODEL + 2 * KV_WIDTH, 0, 0
    o_lr = o_g + 2 * GLA_QK_WIDTH + 2 * GLA_V_WIDTH
    o_gm = o_lr + 2 * GATE_RANK
    head_id = np.arange(D_MODEL) // HEAD_DIM
    mq = jnp.asarray((head_id[:, None] == head_id[None, :]).astype(np.float32) / HEAD_DIM, BF16)
    w2 = jnp.zeros((LANES, 2 * GLA_QK_WIDTH), F32)
    w2 = w2.at[:GATE_RANK, :GLA_QK_WIDTH].set(gate_w2[l, 0])
    w2 = w2.at[GATE_RANK:2 * GATE_RANK, GLA_QK_WIDTH:].set(gate_w2[l, 1])
    row = lambda a: a.reshape(1, -1)
    return dict(
        wq=w[:, o_q:o_k], wkv=w[:, o_k:o_g], wg=w[:, o_g:o_lr],
        wglr=jnp.pad(w[:, o_lr:o_gm], ((0, 0), (0, LANES - 2 * GATE_RANK))),
        wgm=w[:, o_gm:], mq=mq,
        qg=row(jnp.tile(q_norm[l], N_HEADS)), kg=row(jnp.tile(k_norm[l], KV_HEADS)),
        w2=w2.astype(BF16), gb=row(gate_b[l]),
        wa=w_attn_o[l].astype(BF16), wl=w_gla_o[l].astype(BF16), wo=w_out[l].astype(BF16),
        gn=row(jnp.tile(gla_norm[l], GLA_HEADS)),
        ln1g=row(ln1_g[l]), ln1b=row(ln1_b[l]), ln2g=row(ln2_g[l]), ln2b=row(ln2_b[l]),
        pwq=peer_wq[l].T.astype(BF16), pkeys=peer_sub_keys[l].astype(BF16),
    )


def _trunk_layer(x, mod, lw, batch, seq_len, alpha, consts, ctx, layer):
    rope_tabs = None if ctx is None else consts["rope"]
    q, k, v, gq, gk, gv, go, la, gm = _inproj(x, mod, lw, seq_len, rope_tabs)
    if ctx is None:
        attn = _attention(q, k, v, batch, seq_len, min(seq_len, 256), None, layer)
        og, states = _gla(gq, gk, gv, la, batch, seq_len, consts["tri"], None, layer)
    else:
        attn = _attention(q, k, v, batch, seq_len, 128, ctx[:2], layer)
        og, states = _gla(gq, gk, gv, la, batch, seq_len, consts["tri"], ctx[2], layer)
    x1, ht = _postmix(x, mod, attn, og, go, gm, lw, seq_len, alpha)
    route = _peer_route(ht, lw)
    x2 = _peer_dense(ht, route, x1, mod, lw, consts["tables"], layer, seq_len, alpha)
    return x2, (k, v, states)


def kernel(x_prompt, x_sample, cache_k, cache_v, state_gla, c, c_ctx, ada_w, ada_b, w_in, q_norm, k_norm,
           gate_w2, gate_b, gla_norm, w_attn_o, w_gla_o, w_out, ln1_g, ln1_b, ln2_g, ln2_b,
           peer_wq, peer_sub_keys, peer_u, peer_v):
    depth = ada_w.shape[0]
    alpha = (2.0 * depth) ** 0.25
    batch, seq, _ = x_prompt.shape
    dec_batch, dec_seq, _ = x_sample.shape
    past = cache_k.shape[2]

    n_cond = 1 + dec_batch
    cond = jnp.concatenate([c_ctx[None], c, jnp.zeros((-n_cond % 8, D_MODEL), F32)], axis=0)
    mod = _ada_mod(cond, ada_w, ada_b).reshape(depth, cond.shape[0], 6, D_MODEL)

    idx = np.arange(GLA_CHUNK)
    tri = jnp.asarray(np.stack([idx[None, :] <= idx[:, None], idx[None, :] >= idx[:, None]]), BF16)
    consts = dict(tri=tri, rope=_rope_tables(dec_seq), tables=(peer_u.astype(BF16), peer_v.astype(BF16)))
    weights = [_layer_weights(l, w_in, q_norm, k_norm, gate_w2, gate_b, gla_norm, w_attn_o, w_gla_o, w_out,
                              ln1_g, ln1_b, ln2_g, ln2_b, peer_wq, peer_sub_keys, peer_u, peer_v)
               for l in range(depth)]

    xp = x_prompt.reshape(batch * seq, D_MODEL)
    ks, vs, ss = [], [], []
    for l in range(depth):
        xp, (k_l, v_l, s_l) = _trunk_layer(xp, mod[l, 0:1], weights[l], batch, seq, alpha, consts, None, l)
        ks.append(k_l.reshape(batch, seq, KV_HEADS, HEAD_DIM))
        vs.append(v_l.reshape(batch, seq, KV_HEADS, HEAD_DIM))
        ss.append(s_l)
    new_cache_k = jnp.stack(ks, axis=1)
    new_cache_v = jnp.stack(vs, axis=1)
    new_state = jnp.stack(ss, axis=1)

    ctx = (cache_k.reshape(dec_batch, depth, past, KV_WIDTH), cache_v.reshape(dec_batch, depth, past, KV_WIDTH),
           state_gla)
    xs = x_sample.reshape(dec_batch * dec_seq, D_MODEL)
    for l in range(depth):
        xs, _ = _trunk_layer(xs, mod[l, 1:1 + dec_batch], weights[l], dec_batch, dec_seq, alpha, consts, ctx, l)

    return (xp.reshape(batch, seq, D_MODEL), xs.reshape(dec_batch, dec_seq, D_MODEL),
            new_cache_k, new_cache_v, new_state)
```

```python
import functools

import numpy as np
import jax
import jax.numpy as jnp
from jax import lax
from jax.experimental import pallas as pl
from jax.experimental.pallas import tpu as pltpu

F32 = jnp.float32
BF16 = jnp.bfloat16

D_MODEL = 1024
HEAD_DIM = 64
N_HEADS = D_MODEL // HEAD_DIM
KV_HEADS = N_HEADS // 4
Q_PER_KV = N_HEADS // KV_HEADS
KV_WIDTH = KV_HEADS * HEAD_DIM
GRID_W = 64
ROPE_FREQS = HEAD_DIM // 4
ROPE_THETA = 10000.0
GLA_HEADS = 4
GLA_DK = D_MODEL // 2 // GLA_HEADS
GLA_DV = D_MODEL // GLA_HEADS
GLA_QK_WIDTH = GLA_HEADS * GLA_DK
GLA_V_WIDTH = GLA_HEADS * GLA_DV
GATE_RANK = 16
GATE_NORM = 16.0
GLA_CHUNK = 64
N_KEYS = 128
N_EXPERTS = N_KEYS * N_KEYS
PEER_HEADS = 8
PEER_TOPK = 16
PEER_QDIM = 256
PEER_HALF = PEER_QDIM // 2
LN_EPS = 1e-5
RMS_EPS = 1e-6

V7X_VMEM_BYTES = 64 * 1024 * 1024
VMEM_LIMIT = V7X_VMEM_BYTES - 8 * 1024 * 1024
LANES = 128

TOKEN_TILE = 256
PEER_TOKEN_TILE = 512
ROUTE_TOKEN_TILE = 512
PEER_EXPERT_TILE = 1024
PEER_CHUNK_KEYS = 4


def _dot(a, b):
    return jnp.dot(a, b, preferred_element_type=F32)


def _dot_nt(a, b):
    return lax.dot_general(a, b, (((1,), (1,)), ((), ())), preferred_element_type=F32)


def _dot_tn(a, b):
    return lax.dot_general(a, b, (((0,), (0,)), ((), ())), preferred_element_type=F32)


def _const_spec(shape):
    zeros = (0,) * len(shape)
    return pl.BlockSpec(shape, lambda *_: zeros)


def _params(*sem):
    return pltpu.CompilerParams(dimension_semantics=sem, vmem_limit_bytes=VMEM_LIMIT)


def _layer_norm(x, g, b):
    mu = jnp.mean(x, axis=-1, keepdims=True)
    xc = x - mu
    var = jnp.mean(xc * xc, axis=-1, keepdims=True)
    return xc * lax.rsqrt(var + LN_EPS) * g + b


def _ada_kernel(c_ref, w_ref, b_ref, o_ref):
    c = c_ref[...]
    s = c * jax.nn.sigmoid(c)
    o_ref[...] = _dot(s.astype(BF16), w_ref[...].astype(BF16)) + b_ref[...]


def _ada_mod(cond, ada_w, ada_b):
    depth = ada_w.shape[0]
    rows = cond.shape[0]
    return pl.pallas_call(
        _ada_kernel,
        out_shape=jax.ShapeDtypeStruct((depth, rows, 6 * D_MODEL), F32),
        grid=(depth, 6),
        in_specs=[
            pl.BlockSpec((rows, D_MODEL), lambda l, j: (0, 0)),
            pl.BlockSpec((None, D_MODEL, D_MODEL), lambda l, j: (l, 0, j)),
            pl.BlockSpec((None, 1, D_MODEL), lambda l, j: (l, 0, j)),
        ],
        out_specs=pl.BlockSpec((None, rows, D_MODEL), lambda l, j: (l, 0, j)),
        compiler_params=_params("parallel", "parallel"),
        name="ada_mod",
    )(cond, ada_w, ada_b.reshape(depth, 1, 6 * D_MODEL))


def _rope(t, cos, sin_signed):
    width = t.shape[-1]
    up = pltpu.roll(t, width - ROPE_FREQS, 1)
    dn = pltpu.roll(t, ROPE_FREQS, 1)
    lane = lax.broadcasted_iota(jnp.int32, t.shape, 1)
    partner = jnp.where((lane & ROPE_FREQS) == 0, up, dn)
    return t * cos + partner * sin_signed


def _inproj_kernel(rope, *refs):
    (x_ref, mod_ref, wq_ref, wkv_ref, wg_ref, wglr_ref, wgm_ref, mq_ref, qg_ref, kg_ref,
     w2_ref, gb_ref) = refs[:12]
    refs = refs[12:]
    if rope:
        cos_ref, sin_ref = refs[:2]
        refs = refs[2:]
    q_out, k_out, v_out, gq_out, gk_out, gv_out, go_out, la_out, gm_out = refs

    m = mod_ref[0]
    h = (x_ref[...] * (1.0 + m[1:2]) + m[0:1]).astype(BF16)

    q = _dot(h, wq_ref[...])
    qn = q * lax.rsqrt(_dot((q * q).astype(BF16), mq_ref[...]) + RMS_EPS) * qg_ref[...]
    kv = _dot(h, wkv_ref[...])
    k = kv[:, :KV_WIDTH]
    kn = k * lax.rsqrt(_dot((k * k).astype(BF16), mq_ref[:KV_WIDTH, :KV_WIDTH]) + RMS_EPS) * kg_ref[...]
    if rope:
        cos = cos_ref[...]
        sin = sin_ref[...]
        qn = _rope(qn, cos, sin)
        kn = _rope(kn, cos[:, :KV_WIDTH], sin[:, :KV_WIDTH])
    q_out[...] = qn * (HEAD_DIM ** -0.5)
    k_out[...] = kn
    v_out[...] = kv[:, KV_WIDTH:]

    g = _dot(h, wg_ref[...])
    gq_out[...] = g[:, :GLA_QK_WIDTH] * (GLA_DK ** -0.5)
    gk_out[...] = g[:, GLA_QK_WIDTH:2 * GLA_QK_WIDTH]
    gv_out[...] = g[:, 2 * GLA_QK_WIDTH:2 * GLA_QK_WIDTH + GLA_V_WIDTH]
    go_out[...] = g[:, 2 * GLA_QK_WIDTH + GLA_V_WIDTH:]

    glr = _dot(h, wglr_ref[...])
    z = _dot(glr.astype(BF16), w2_ref[...]) + gb_ref[...]
    la_out[...] = (jnp.minimum(z, 0.0) - jnp.log1p(jnp.exp(-jnp.abs(z)))) * (1.0 / GATE_NORM)
    gm_out[...] = jax.nn.sigmoid(_dot(h, wgm_ref[...]))


def _inproj(x, mod, lw, seq_len, rope_tabs):
    tokens = x.shape[0]
    tm = TOKEN_TILE
    rope = rope_tabs is not None
    row = lambda i: (i, 0)
    ins = [x, mod, lw["wq"], lw["wkv"], lw["wg"], lw["wglr"], lw["wgm"], lw["mq"], lw["qg"], lw["kg"],
           lw["w2"], lw["gb"]]
    in_specs = [pl.BlockSpec((tm, D_MODEL), row),
                pl.BlockSpec((1, 6, D_MODEL), lambda i: ((i * tm) // seq_len % mod.shape[0], 0, 0))]
    in_specs += [_const_spec(a.shape) for a in ins[2:]]
    if rope:
        per_seq = seq_len // tm
        ins += list(rope_tabs)
        in_specs += [pl.BlockSpec((tm, D_MODEL), lambda i: (i % per_seq, 0))] * 2
    widths = (D_MODEL, KV_WIDTH, KV_WIDTH, GLA_QK_WIDTH, GLA_QK_WIDTH, GLA_V_WIDTH, GLA_V_WIDTH,
              2 * GLA_QK_WIDTH, 2 * D_MODEL)
    return pl.pallas_call(
        functools.partial(_inproj_kernel, rope),
        out_shape=[jax.ShapeDtypeStruct((tokens, w), F32) for w in widths],
        grid=(tokens // tm,),
        in_specs=in_specs,
        out_specs=[pl.BlockSpec((tm, w), row) for w in widths],
        compiler_params=_params("parallel"),
        name="inproj_rope" if rope else "inproj",
    )(*ins)


def _attn_kernel(has_ctx, *refs):
    if has_ctx:
        q_ref, k_ref, v_ref, ck_ref, cv_ref, o_ref = refs
    else:
        q_ref, k_ref, v_ref, o_ref = refs
    tq = q_ref.shape[0]
    for g in range(KV_HEADS):
        gs = slice(g * HEAD_DIM, (g + 1) * HEAD_DIM)
        kg = k_ref[:, gs].astype(BF16)
        vg = v_ref[:, gs].astype(BF16)
        if has_ctx:
            kg = jnp.concatenate([kg, ck_ref[:, gs].astype(BF16)], axis=0)
            vg = jnp.concatenate([vg, cv_ref[:, gs].astype(BF16)], axis=0)
        heads = [q_ref[:, (Q_PER_KV * g + r) * HEAD_DIM:(Q_PER_KV * g + r + 1) * HEAD_DIM]
                 for r in range(Q_PER_KV)]
        qs = jnp.concatenate(heads, axis=0).astype(BF16)
        s = _dot_nt(qs, kg)
        p = jnp.exp(s - jnp.max(s, axis=-1, keepdims=True))
        o = _dot(p.astype(BF16), vg) / jnp.sum(p, axis=-1, keepdims=True)
        for r in range(Q_PER_KV):
            h0 = (Q_PER_KV * g + r) * HEAD_DIM
            o_ref[:, h0:h0 + HEAD_DIM] = o[r * tq:(r + 1) * tq]


def _attention(q, k, v, batch, seq_len, tq, ctx_kv, layer):
    tokens = q.shape[0]
    nq = seq_len // tq
    ins = [q, k, v]
    in_specs = [pl.BlockSpec((tq, D_MODEL), lambda b, i: (b * nq + i, 0)),
                pl.BlockSpec((seq_len, KV_WIDTH), lambda b, i: (b, 0)),
                pl.BlockSpec((seq_len, KV_WIDTH), lambda b, i: (b, 0))]
    if ctx_kv is not None:
        past = ctx_kv[0].shape[2]
        ins += list(ctx_kv)
        in_specs += [pl.BlockSpec((None, None, past, KV_WIDTH), lambda b, i: (b, layer, 0, 0))] * 2
    return pl.pallas_call(
        functools.partial(_attn_kernel, ctx_kv is not None),
        out_shape=jax.ShapeDtypeStruct((tokens, D_MODEL), F32),
        grid=(batch, nq),
        in_specs=in_specs,
        out_specs=pl.BlockSpec((tq, D_MODEL), lambda b, i: (b * nq + i, 0)),
        compiler_params=_params("parallel", "parallel"),
        name="attention_ctx" if ctx_kv is not None else "attention",
    )(*ins)


def _split3(x):
    hi = x.astype(BF16)
    r = x - hi.astype(F32)
    mid = r.astype(BF16)
    lo = (r - mid.astype(F32)).astype(BF16)
    return hi, mid, lo


def _gla_kernel(has_s0, *refs):
    gq_ref, gk_ref, gv_ref, la_ref, tri_ref = refs[:5]
    refs = refs[5:]
    if has_s0:
        s0_ref = refs[0]
        refs = refs[1:]
    o_ref, st_out, bf_ref, bb_ref, st_ref = refs
    seq_len = gq_ref.shape[0]
    n_chunks = seq_len // GLA_CHUNK
    tril = tri_ref[0]
    triu = tri_ref[1]

    def cumsum_chunk(c, carry):
        rows = pl.ds(pl.multiple_of(c * GLA_CHUNK, GLA_CHUNK), GLA_CHUNK)
        la = la_ref[rows, :]
        pf = _split3(la[:, :GLA_QK_WIDTH])
        pb = _split3(la[:, GLA_QK_WIDTH:])
        bf_ref[rows, :] = _dot(tril, pf[0]) + _dot(tril, pf[1]) + _dot(tril, pf[2])
        bb_ref[rows, :] = _dot(triu, pb[0]) + _dot(triu, pb[1]) + _dot(triu, pb[2])
        return carry

    lax.fori_loop(0, n_chunks, cumsum_chunk, 0)

    ri = lax.broadcasted_iota(jnp.int32, (GLA_CHUNK, GLA_CHUNK), 0)
    ci = lax.broadcasted_iota(jnp.int32, (GLA_CHUNK, GLA_CHUNK), 1)
    for d in range(2):
        for h in range(GLA_HEADS):
            if has_s0:
                st_ref[d * GLA_HEADS + h] = s0_ref[d, h].T
            else:
                st_ref[d * GLA_HEADS + h] = jnp.zeros((GLA_DV, GLA_DK), F32)
    o_ref[...] = jnp.zeros_like(o_ref)

    def step(i, carry):
        for d in range(2):
            b_ref = bf_ref if d == 0 else bb_ref
            keep = (ci <= ri) if d == 0 else (ci >= ri)
            c = i if d == 0 else n_chunks - 1 - i
            rows = pl.ds(pl.multiple_of(c * GLA_CHUNK, GLA_CHUNK), GLA_CHUNK)
            for h in range(GLA_HEADS):
                ks = slice(h * GLA_DK, (h + 1) * GLA_DK)
                vs = slice(h * GLA_DV, (h + 1) * GLA_DV)
                b = b_ref[rows, ks]
                bl = b[GLA_CHUNK - 1:GLA_CHUNK] if d == 0 else b[0:1]
                kk = gk_ref[rows, ks]
                v = gv_ref[rows, vs].astype(BF16)
                qe = (gq_ref[rows, ks] * jnp.exp(b)).astype(BF16)
                ke = (kk * jnp.exp(-b)).astype(BF16)
                kl = (kk * jnp.exp(bl - b)).astype(BF16)
                a = jnp.where(keep, _dot_nt(qe, ke), 0.0).astype(BF16)
                st = st_ref[d * GLA_HEADS + h]
                o_ref[rows, vs] += _dot(a, v) + _dot_nt(qe, st.astype(BF16))
                st_ref[d * GLA_HEADS + h] = st * jnp.exp(bl) + _dot_tn(v, kl)
        return carry

    lax.fori_loop(0, n_chunks, step, 0)
    for d in range(2):
        for h in range(GLA_HEADS):
            st_out[d, h] = st_ref[d * GLA_HEADS + h].T


def _gla(gq, gk, gv, la, batch, seq_len, tri, s0, layer):
    tokens = gq.shape[0]
    seq = lambda w: pl.BlockSpec((seq_len, w), lambda b: (b, 0))
    ins = [gq, gk, gv, la, tri]
    in_specs = [seq(GLA_QK_WIDTH), seq(GLA_QK_WIDTH), seq(GLA_V_WIDTH), seq(2 * GLA_QK_WIDTH),
                _const_spec(tri.shape)]
    if s0 is not None:
        ins.append(s0)
        in_specs.append(pl.BlockSpec((None, None, 2, GLA_HEADS, GLA_DK, GLA_DV),
                                     lambda b: (b, layer, 0, 0, 0, 0)))
    return pl.pallas_call(
        functools.partial(_gla_kernel, s0 is not None),
        out_shape=[jax.ShapeDtypeStruct((tokens, GLA_V_WIDTH), F32),
                   jax.ShapeDtypeStruct((batch, 2, GLA_HEADS, GLA_DK, GLA_DV), F32)],
        grid=(batch,),
        in_specs=in_specs,
        out_specs=[seq(GLA_V_WIDTH),
                   pl.BlockSpec((None, 2, GLA_HEADS, GLA_DK, GLA_DV), lambda b: (b, 0, 0, 0, 0))],
        scratch_shapes=[pltpu.VMEM((seq_len, GLA_QK_WIDTH), F32),
                        pltpu.VMEM((seq_len, GLA_QK_WIDTH), F32),
                        pltpu.VMEM((2 * GLA_HEADS, GLA_DV, GLA_DK), F32)],
        compiler_params=_params("parallel"),
        name="gla_s0" if s0 is not None else "gla",
    )(*ins)


def _postmix_kernel(alpha, x_ref, mod_ref, at_ref, og_ref, go_ref, gm_ref, wa_ref, wl_ref, wo_ref,
                    gn_ref, lg_ref, lb_ref, x1_ref, ht_ref):
    m = mod_ref[0]
    og = og_ref[...]
    parts = []
    for h in range(GLA_HEADS):
        oh = og[:, h * GLA_DV:(h + 1) * GLA_DV]
        parts.append(oh * lax.rsqrt(jnp.mean(oh * oh, axis=-1, keepdims=True) + RMS_EPS))
    go = go_ref[...]
    o = jnp.concatenate(parts, axis=-1) * gn_ref[...] * (go * jax.nn.sigmoid(go))
    gm = gm_ref[...]
    y = (gm[:, :D_MODEL] * _dot(at_ref[...].astype(BF16), wa_ref[...])
         + gm[:, D_MODEL:] * _dot(o.astype(BF16), wl_ref[...]))
    mix = _dot(y.astype(BF16), wo_ref[...])
    x1 = _layer_norm(alpha * x_ref[...] + m[2:3] * mix, lg_ref[...], lb_ref[...])
    x1_ref[...] = x1
    ht_ref[...] = (x1 * (1.0 + m[4:5]) + m[3:4]).T.astype(BF16)


def _postmix(x, mod, attn, og, go, gm, lw, seq_len, alpha):
    tokens = x.shape[0]
    tm = TOKEN_TILE
    row = lambda w: pl.BlockSpec((tm, w), lambda i: (i, 0))
    consts = [lw["wa"], lw["wl"], lw["wo"], lw["gn"], lw["ln1g"], lw["ln1b"]]
    return pl.pallas_call(
        functools.partial(_postmix_kernel, alpha),
        out_shape=[jax.ShapeDtypeStruct((tokens, D_MODEL), F32),
                   jax.ShapeDtypeStruct((D_MODEL, tokens), BF16)],
        grid=(tokens // tm,),
        in_specs=[row(D_MODEL),
                  pl.BlockSpec((1, 6, D_MODEL), lambda i: ((i * tm) // seq_len % mod.shape[0], 0, 0)),
                  row(D_MODEL), row(D_MODEL), row(D_MODEL), row(2 * D_MODEL)]
                 + [_const_spec(a.shape) for a in consts],
        out_specs=[row(D_MODEL), pl.BlockSpec((D_MODEL, tm), lambda i: (0, i))],
        compiler_params=_params("parallel"),
        name="postmix",
    )(x, mod, attn, og, go, gm, *consts)


def _peer_candidate_tables(lanes):
    groups = [[(0, r) for r in range(16)], [(r, 0) for r in range(16)]]
    for t in (1, 2, 3):
        groups.append([(t, r) for r in range(8)])
        if t < 3:
            groups.append([(r, t) for r in range(8)])
    seen = set()
    ci, neg = [], []
    for grp in groups:
        for (r1, r2) in grp:
            ok = (r1 + 1) * (r2 + 1) <= PEER_TOPK and (r1, r2) not in seen
            if ok:
                seen.add((r1, r2))
            ci.append(float(r1 * PEER_TOPK + r2) if ok else 1e9)
            neg.append(0.0 if ok else -np.inf)
    tab = np.stack([np.asarray(ci, np.float32), np.asarray(neg, np.float32)])
    return np.ascontiguousarray(np.broadcast_to(tab[:, :, None], tab.shape + (lanes,)))


def _extract_top(s, exact_ties):
    key = lax.broadcasted_iota(jnp.int32, s.shape, 0).astype(F32)
    slot = lax.broadcasted_iota(jnp.int32, (PEER_TOPK, s.shape[1]), 0)
    rank = jnp.full(s.shape, float(PEER_TOPK), F32)
    vals = jnp.zeros((PEER_TOPK, s.shape[1]), F32)
    for r in range(PEER_TOPK):
        m = jnp.max(s, axis=0, keepdims=True)
        hit = s == m
        if exact_ties:
            hit = key == jnp.min(jnp.where(hit, key, float(N_KEYS)), axis=0, keepdims=True)
        rank = jnp.where(hit, float(r), rank)
        s = jnp.where(hit, -jnp.inf, s)
        vals = jnp.where(slot == r, m, vals)
    ranked = jnp.sum(jnp.where(rank < float(PEER_TOPK), 1.0, 0.0), axis=0, keepdims=True)
    return vals, rank, ranked


def _candidate_counts(v1, v2, ci, neg, exact_ties):
    lo = slice(0, 8)
    cand = jnp.concatenate([
        v1[0:1] + v2, v1 + v2[0:1],
        v1[1:2] + v2[lo], v1[lo] + v2[1:2],
        v1[2:3] + v2[lo], v1[lo] + v2[2:3],
        v1[3:4] + v2[lo]], axis=0) + neg
    taken = jnp.zeros(cand.shape, F32)
    for _ in range(PEER_TOPK):
        m = jnp.max(cand, axis=0, keepdims=True)
        hit = cand == m
        if exact_ties:
            hit = ci == jnp.min(jnp.where(hit, ci, 2e9), axis=0, keepdims=True)
        taken = jnp.where(hit, 1.0, taken)
        cand = jnp.where(hit, -jnp.inf, cand)
    row_sum = lambda a, b: jnp.sum(taken[a:b], axis=0, keepdims=True)
    slot = lax.broadcasted_iota(jnp.int32, v1.shape, 0)
    counts = taken[16:32] + jnp.concatenate(
        [taken[40:48] + taken[56:64], jnp.zeros((8, v1.shape[1]), F32)], axis=0)
    counts += jnp.where(slot == 0, row_sum(0, 16), 0.0)
    counts += jnp.where(slot == 1, row_sum(32, 40), 0.0)
    counts += jnp.where(slot == 2, row_sum(48, 56), 0.0)
    counts += jnp.where(slot == 3, row_sum(64, 72), 0.0)
    return counts, jnp.sum(counts, axis=0, keepdims=True)


def _skewed_group(lane_group, n_groups):
    return (lane_group + 1) % n_groups


def _route_kernel(ht_ref, wq_ref, keys_ref, tab_ref, cnt_out, p1_out, rk_out, p2_out,
                  q_ref, v1_ref, v2_ref, rank1_ref, rank2_ref, counts_ref):
    q_ref[...] = _dot(wq_ref[...], ht_ref[...])
    ci = tab_ref[0]
    neg = tab_ref[1]

    def head(h, carry):
        r0 = pl.multiple_of(h * PEER_QDIM, PEER_QDIM)
        s1 = _dot(keys_ref[0], q_ref[pl.ds(r0, PEER_HALF), :].astype(BF16))
        s2 = _dot(keys_ref[1], q_ref[pl.ds(r0 + PEER_HALF, PEER_HALF), :].astype(BF16))

        def select(exact_ties):
            v1, rank1, n1 = _extract_top(s1, exact_ties)
            v2, rank2, n2 = _extract_top(s2, exact_ties)
            counts, n3 = _candidate_counts(v1, v2, ci, neg, exact_ties)
            v1_ref[...], v2_ref[...], counts_ref[...] = v1, v2, counts
            rank1_ref[...], rank2_ref[...] = rank1, rank2
            full = float(PEER_TOPK)
            return jnp.where((n1 == full) & (n2 == full) & (n3 == full), 0.0, 1.0)

        tied = jnp.max(select(False))

        @pl.when(tied > 0.0)
        def _():
            select(True)

        v1, v2, counts = v1_ref[...], v2_ref[...], counts_ref[...]
        rank1 = rank1_ref[...]
        used = jnp.sum(jnp.where(counts > 0.0, 1.0, 0.0), axis=0, keepdims=True)
        cnt = jnp.where(rank1 < used, 1.0, 0.0)
        for r in range(PEER_TOPK // 2):
            cnt = jnp.where(rank1 == float(r), counts[r:r + 1], cnt)
        e1 = jnp.exp(v1 - v1[0:1])
        e2 = jnp.exp(v2 - v2[0:1])
        inner = jnp.zeros(v1.shape, F32)
        for r in range(PEER_TOPK):
            inner += jnp.where(counts > float(r), e2[r:r + 1], 0.0)
        z = jnp.sum(e1 * inner, axis=0, keepdims=True)
        rows = pl.ds(pl.multiple_of(h * N_KEYS, N_KEYS), N_KEYS)
        cnt_out[rows, :] = cnt
        p1_out[rows, :] = jnp.exp(s1 - v1[0:1])
        half = pl.ds(pl.multiple_of(h * (N_KEYS // 2), N_KEYS // 2), N_KEYS // 2)
        rk_out[half, :] = pltpu.bitcast(rank2_ref[...].astype(BF16), jnp.uint32)
        p2 = pltpu.bitcast((jnp.exp(s2 - v2[0:1]) / z).astype(BF16), jnp.uint32)
        n_groups = p2.shape[1] // LANES
        for lg in range(n_groups):
            dst = _skewed_group(lg, n_groups) * LANES
            p2_out[half, dst:dst + LANES] = p2[:, lg * LANES:(lg + 1) * LANES]
        return carry

    lax.fori_loop(0, PEER_HEADS, head, 0)


def _peer_route(ht, lw):
    tokens = ht.shape[1]
    rt = min(ROUTE_TOKEN_TILE, tokens)
    tab = jnp.asarray(_peer_candidate_tables(rt))
    n_rows = PEER_HEADS * N_KEYS
    dense = lambda dt, rows: jax.ShapeDtypeStruct((rows, tokens), dt)
    out_spec = lambda rows: pl.BlockSpec((rows, rt), lambda i: (0, i))
    return pl.pallas_call(
        _route_kernel,
        out_shape=[dense(F32, n_rows), dense(F32, n_rows),
                   dense(jnp.uint32, n_rows // 2), dense(jnp.uint32, n_rows // 2)],
        grid=(tokens // rt,),
        in_specs=[pl.BlockSpec((D_MODEL, rt), lambda i: (0, i)),
                  _const_spec(lw["pwq"].shape), _const_spec(lw["pkeys"].shape), _const_spec(tab.shape)],
        out_specs=[out_spec(n_rows), out_spec(n_rows), out_spec(n_rows // 2), out_spec(n_rows // 2)],
        scratch_shapes=[pltpu.VMEM((PEER_HEADS * PEER_QDIM, rt), F32),
                        pltpu.VMEM((PEER_TOPK, rt), F32), pltpu.VMEM((PEER_TOPK, rt), F32),
                        pltpu.VMEM((N_KEYS, rt), F32), pltpu.VMEM((N_KEYS, rt), F32),
                        pltpu.VMEM((PEER_TOPK, rt), F32)],
        compiler_params=_params("parallel"),
        name="peer_route",
    )(ht, lw["pwq"], lw["pkeys"], tab)


def _bf16_parts(c):
    def rounded(v):
        bits = np.array([v], np.float32).view(np.uint32)
        bits = (bits + np.uint32(0x7FFF) + ((bits >> np.uint32(16)) & np.uint32(1))) & np.uint32(0xFFFF0000)
        return float(bits.view(np.float32)[0])
    hi = rounded(c)
    return hi, rounded(c - hi)


_GELU_K1 = -2.0 * 0.7978845608028654 * 1.4426950408889634
_GELU_K1_PARTS = _bf16_parts(_GELU_K1)
_GELU_K2_PARTS = _bf16_parts(_GELU_K1 * 0.044715)


def _gelu_tanh(x):
    t = x * x
    poly = (t * _GELU_K2_PARTS[0] + _GELU_K1_PARTS[0]) + (t * _GELU_K2_PARTS[1] + _GELU_K1_PARTS[1])
    return x / (1.0 + jnp.exp2(x * poly))


def _peer_kernel(alpha, ht_ref, u_ref, un_ref, vt_ref, cnt_ref, p1_ref, rk_ref, p2_ref, x1_ref, mod_ref,
                 lg_ref, lb_ref, o_ref, acc_ref, a_ref, g_ref):
    j = pl.program_id(1)
    tb = ht_ref.shape[1]
    first_keys = u_ref.shape[0] // N_KEYS
    n_chunks = first_keys // PEER_CHUNK_KEYS
    chunk_rows = lambda c: slice(c * PEER_CHUNK_KEYS * N_KEYS, (c + 1) * PEER_CHUNK_KEYS * N_KEYS)

    @pl.when(j == 0)
    def _():
        acc_ref[...] = jnp.zeros_like(acc_ref)
        a_ref[chunk_rows(0), :] = _dot(u_ref[chunk_rows(0), :], ht_ref[...])

    head_rows = lambda h: pl.ds(pl.multiple_of(h * N_KEYS + j * first_keys, first_keys), first_keys)
    ht = ht_ref[...]

    def activations(c):
        a_ref[chunk_rows(c), :] = _dot(u_ref[chunk_rows(c), :], ht)

    for c in range(n_chunks):
        rows = chunk_rows(c)
        if c + 1 < n_chunks:
            activations(c + 1)
        else:
            a_ref[chunk_rows(0), :] = _dot(un_ref[...], ht)
        for a in range(c * PEER_CHUNK_KEYS, (c + 1) * PEER_CHUNK_KEYS):
            for lg in range(tb // LANES):
                ls = slice(lg * LANES, (lg + 1) * LANES)
                lg2 = _skewed_group(lg, tb // LANES)
                ls2 = slice(lg2 * LANES, (lg2 + 1) * LANES)
                row_a = lambda ref, h: jnp.broadcast_to(ref[head_rows(h), ls][a:a + 1], (16, LANES)).astype(BF16)
                groups = range(N_KEYS // 16)
                w = [jnp.zeros((16, LANES), BF16) for _ in groups]
                for h in range(PEER_HEADS):
                    cb = row_a(cnt_ref, h)
                    pb = row_a(p1_ref, h)
                    for g in groups:
                        ks = slice(h * (N_KEYS // 2) + g * 8, h * (N_KEYS // 2) + (g + 1) * 8)
                        rk = pltpu.bitcast(rk_ref[ks, ls], BF16)
                        p2 = pltpu.bitcast(p2_ref[ks, ls2], BF16)
                        w[g] += jnp.where(rk < cb, p2, 0.0) * pb
                for g in groups:
                    er = slice(a * N_KEYS + g * 16, a * N_KEYS + (g + 1) * 16)
                    g_ref[er, ls] = w[g] * _gelu_tanh(a_ref[er, ls].astype(BF16))
        acc_ref[...] += _dot_tn(vt_ref[rows, :], g_ref[rows, :])

    @pl.when(j == pl.num_programs(1) - 1)
    def _():
        m = mod_ref[0]
        o_ref[...] = _layer_norm(alpha * x1_ref[...] + m[5:6] * acc_ref[...].T, lg_ref[...], lb_ref[...])


def _peer_dense(ht, route, x1, mod, lw, tables, layer, seq_len, alpha):
    tokens = x1.shape[0]
    tb = min(PEER_TOKEN_TILE, tokens)
    assert tb == min(ROUTE_TOKEN_TILE, tokens), "the p2 lane-group skew is per routing block"
    et = PEER_EXPERT_TILE
    n_tiles = N_EXPERTS // et
    chunk = PEER_CHUNK_KEYS * N_KEYS
    assert (et // N_KEYS) % 8 == 0, "whole sublane tiles of first-key rows per expert tile"
    dense = pl.BlockSpec((PEER_HEADS * N_KEYS, tb), lambda i, j: (0, i))
    packed = pl.BlockSpec((PEER_HEADS * N_KEYS // 2, tb), lambda i, j: (0, i))
    return pl.pallas_call(
        functools.partial(_peer_kernel, alpha),
        out_shape=jax.ShapeDtypeStruct((tokens, D_MODEL), F32),
        grid=(tokens // tb, N_EXPERTS // et),
        in_specs=[pl.BlockSpec((D_MODEL, tb), lambda i, j: (0, i)),
                  pl.BlockSpec((None, et, D_MODEL), lambda i, j: (layer, j, 0)),
                  pl.BlockSpec((None, chunk, D_MODEL),
                               lambda i, j: (layer, jnp.minimum(j + 1, n_tiles - 1) * (et // chunk), 0)),
                  pl.BlockSpec((None, et, D_MODEL), lambda i, j: (layer, j, 0)),
                  dense, dense, packed, packed,
                  pl.BlockSpec((tb, D_MODEL), lambda i, j: (i, 0)),
                  pl.BlockSpec((1, 6, D_MODEL), lambda i, j: ((i * tb) // seq_len % mod.shape[0], 0, 0)),
                  _const_spec(lw["ln2g"].shape), _const_spec(lw["ln2b"].shape)],
        out_specs=pl.BlockSpec((tb, D_MODEL), lambda i, j: (i, 0)),
        scratch_shapes=[pltpu.VMEM((D_MODEL, tb), F32),
                        pltpu.VMEM((et, tb), F32),
                        pltpu.VMEM((et, tb), BF16)],
        compiler_params=_params("parallel", "arbitrary"),
        name="peer_dense",
    )(ht, tables[0], tables[0], tables[1], *route, x1, mod, lw["ln2g"], lw["ln2b"])


def _rope_tables(seq_len):
    rows = seq_len // GRID_W
    r = jnp.repeat(jnp.arange(rows, dtype=F32), GRID_W)
    col = jnp.tile(jnp.arange(GRID_W, dtype=F32), rows)
    inv = ROPE_THETA ** (-jnp.arange(ROPE_FREQS, dtype=F32) / ROPE_FREQS)
    ang = jnp.stack([r[:, None] * inv, col[:, None] * inv], axis=1)
    cos, sin = jnp.cos(ang), jnp.sin(ang)
    cos_h = jnp.concatenate([cos, cos], axis=-1).reshape(seq_len, HEAD_DIM)
    sin_h = jnp.concatenate([-sin, sin], axis=-1).reshape(seq_len, HEAD_DIM)
    return jnp.tile(cos_h, (1, N_HEADS)), jnp.tile(sin_h, (1, N_HEADS))


def _layer_weights(l, w_in, q_norm, k_norm, gate_w2, gate_b, gla_norm, w_attn_o, w_gla_o, w_out,
                   ln1_g, ln1_b, ln2_g, ln2_b, peer_wq, peer_sub_keys, peer_u, peer_v):
    w = w_in[l].astype(BF16)
    o_q, o_k, o_g, o_lr, o_gm = 0, D_MODEL, D_MODEL + 2 * KV_WIDTH, 0, 0
    o_lr = o_g + 2 * GLA_QK_WIDTH + 2 * GLA_V_WIDTH
    o_gm = o_lr + 2 * GATE_RANK
    head_id = np.arange(D_MODEL) // HEAD_DIM
    mq = jnp.asarray((head_id[:, None] == head_id[None, :]).astype(np.float32) / HEAD_DIM, BF16)
    w2 = jnp.zeros((LANES, 2 * GLA_QK_WIDTH), F32)
    w2 = w2.at[:GATE_RANK, :GLA_QK_WIDTH].set(gate_w2[l, 0])
    w2 = w2.at[GATE_RANK:2 * GATE_RANK, GLA_QK_WIDTH:].set(gate_w2[l, 1])
    row = lambda a: a.reshape(1, -1)
    return dict(
        wq=w[:, o_q:o_k], wkv=w[:, o_k:o_g], wg=w[:, o_g:o_lr],
        wglr=jnp.pad(w[:, o_lr:o_gm], ((0, 0), (0, LANES - 2 * GATE_RANK))),
        wgm=w[:, o_gm:], mq=mq,
        qg=row(jnp.tile(q_norm[l], N_HEADS)), kg=row(jnp.tile(k_norm[l], KV_HEADS)),
        w2=w2.astype(BF16), gb=row(gate_b[l]),
        wa=w_attn_o[l].astype(BF16), wl=w_gla_o[l].astype(BF16), wo=w_out[l].astype(BF16),
        gn=row(jnp.tile(gla_norm[l], GLA_HEADS)),
        ln1g=row(ln1_g[l]), ln1b=row(ln1_b[l]), ln2g=row(ln2_g[l]), ln2b=row(ln2_b[l]),
        pwq=peer_wq[l].T.astype(BF16), pkeys=peer_sub_keys[l].astype(BF16),
    )


def _trunk_layer(x, mod, lw, batch, seq_len, alpha, consts, ctx, layer):
    rope_tabs = None if ctx is None else consts["rope"]
    q, k, v, gq, gk, gv, go, la, gm = _inproj(x, mod, lw, seq_len, rope_tabs)
    if ctx is None:
        attn = _attention(q, k, v, batch, seq_len, min(seq_len, 256), None, layer)
        og, states = _gla(gq, gk, gv, la, batch, seq_len, consts["tri"], None, layer)
    else:
        attn = _attention(q, k, v, batch, seq_len, 128, ctx[:2], layer)
        og, states = _gla(gq, gk, gv, la, batch, seq_len, consts["tri"], ctx[2], layer)
    x1, ht = _postmix(x, mod, attn, og, go, gm, lw, seq_len, alpha)
    route = _peer_route(ht, lw)
    x2 = _peer_dense(ht, route, x1, mod, lw, consts["tables"], layer, seq_len, alpha)
    return x2, (k, v, states)


def kernel(x_prompt, x_sample, cache_k, cache_v, state_gla, c, c_ctx, ada_w, ada_b, w_in, q_norm, k_norm,
           gate_w2, gate_b, gla_norm, w_attn_o, w_gla_o, w_out, ln1_g, ln1_b, ln2_g, ln2_b,
           peer_wq, peer_sub_keys, peer_u, peer_v):
    depth = ada_w.shape[0]
    alpha = (2.0 * depth) ** 0.25
    batch, seq, _ = x_prompt.shape
    dec_batch, dec_seq, _ = x_sample.shape
    past = cache_k.shape[2]

    n_cond = 1 + dec_batch
    cond = jnp.concatenate([c_ctx[None], c, jnp.zeros((-n_cond % 8, D_MODEL), F32)], axis=0)
    mod = _ada_mod(cond, ada_w, ada_b).reshape(depth, cond.shape[0], 6, D_MODEL)

    idx = np.arange(GLA_CHUNK)
    tri = jnp.asarray(np.stack([idx[None, :] <= idx[:, None], idx[None, :] >= idx[:, None]]), BF16)
    consts = dict(tri=tri, rope=_rope_tables(dec_seq), tables=(peer_u.astype(BF16), peer_v.astype(BF16)))
    weights = [_layer_weights(l, w_in, q_norm, k_norm, gate_w2, gate_b, gla_norm, w_attn_o, w_gla_o, w_out,
                              ln1_g, ln1_b, ln2_g, ln2_b, peer_wq, peer_sub_keys, peer_u, peer_v)
               for l in range(depth)]

    xp = x_prompt.reshape(batch * seq, D_MODEL)
    ks, vs, ss = [], [], []
    for l in range(depth):
        xp, (k_l, v_l, s_l) = _trunk_layer(xp, mod[l, 0:1], weights[l], batch, seq, alpha, consts, None, l)
        ks.append(k_l.reshape(batch, seq, KV_HEADS, HEAD_DIM))
        vs.append(v_l.reshape(batch, seq, KV_HEADS, HEAD_DIM))
        ss.append(s_l)
    new_cache_k = jnp.stack(ks, axis=1)
    new_cache_v = jnp.stack(vs, axis=1)
    new_state = jnp.stack(ss, axis=1)

    ctx = (cache_k.reshape(dec_batch, depth, past, KV_WIDTH), cache_v.reshape(dec_batch, depth, past, KV_WIDTH),
           state_gla)
    xs = x_sample.reshape(dec_batch * dec_seq, D_MODEL)
    for l in range(depth):
        xs, _ = _trunk_layer(xs, mod[l, 1:1 + dec_batch], weights[l], dec_batch, dec_seq, alpha, consts, ctx, l)

    return (xp.reshape(batch, seq, D_MODEL), xs.reshape(dec_batch, dec_seq, D_MODEL),
            new_cache_k, new_cache_v, new_state)
```

```python
import functools

import numpy as np
import jax
import jax.numpy as jnp
from jax import lax
from jax.experimental import pallas as pl
from jax.experimental.pallas import tpu as pltpu

F32 = jnp.float32
BF16 = jnp.bfloat16

D_MODEL = 1024
HEAD_DIM = 64
N_HEADS = D_MODEL // HEAD_DIM
KV_HEADS = N_HEADS // 4
Q_PER_KV = N_HEADS // KV_HEADS
KV_WIDTH = KV_HEADS * HEAD_DIM
GRID_W = 64
ROPE_FREQS = HEAD_DIM // 4
ROPE_THETA = 10000.0
GLA_HEADS = 4
GLA_DK = D_MODEL // 2 // GLA_HEADS
GLA_DV = D_MODEL // GLA_HEADS
GLA_QK_WIDTH = GLA_HEADS * GLA_DK
GLA_V_WIDTH = GLA_HEADS * GLA_DV
GATE_RANK = 16
GATE_NORM = 16.0
GLA_CHUNK = 64
N_KEYS = 128
N_EXPERTS = N_KEYS * N_KEYS
PEER_HEADS = 8
PEER_TOPK = 16
PEER_QDIM = 256
PEER_HALF = PEER_QDIM // 2
LN_EPS = 1e-5
RMS_EPS = 1e-6

V7X_VMEM_BYTES = 64 * 1024 * 1024
VMEM_LIMIT = V7X_VMEM_BYTES - 8 * 1024 * 1024
LANES = 128

TOKEN_TILE = 256
ATTN_KEY_BLOCK = 512
PEER_TOKEN_TILE = 512
ROUTE_TOKEN_TILE = 512
PEER_EXPERT_TILE = 1024
PEER_CHUNK_KEYS = 4


def _dot(a, b):
    return jnp.dot(a, b, preferred_element_type=F32)


def _dot_nt(a, b):
    return lax.dot_general(a, b, (((1,), (1,)), ((), ())), preferred_element_type=F32)


def _dot_tn(a, b):
    return lax.dot_general(a, b, (((0,), (0,)), ((), ())), preferred_element_type=F32)


def _const_spec(shape):
    zeros = (0,) * len(shape)
    return pl.BlockSpec(shape, lambda *_: zeros)


def _params(*sem):
    return pltpu.CompilerParams(dimension_semantics=sem, vmem_limit_bytes=VMEM_LIMIT)


def _layer_norm(x, g, b):
    mu = jnp.mean(x, axis=-1, keepdims=True)
    xc = x - mu
    var = jnp.mean(xc * xc, axis=-1, keepdims=True)
    return xc * lax.rsqrt(var + LN_EPS) * g + b


def _ada_kernel(c_ref, w_ref, b_ref, o_ref):
    c = c_ref[...]
    s = c * jax.nn.sigmoid(c)
    o_ref[...] = _dot(s.astype(BF16), w_ref[...].astype(BF16)) + b_ref[...]


def _ada_mod(cond, ada_w, ada_b):
    depth = ada_w.shape[0]
    rows = cond.shape[0]
    return pl.pallas_call(
        _ada_kernel,
        out_shape=jax.ShapeDtypeStruct((depth, rows, 6 * D_MODEL), F32),
        grid=(depth, 6),
        in_specs=[
            pl.BlockSpec((rows, D_MODEL), lambda l, j: (0, 0)),
            pl.BlockSpec((None, D_MODEL, D_MODEL), lambda l, j: (l, 0, j)),
            pl.BlockSpec((None, 1, D_MODEL), lambda l, j: (l, 0, j)),
        ],
        out_specs=pl.BlockSpec((None, rows, D_MODEL), lambda l, j: (l, 0, j)),
        compiler_params=_params("parallel", "parallel"),
        name="ada_mod",
    )(cond, ada_w, ada_b.reshape(depth, 1, 6 * D_MODEL))


def _rope(t, cos, sin_signed):
    width = t.shape[-1]
    up = pltpu.roll(t, width - ROPE_FREQS, 1)
    dn = pltpu.roll(t, ROPE_FREQS, 1)
    lane = lax.broadcasted_iota(jnp.int32, t.shape, 1)
    partner = jnp.where((lane & ROPE_FREQS) == 0, up, dn)
    return t * cos + partner * sin_signed


def _inproj_kernel(rope, *refs):
    (x_ref, mod_ref, wq_ref, wkv_ref, wg_ref, wglr_ref, wgm_ref, mq_ref, qg_ref, kg_ref,
     w2_ref, gb_ref) = refs[:12]
    refs = refs[12:]
    if rope:
        cos_ref, sin_ref = refs[:2]
        refs = refs[2:]
    q_out, k_out, v_out, gq_out, gk_out, gv_out, go_out, la_out, gm_out = refs

    m = mod_ref[0]
    h = (x_ref[...] * (1.0 + m[1:2]) + m[0:1]).astype(BF16)

    q = _dot(h, wq_ref[...])
    qn = q * lax.rsqrt(_dot((q * q).astype(BF16), mq_ref[...]) + RMS_EPS) * qg_ref[...]
    kv = _dot(h, wkv_ref[...])
    k = kv[:, :KV_WIDTH]
    kn = k * lax.rsqrt(_dot((k * k).astype(BF16), mq_ref[:KV_WIDTH, :KV_WIDTH]) + RMS_EPS) * kg_ref[...]
    if rope:
        cos = cos_ref[...]
        sin = sin_ref[...]
        qn = _rope(qn, cos, sin)
        kn = _rope(kn, cos[:, :KV_WIDTH], sin[:, :KV_WIDTH])
    q_out[...] = qn * (HEAD_DIM ** -0.5)
    k_out[...] = kn
    v_out[...] = kv[:, KV_WIDTH:]

    g = _dot(h, wg_ref[...])
    gq_out[...] = g[:, :GLA_QK_WIDTH] * (GLA_DK ** -0.5)
    gk_out[...] = g[:, GLA_QK_WIDTH:2 * GLA_QK_WIDTH]
    gv_out[...] = g[:, 2 * GLA_QK_WIDTH:2 * GLA_QK_WIDTH + GLA_V_WIDTH]
    go_out[...] = g[:, 2 * GLA_QK_WIDTH + GLA_V_WIDTH:]

    glr = _dot(h, wglr_ref[...])
    z = _dot(glr.astype(BF16), w2_ref[...]) + gb_ref[...]
    la_out[...] = (jnp.minimum(z, 0.0) - jnp.log1p(jnp.exp(-jnp.abs(z)))) * (1.0 / GATE_NORM)
    gm_out[...] = jax.nn.sigmoid(_dot(h, wgm_ref[...]))


def _inproj(x, mod, lw, seq_len, rope_tabs):
    tokens = x.shape[0]
    tm = TOKEN_TILE
    rope = rope_tabs is not None
    row = lambda i: (i, 0)
    ins = [x, mod, lw["wq"], lw["wkv"], lw["wg"], lw["wglr"], lw["wgm"], lw["mq"], lw["qg"], lw["kg"],
           lw["w2"], lw["gb"]]
    in_specs = [pl.BlockSpec((tm, D_MODEL), row),
                pl.BlockSpec((1, 6, D_MODEL), lambda i: ((i * tm) // seq_len % mod.shape[0], 0, 0))]
    in_specs += [_const_spec(a.shape) for a in ins[2:]]
    if rope:
        per_seq = seq_len // tm
        ins += list(rope_tabs)
        in_specs += [pl.BlockSpec((tm, D_MODEL), lambda i: (i % per_seq, 0))] * 2
    widths = (D_MODEL, KV_WIDTH, KV_WIDTH, GLA_QK_WIDTH, GLA_QK_WIDTH, GLA_V_WIDTH, GLA_V_WIDTH,
              2 * GLA_QK_WIDTH, 2 * D_MODEL)
    return pl.pallas_call(
        functools.partial(_inproj_kernel, rope),
        out_shape=[jax.ShapeDtypeStruct((tokens, w), F32) for w in widths],
        grid=(tokens // tm,),
        in_specs=in_specs,
        out_specs=[pl.BlockSpec((tm, w), row) for w in widths],
        compiler_params=_params("parallel"),
        name="inproj_rope" if rope else "inproj",
    )(*ins)


def _attn_kernel(has_ctx, *refs):
    if has_ctx:
        q_ref, k_ref, v_ref, ck_ref, cv_ref, o_ref = refs
    else:
        q_ref, k_ref, v_ref, o_ref = refs
    tq = q_ref.shape[0]
    blk = min(k_ref.shape[0], ATTN_KEY_BLOCK)
    sources = [(k_ref, v_ref, r0) for r0 in range(0, k_ref.shape[0], blk)]
    if has_ctx:
        sources += [(ck_ref, cv_ref, r0) for r0 in range(0, ck_ref.shape[0], blk)]
    for g in range(KV_HEADS):
        gs = slice(g * HEAD_DIM, (g + 1) * HEAD_DIM)
        heads = [q_ref[:, (Q_PER_KV * g + r) * HEAD_DIM:(Q_PER_KV * g + r + 1) * HEAD_DIM]
                 for r in range(Q_PER_KV)]
        qs = jnp.concatenate(heads, axis=0).astype(BF16)
        m = l = o = None
        for kr, vr, r0 in sources:
            s = _dot_nt(qs, kr[r0:r0 + blk, gs].astype(BF16))
            vb = vr[r0:r0 + blk, gs].astype(BF16)
            mb = jnp.max(s, axis=-1, keepdims=True)
            if m is None:
                m = mb
                p = jnp.exp(s - m)
                l = jnp.sum(p, axis=-1, keepdims=True)
                o = _dot(p.astype(BF16), vb)
            else:
                m_new = jnp.maximum(m, mb)
                corr = jnp.exp(m - m_new)
                p = jnp.exp(s - m_new)
                l = l * corr + jnp.sum(p, axis=-1, keepdims=True)
                o = o * corr + _dot(p.astype(BF16), vb)
                m = m_new
        o = o / l
        for r in range(Q_PER_KV):
            h0 = (Q_PER_KV * g + r) * HEAD_DIM
            o_ref[:, h0:h0 + HEAD_DIM] = o[r * tq:(r + 1) * tq]


def _attention(q, k, v, batch, seq_len, tq, ctx_kv, layer):
    tokens = q.shape[0]
    nq = seq_len // tq
    ins = [q, k, v]
    in_specs = [pl.BlockSpec((tq, D_MODEL), lambda b, i: (b * nq + i, 0)),
                pl.BlockSpec((seq_len, KV_WIDTH), lambda b, i: (b, 0)),
                pl.BlockSpec((seq_len, KV_WIDTH), lambda b, i: (b, 0))]
    if ctx_kv is not None:
        past = ctx_kv[0].shape[2]
        ins += list(ctx_kv)
        in_specs += [pl.BlockSpec((None, None, past, KV_WIDTH), lambda b, i: (b, layer, 0, 0))] * 2
    return pl.pallas_call(
        functools.partial(_attn_kernel, ctx_kv is not None),
        out_shape=jax.ShapeDtypeStruct((tokens, D_MODEL), F32),
        grid=(batch, nq),
        in_specs=in_specs,
        out_specs=pl.BlockSpec((tq, D_MODEL), lambda b, i: (b * nq + i, 0)),
        compiler_params=_params("parallel", "parallel"),
        name="attention_ctx" if ctx_kv is not None else "attention",
    )(*ins)


def _split3(x):
    hi = x.astype(BF16)
    r = x - hi.astype(F32)
    mid = r.astype(BF16)
    lo = (r - mid.astype(F32)).astype(BF16)
    return hi, mid, lo


def _gla_kernel(has_s0, *refs):
    gq_ref, gk_ref, gv_ref, la_ref, tri_ref = refs[:5]
    refs = refs[5:]
    if has_s0:
        s0_ref = refs[0]
        refs = refs[1:]
    o_ref, st_out, bf_ref, bb_ref, st_ref = refs
    seq_len = gq_ref.shape[0]
    n_chunks = seq_len // GLA_CHUNK
    tril = tri_ref[0]
    triu = tri_ref[1]

    def cumsum_chunk(c, carry):
        rows = pl.ds(pl.multiple_of(c * GLA_CHUNK, GLA_CHUNK), GLA_CHUNK)
        la = la_ref[rows, :]
        pf = _split3(la[:, :GLA_QK_WIDTH])
        pb = _split3(la[:, GLA_QK_WIDTH:])
        bf_ref[rows, :] = _dot(tril, pf[0]) + _dot(tril, pf[1]) + _dot(tril, pf[2])
        bb_ref[rows, :] = _dot(triu, pb[0]) + _dot(triu, pb[1]) + _dot(triu, pb[2])
        return carry

    lax.fori_loop(0, n_chunks, cumsum_chunk, 0)

    ri = lax.broadcasted_iota(jnp.int32, (GLA_CHUNK, GLA_CHUNK), 0)
    ci = lax.broadcasted_iota(jnp.int32, (GLA_CHUNK, GLA_CHUNK), 1)
    for d in range(2):
        for h in range(GLA_HEADS):
            if has_s0:
                st_ref[d * GLA_HEADS + h] = s0_ref[d, h].T
            else:
                st_ref[d * GLA_HEADS + h] = jnp.zeros((GLA_DV, GLA_DK), F32)
    o_ref[...] = jnp.zeros_like(o_ref)

    def step(i, carry):
        for d in range(2):
            b_ref = bf_ref if d == 0 else bb_ref
            keep = (ci <= ri) if d == 0 else (ci >= ri)
            c = i if d == 0 else n_chunks - 1 - i
            rows = pl.ds(pl.multiple_of(c * GLA_CHUNK, GLA_CHUNK), GLA_CHUNK)
            for h in range(GLA_HEADS):
                ks = slice(h * GLA_DK, (h + 1) * GLA_DK)
                vs = slice(h * GLA_DV, (h + 1) * GLA_DV)
                b = b_ref[rows, ks]
                bl = b[GLA_CHUNK - 1:GLA_CHUNK] if d == 0 else b[0:1]
                kk = gk_ref[rows, ks]
                v = gv_ref[rows, vs].astype(BF16)
                qe = (gq_ref[rows, ks] * jnp.exp(b)).astype(BF16)
                ke = (kk * jnp.exp(-b)).astype(BF16)
                kl = (kk * jnp.exp(bl - b)).astype(BF16)
                a = jnp.where(keep, _dot_nt(qe, ke), 0.0).astype(BF16)
                st = st_ref[d * GLA_HEADS + h]
                o_ref[rows, vs] += _dot(a, v) + _dot_nt(qe, st.astype(BF16))
                st_ref[d * GLA_HEADS + h] = st * jnp.exp(bl) + _dot_tn(v, kl)
        return carry

    lax.fori_loop(0, n_chunks, step, 0)
    for d in range(2):
        for h in range(GLA_HEADS):
            st_out[d, h] = st_ref[d * GLA_HEADS + h].T


def _gla(gq, gk, gv, la, batch, seq_len, tri, s0, layer):
    tokens = gq.shape[0]
    seq = lambda w: pl.BlockSpec((seq_len, w), lambda b: (b, 0))
    ins = [gq, gk, gv, la, tri]
    in_specs = [seq(GLA_QK_WIDTH), seq(GLA_QK_WIDTH), seq(GLA_V_WIDTH), seq(2 * GLA_QK_WIDTH),
                _const_spec(tri.shape)]
    if s0 is not None:
        ins.append(s0)
        in_specs.append(pl.BlockSpec((None, None, 2, GLA_HEADS, GLA_DK, GLA_DV),
                                     lambda b: (b, layer, 0, 0, 0, 0)))
    return pl.pallas_call(
        functools.partial(_gla_kernel, s0 is not None),
        out_shape=[jax.ShapeDtypeStruct((tokens, GLA_V_WIDTH), F32),
                   jax.ShapeDtypeStruct((batch, 2, GLA_HEADS, GLA_DK, GLA_DV), F32)],
        grid=(batch,),
        in_specs=in_specs,
        out_specs=[seq(GLA_V_WIDTH),
                   pl.BlockSpec((None, 2, GLA_HEADS, GLA_DK, GLA_DV), lambda b: (b, 0, 0, 0, 0))],
        scratch_shapes=[pltpu.VMEM((seq_len, GLA_QK_WIDTH), F32),
                        pltpu.VMEM((seq_len, GLA_QK_WIDTH), F32),
                        pltpu.VMEM((2 * GLA_HEADS, GLA_DV, GLA_DK), F32)],
        compiler_params=_params("parallel"),
        name="gla_s0" if s0 is not None else "gla",
    )(*ins)


def _postmix_kernel(alpha, x_ref, mod_ref, at_ref, og_ref, go_ref, gm_ref, wa_ref, wl_ref, wo_ref,
                    gn_ref, lg_ref, lb_ref, x1_ref, ht_ref):
    m = mod_ref[0]
    og = og_ref[...]
    parts = []
    for h in range(GLA_HEADS):
        oh = og[:, h * GLA_DV:(h + 1) * GLA_DV]
        parts.append(oh * lax.rsqrt(jnp.mean(oh * oh, axis=-1, keepdims=True) + RMS_EPS))
    go = go_ref[...]
    o = jnp.concatenate(parts, axis=-1) * gn_ref[...] * (go * jax.nn.sigmoid(go))
    gm = gm_ref[...]
    y = (gm[:, :D_MODEL] * _dot(at_ref[...].astype(BF16), wa_ref[...])
         + gm[:, D_MODEL:] * _dot(o.astype(BF16), wl_ref[...]))
    mix = _dot(y.astype(BF16), wo_ref[...])
    x1 = _layer_norm(alpha * x_ref[...] + m[2:3] * mix, lg_ref[...], lb_ref[...])
    x1_ref[...] = x1
    ht_ref[...] = (x1 * (1.0 + m[4:5]) + m[3:4]).T.astype(BF16)


def _postmix(x, mod, attn, og, go, gm, lw, seq_len, alpha):
    tokens = x.shape[0]
    tm = TOKEN_TILE
    row = lambda w: pl.BlockSpec((tm, w), lambda i: (i, 0))
    consts = [lw["wa"], lw["wl"], lw["wo"], lw["gn"], lw["ln1g"], lw["ln1b"]]
    return pl.pallas_call(
        functools.partial(_postmix_kernel, alpha),
        out_shape=[jax.ShapeDtypeStruct((tokens, D_MODEL), F32),
                   jax.ShapeDtypeStruct((D_MODEL, tokens), BF16)],
        grid=(tokens // tm,),
        in_specs=[row(D_MODEL),
                  pl.BlockSpec((1, 6, D_MODEL), lambda i: ((i * tm) // seq_len % mod.shape[0], 0, 0)),
                  row(D_MODEL), row(D_MODEL), row(D_MODEL), row(2 * D_MODEL)]
                 + [_const_spec(a.shape) for a in consts],
        out_specs=[row(D_MODEL), pl.BlockSpec((D_MODEL, tm), lambda i: (0, i))],
        compiler_params=_params("parallel"),
        name="postmix",
    )(x, mod, attn, og, go, gm, *consts)


def _peer_candidate_tables(lanes):
    groups = [[(0, r) for r in range(16)], [(r, 0) for r in range(16)]]
    for t in (1, 2, 3):
        groups.append([(t, r) for r in range(8)])
        if t < 3:
            groups.append([(r, t) for r in range(8)])
    seen = set()
    ci, neg = [], []
    for grp in groups:
        for (r1, r2) in grp:
            ok = (r1 + 1) * (r2 + 1) <= PEER_TOPK and (r1, r2) not in seen
            if ok:
                seen.add((r1, r2))
            ci.append(float(r1 * PEER_TOPK + r2) if ok else 1e9)
            neg.append(0.0 if ok else -np.inf)
    tab = np.stack([np.asarray(ci, np.float32), np.asarray(neg, np.float32)])
    return np.ascontiguousarray(np.broadcast_to(tab[:, :, None], tab.shape + (lanes,)))


def _extract_top(s, exact_ties):
    key = lax.broadcasted_iota(jnp.int32, s.shape, 0).astype(F32)
    slot = lax.broadcasted_iota(jnp.int32, (PEER_TOPK, s.shape[1]), 0)
    rank = jnp.full(s.shape, float(PEER_TOPK), F32)
    vals = jnp.zeros((PEER_TOPK, s.shape[1]), F32)
    for r in range(PEER_TOPK):
        m = jnp.max(s, axis=0, keepdims=True)
        hit = s == m
        if exact_ties:
            hit = key == jnp.min(jnp.where(hit, key, float(N_KEYS)), axis=0, keepdims=True)
        rank = jnp.where(hit, float(r), rank)
        s = jnp.where(hit, -jnp.inf, s)
        vals = jnp.where(slot == r, m, vals)
    ranked = jnp.sum(jnp.where(rank < float(PEER_TOPK), 1.0, 0.0), axis=0, keepdims=True)
    return vals, rank, ranked


def _candidate_counts(v1, v2, ci, neg, exact_ties):
    lo = slice(0, 8)
    cand = jnp.concatenate([
        v1[0:1] + v2, v1 + v2[0:1],
        v1[1:2] + v2[lo], v1[lo] + v2[1:2],
        v1[2:3] + v2[lo], v1[lo] + v2[2:3],
        v1[3:4] + v2[lo]], axis=0) + neg
    taken = jnp.zeros(cand.shape, F32)
    for _ in range(PEER_TOPK):
        m = jnp.max(cand, axis=0, keepdims=True)
        hit = cand == m
        if exact_ties:
            hit = ci == jnp.min(jnp.where(hit, ci, 2e9), axis=0, keepdims=True)
        taken = jnp.where(hit, 1.0, taken)
        cand = jnp.where(hit, -jnp.inf, cand)
    row_sum = lambda a, b: jnp.sum(taken[a:b], axis=0, keepdims=True)
    slot = lax.broadcasted_iota(jnp.int32, v1.shape, 0)
    counts = taken[16:32] + jnp.concatenate(
        [taken[40:48] + taken[56:64], jnp.zeros((8, v1.shape[1]), F32)], axis=0)
    counts += jnp.where(slot == 0, row_sum(0, 16), 0.0)
    counts += jnp.where(slot == 1, row_sum(32, 40), 0.0)
    counts += jnp.where(slot == 2, row_sum(48, 56), 0.0)
    counts += jnp.where(slot == 3, row_sum(64, 72), 0.0)
    return counts, jnp.sum(counts, axis=0, keepdims=True)


def _skewed_group(lane_group, n_groups):
    return (lane_group + 1) % n_groups


def _route_kernel(ht_ref, wq_ref, keys_ref, tab_ref, cnt_out, p1_out, rk_out, p2_out,
                  q_ref, v1_ref, v2_ref, rank1_ref, rank2_ref, counts_ref):
    q_ref[...] = _dot(wq_ref[...], ht_ref[...])
    ci = tab_ref[0]
    neg = tab_ref[1]

    def head(h, carry):
        r0 = pl.multiple_of(h * PEER_QDIM, PEER_QDIM)
        s1 = _dot(keys_ref[0], q_ref[pl.ds(r0, PEER_HALF), :].astype(BF16))
        s2 = _dot(keys_ref[1], q_ref[pl.ds(r0 + PEER_HALF, PEER_HALF), :].astype(BF16))

        def select(exact_ties):
            v1, rank1, n1 = _extract_top(s1, exact_ties)
            v2, rank2, n2 = _extract_top(s2, exact_ties)
            counts, n3 = _candidate_counts(v1, v2, ci, neg, exact_ties)
            v1_ref[...], v2_ref[...], counts_ref[...] = v1, v2, counts
            rank1_ref[...], rank2_ref[...] = rank1, rank2
            full = float(PEER_TOPK)
            return jnp.where((n1 == full) & (n2 == full) & (n3 == full), 0.0, 1.0)

        tied = jnp.max(select(False))

        @pl.when(tied > 0.0)
        def _():
            select(True)

        v1, v2, counts = v1_ref[...], v2_ref[...], counts_ref[...]
        rank1 = rank1_ref[...]
        used = jnp.sum(jnp.where(counts > 0.0, 1.0, 0.0), axis=0, keepdims=True)
        cnt = jnp.where(rank1 < used, 1.0, 0.0)
        for r in range(PEER_TOPK // 2):
            cnt = jnp.where(rank1 == float(r), counts[r:r + 1], cnt)
        e1 = jnp.exp(v1 - v1[0:1])
        e2 = jnp.exp(v2 - v2[0:1])
        inner = jnp.zeros(v1.shape, F32)
        for r in range(PEER_TOPK):
            inner += jnp.where(counts > float(r), e2[r:r + 1], 0.0)
        z = jnp.sum(e1 * inner, axis=0, keepdims=True)
        rows = pl.ds(pl.multiple_of(h * N_KEYS, N_KEYS), N_KEYS)
        cnt_out[rows, :] = cnt
        p1_out[rows, :] = jnp.exp(s1 - v1[0:1])
        half = pl.ds(pl.multiple_of(h * (N_KEYS // 2), N_KEYS // 2), N_KEYS // 2)
        rk_out[half, :] = pltpu.bitcast(rank2_ref[...].astype(BF16), jnp.uint32)
        p2 = pltpu.bitcast((jnp.exp(s2 - v2[0:1]) / z).astype(BF16), jnp.uint32)
        n_groups = p2.shape[1] // LANES
        for lg in range(n_groups):
            dst = _skewed_group(lg, n_groups) * LANES
            p2_out[half, dst:dst + LANES] = p2[:, lg * LANES:(lg + 1) * LANES]
        return carry

    lax.fori_loop(0, PEER_HEADS, head, 0)


def _peer_route(ht, lw):
    tokens = ht.shape[1]
    rt = min(ROUTE_TOKEN_TILE, tokens)
    tab = jnp.asarray(_peer_candidate_tables(rt))
    n_rows = PEER_HEADS * N_KEYS
    dense = lambda dt, rows: jax.ShapeDtypeStruct((rows, tokens), dt)
    out_spec = lambda rows: pl.BlockSpec((rows, rt), lambda i: (0, i))
    return pl.pallas_call(
        _route_kernel,
        out_shape=[dense(F32, n_rows), dense(F32, n_rows),
                   dense(jnp.uint32, n_rows // 2), dense(jnp.uint32, n_rows // 2)],
        grid=(tokens // rt,),
        in_specs=[pl.BlockSpec((D_MODEL, rt), lambda i: (0, i)),
                  _const_spec(lw["pwq"].shape), _const_spec(lw["pkeys"].shape), _const_spec(tab.shape)],
        out_specs=[out_spec(n_rows), out_spec(n_rows), out_spec(n_rows // 2), out_spec(n_rows // 2)],
        scratch_shapes=[pltpu.VMEM((PEER_HEADS * PEER_QDIM, rt), F32),
                        pltpu.VMEM((PEER_TOPK, rt), F32), pltpu.VMEM((PEER_TOPK, rt), F32),
                        pltpu.VMEM((N_KEYS, rt), F32), pltpu.VMEM((N_KEYS, rt), F32),
                        pltpu.VMEM((PEER_TOPK, rt), F32)],
        compiler_params=_params("parallel"),
        name="peer_route",
    )(ht, lw["pwq"], lw["pkeys"], tab)


def _bf16_parts(c):
    def rounded(v):
        bits = np.array([v], np.float32).view(np.uint32)
        bits = (bits + np.uint32(0x7FFF) + ((bits >> np.uint32(16)) & np.uint32(1))) & np.uint32(0xFFFF0000)
        return float(bits.view(np.float32)[0])
    hi = rounded(c)
    return hi, rounded(c - hi)


_GELU_K1 = -2.0 * 0.7978845608028654 * 1.4426950408889634
_GELU_K1_PARTS = _bf16_parts(_GELU_K1)
_GELU_K2_PARTS = _bf16_parts(_GELU_K1 * 0.044715)


def _gelu_tanh(x):
    t = x * x
    poly = (t * _GELU_K2_PARTS[0] + _GELU_K1_PARTS[0]) + (t * _GELU_K2_PARTS[1] + _GELU_K1_PARTS[1])
    return x / (1.0 + jnp.exp2(x * poly))


def _peer_kernel(alpha, ht_ref, u_ref, un_ref, vt_ref, cnt_ref, p1_ref, rk_ref, p2_ref, x1_ref, mod_ref,
                 lg_ref, lb_ref, o_ref, acc_ref, a_ref, g_ref):
    j = pl.program_id(1)
    tb = ht_ref.shape[1]
    first_keys = u_ref.shape[0] // N_KEYS
    n_chunks = first_keys // PEER_CHUNK_KEYS
    chunk_rows = lambda c: slice(c * PEER_CHUNK_KEYS * N_KEYS, (c + 1) * PEER_CHUNK_KEYS * N_KEYS)

    @pl.when(j == 0)
    def _():
        acc_ref[...] = jnp.zeros_like(acc_ref)
        a_ref[chunk_rows(0), :] = _dot(u_ref[chunk_rows(0), :], ht_ref[...])

    head_rows = lambda h: pl.ds(pl.multiple_of(h * N_KEYS + j * first_keys, first_keys), first_keys)
    ht = ht_ref[...]

    def activations(c):
        a_ref[chunk_rows(c), :] = _dot(u_ref[chunk_rows(c), :], ht)

    for c in range(n_chunks):
        rows = chunk_rows(c)
        if c + 1 < n_chunks:
            activations(c + 1)
        else:
            a_ref[chunk_rows(0), :] = _dot(un_ref[...], ht)
        for a in range(c * PEER_CHUNK_KEYS, (c + 1) * PEER_CHUNK_KEYS):
            for lg in range(tb // LANES):
                ls = slice(lg * LANES, (lg + 1) * LANES)
                lg2 = _skewed_group(lg, tb // LANES)
                ls2 = slice(lg2 * LANES, (lg2 + 1) * LANES)
                row_a = lambda ref, h: jnp.broadcast_to(ref[head_rows(h), ls][a:a + 1], (16, LANES)).astype(BF16)
                groups = range(N_KEYS // 16)
                w = [jnp.zeros((16, LANES), BF16) for _ in groups]
                for h in range(PEER_HEADS):
                    cb = row_a(cnt_ref, h)
                    pb = row_a(p1_ref, h)
                    for g in groups:
                        ks = slice(h * (N_KEYS // 2) + g * 8, h * (N_KEYS // 2) + (g + 1) * 8)
                        rk = pltpu.bitcast(rk_ref[ks, ls], BF16)
                        p2 = pltpu.bitcast(p2_ref[ks, ls2], BF16)
                        w[g] += jnp.where(rk < cb, p2, 0.0) * pb
                for g in groups:
                    er = slice(a * N_KEYS + g * 16, a * N_KEYS + (g + 1) * 16)
                    g_ref[er, ls] = w[g] * _gelu_tanh(a_ref[er, ls].astype(BF16))
        acc_ref[...] += _dot_tn(vt_ref[rows, :], g_ref[rows, :])

    @pl.when(j == pl.num_programs(1) - 1)
    def _():
        m = mod_ref[0]
        o_ref[...] = _layer_norm(alpha * x1_ref[...] + m[5:6] * acc_ref[...].T, lg_ref[...], lb_ref[...])


def _peer_dense(ht, route, x1, mod, lw, tables, layer, seq_len, alpha):
    tokens = x1.shape[0]
    tb = min(PEER_TOKEN_TILE, tokens)
    assert tb == min(ROUTE_TOKEN_TILE, tokens), "the p2 lane-group skew is per routing block"
    et = PEER_EXPERT_TILE
    n_tiles = N_EXPERTS // et
    chunk = PEER_CHUNK_KEYS * N_KEYS
    assert (et // N_KEYS) % 8 == 0, "whole sublane tiles of first-key rows per expert tile"
    dense = pl.BlockSpec((PEER_HEADS * N_KEYS, tb), lambda i, j: (0, i))
    packed = pl.BlockSpec((PEER_HEADS * N_KEYS // 2, tb), lambda i, j: (0, i))
    return pl.pallas_call(
        functools.partial(_peer_kernel, alpha),
        out_shape=jax.ShapeDtypeStruct((tokens, D_MODEL), F32),
        grid=(tokens // tb, N_EXPERTS // et),
        in_specs=[pl.BlockSpec((D_MODEL, tb), lambda i, j: (0, i)),
                  pl.BlockSpec((None, et, D_MODEL), lambda i, j: (layer, j, 0)),
                  pl.BlockSpec((None, chunk, D_MODEL),
                               lambda i, j: (layer, jnp.minimum(j + 1, n_tiles - 1) * (et // chunk), 0)),
                  pl.BlockSpec((None, et, D_MODEL), lambda i, j: (layer, j, 0)),
                  dense, dense, packed, packed,
                  pl.BlockSpec((tb, D_MODEL), lambda i, j: (i, 0)),
                  pl.BlockSpec((1, 6, D_MODEL), lambda i, j: ((i * tb) // seq_len % mod.shape[0], 0, 0)),
                  _const_spec(lw["ln2g"].shape), _const_spec(lw["ln2b"].shape)],
        out_specs=pl.BlockSpec((tb, D_MODEL), lambda i, j: (i, 0)),
        scratch_shapes=[pltpu.VMEM((D_MODEL, tb), F32),
                        pltpu.VMEM((et, tb), F32),
                        pltpu.VMEM((et, tb), BF16)],
        compiler_params=_params("parallel", "arbitrary"),
        name="peer_dense",
    )(ht, tables[0], tables[0], tables[1], *route, x1, mod, lw["ln2g"], lw["ln2b"])


def _rope_tables(seq_len):
    rows = seq_len // GRID_W
    r = jnp.repeat(jnp.arange(rows, dtype=F32), GRID_W)
    col = jnp.tile(jnp.arange(GRID_W, dtype=F32), rows)
    inv = ROPE_THETA ** (-jnp.arange(ROPE_FREQS, dtype=F32) / ROPE_FREQS)
    ang = jnp.stack([r[:, None] * inv, col[:, None] * inv], axis=1)
    cos, sin = jnp.cos(ang), jnp.sin(ang)
    cos_h = jnp.concatenate([cos, cos], axis=-1).reshape(seq_len, HEAD_DIM)
    sin_h = jnp.concatenate([-sin, sin], axis=-1).reshape(seq_len, HEAD_DIM)
    return jnp.tile(cos_h, (1, N_HEADS)), jnp.tile(sin_h, (1, N_HEADS))


def _layer_weights(l, w_in, q_norm, k_norm, gate_w2, gate_b, gla_norm, w_attn_o, w_gla_o, w_out,
                   ln1_g, ln1_b, ln2_g, ln2_b, peer_wq, peer_sub_keys, peer_u, peer_v):
    w = w_in[l].astype(BF16)
    o_q, o_k, o_g, o_lr, o_gm = 0, D_MODEL, D_MODEL + 2 * KV_WIDTH, 0, 0
    o_lr = o_g + 2 * GLA_QK_WIDTH + 2 * GLA_V_WIDTH
    o_gm = o_lr + 2 * GATE_RANK
    head_id = np.arange(D_MODEL) // HEAD_DIM
    mq = jnp.asarray((head_id[:, None] == head_id[None, :]).astype(np.float32) / HEAD_DIM, BF16)
    w2 = jnp.zeros((LANES, 2 * GLA_QK_WIDTH), F32)
    w2 = w2.at[:GATE_RANK, :GLA_QK_WIDTH].set(gate_w2[l, 0])
    w2 = w2.at[GATE_RANK:2 * GATE_RANK, GLA_QK_WIDTH:].set(gate_w2[l, 1])
    row = lambda a: a.reshape(1, -1)
    return dict(
        wq=w[:, o_q:o_k], wkv=w[:, o_k:o_g], wg=w[:, o_g:o_lr],
        wglr=jnp.pad(w[:, o_lr:o_gm], ((0, 0), (0, LANES - 2 * GATE_RANK))),
        wgm=w[:, o_gm:], mq=mq,
        qg=row(jnp.tile(q_norm[l], N_HEADS)), kg=row(jnp.tile(k_norm[l], KV_HEADS)),
        w2=w2.astype(BF16), gb=row(gate_b[l]),
        wa=w_attn_o[l].astype(BF16), wl=w_gla_o[l].astype(BF16), wo=w_out[l].astype(BF16),
        gn=row(jnp.tile(gla_norm[l], GLA_HEADS)),
        ln1g=row(ln1_g[l]), ln1b=row(ln1_b[l]), ln2g=row(ln2_g[l]), ln2b=row(ln2_b[l]),
        pwq=peer_wq[l].T.astype(BF16), pkeys=peer_sub_keys[l].astype(BF16),
    )


def _trunk_layer(x, mod, lw, batch, seq_len, alpha, consts, ctx, layer):
    rope_tabs = None if ctx is None else consts["rope"]
    q, k, v, gq, gk, gv, go, la, gm = _inproj(x, mod, lw, seq_len, rope_tabs)
    if ctx is None:
        attn = _attention(q, k, v, batch, seq_len, min(seq_len, 256), None, layer)
        og, states = _gla(gq, gk, gv, la, batch, seq_len, consts["tri"], None, layer)
    else:
        attn = _attention(q, k, v, batch, seq_len, 128, ctx[:2], layer)
        og, states = _gla(gq, gk, gv, la, batch, seq_len, consts["tri"], ctx[2], layer)
    x1, ht = _postmix(x, mod, attn, og, go, gm, lw, seq_len, alpha)
    route = _peer_route(ht, lw)
    x2 = _peer_dense(ht, route, x1, mod, lw, consts["tables"], layer, seq_len, alpha)
    return x2, (k, v, states)


def kernel(x_prompt, x_sample, cache_k, cache_v, state_gla, c, c_ctx, ada_w, ada_b, w_in, q_norm, k_norm,
           gate_w2, gate_b, gla_norm, w_attn_o, w_gla_o, w_out, ln1_g, ln1_b, ln2_g, ln2_b,
           peer_wq, peer_sub_keys, peer_u, peer_v):
    depth = ada_w.shape[0]
    alpha = (2.0 * depth) ** 0.25
    batch, seq, _ = x_prompt.shape
    dec_batch, dec_seq, _ = x_sample.shape
    past = cache_k.shape[2]

    n_cond = 1 + dec_batch
    cond = jnp.concatenate([c_ctx[None], c, jnp.zeros((-n_cond % 8, D_MODEL), F32)], axis=0)
    mod = _ada_mod(cond, ada_w, ada_b).reshape(depth, cond.shape[0], 6, D_MODEL)

    idx = np.arange(GLA_CHUNK)
    tri = jnp.asarray(np.stack([idx[None, :] <= idx[:, None], idx[None, :] >= idx[:, None]]), BF16)
    consts = dict(tri=tri, rope=_rope_tables(dec_seq), tables=(peer_u.astype(BF16), peer_v.astype(BF16)))
    weights = [_layer_weights(l, w_in, q_norm, k_norm, gate_w2, gate_b, gla_norm, w_attn_o, w_gla_o, w_out,
                              ln1_g, ln1_b, ln2_g, ln2_b, peer_wq, peer_sub_keys, peer_u, peer_v)
               for l in range(depth)]

    xp = x_prompt.reshape(batch * seq, D_MODEL)
    ks, vs, ss = [], [], []
    for l in range(depth):
        xp, (k_l, v_l, s_l) = _trunk_layer(xp, mod[l, 0:1], weights[l], batch, seq, alpha, consts, None, l)
        ks.append(k_l.reshape(batch, seq, KV_HEADS, HEAD_DIM))
        vs.append(v_l.reshape(batch, seq, KV_HEADS, HEAD_DIM))
        ss.append(s_l)
    new_cache_k = jnp.stack(ks, axis=1)
    new_cache_v = jnp.stack(vs, axis=1)
    new_state = jnp.stack(ss, axis=1)

    ctx = (cache_k.reshape(dec_batch, depth, past, KV_WIDTH), cache_v.reshape(dec_batch, depth, past, KV_WIDTH),
           state_gla)
    xs = x_sample.reshape(dec_batch * dec_seq, D_MODEL)
    for l in range(depth):
        xs, _ = _trunk_layer(xs, mod[l, 1:1 + dec_batch], weights[l], dec_batch, dec_seq, alpha, consts, ctx, l)

    return (xp.reshape(batch, seq, D_MODEL), xs.reshape(dec_batch, dec_seq, D_MODEL),
            new_cache_k, new_cache_v, new_state)
```
